```python
import math
import jax, jax.numpy as jnp
from jax import lax
import numpy as np

D_MODEL = 1024
BATCH = 4
SEQ = 8192
DEPTH = 2
DEC_BATCH = 16
DEC_SEQ = 64
PAST_LEN = 4096

CHUNK = 64
N_MIXERS = 2
N_SSM_LAYERS = (DEPTH + 1) // 2
N_FOX_LAYERS = DEPTH // 2
SSM_GROUP = 16
SSM_GROUPS = D_MODEL // SSM_GROUP
SSM_STATE = 64
SSM_DT_MIN = 1e-3
SSM_DT_MAX = 1e-1
FOX_HEADS = 16
FOX_HEAD_DIM = D_MODEL // FOX_HEADS
FOX_Q_BLOCK = 128
FOX_BIAS_INIT = 3.0
D_FF = 2816
CONV_WIDTH = 3
NORM_EPS = 1e-6
NEG_INF = -1e30

kernel_name = "hybrid_s5_fox_convffn_stream_step"


def rms_norm(x, g):
    x32 = x.astype(jnp.float32)
    y = x32 * lax.rsqrt(jnp.mean(x32 * x32, axis=-1, keepdims=True) + NORM_EPS)
    return (y * g.astype(jnp.float32)).astype(x.dtype)


def _cmul(ar, ai, br, bi):
    return ar * br - ai * bi, ar * bi + ai * br


def _ssm_combine(earlier, later):
    a1r, a1i, b1r, b1i = earlier
    a2r, a2i, b2r, b2i = later
    ar, ai = _cmul(a2r, a2i, a1r, a1i)
    br, bi = _cmul(a2r, a2i, b1r, b1i)
    return ar, ai, br + b2r, bi + b2i


def s5_mixer(u, h0_re, h0_im, a_re, a_im, log_step, b_re, b_im, c_re, c_im, d_skip, w_glu):
    n, l, _ = u.shape
    f32 = jnp.float32
    lam_re = a_re.astype(f32)
    lam_im = a_im.astype(f32)
    dt = jnp.exp(log_step.astype(f32))[:, None]
    mag = jnp.exp(lam_re * dt)
    abar_re = mag * jnp.cos(lam_im * dt)
    abar_im = mag * jnp.sin(lam_im * dt)
    den = lam_re * lam_re + lam_im * lam_im
    nr = abar_re - 1.0
    z_re = (nr * lam_re + abar_im * lam_im) / den
    z_im = (abar_im * lam_re - nr * lam_im) / den
    bb_re, bb_im = _cmul(z_re[..., None], z_im[..., None], b_re.astype(f32), b_im.astype(f32))
    ug = u.astype(f32).reshape(n, l, SSM_GROUPS, SSM_GROUP)
    bu_re = jnp.einsum('nlgh,gph->nlgp', ug, bb_re)
    bu_im = jnp.einsum('nlgh,gph->nlgp', ug, bb_im)
    h0r, h0i = _cmul(abar_re, abar_im, h0_re.astype(f32), h0_im.astype(f32))
    bu_re = bu_re.at[:, 0].add(h0r)
    bu_im = bu_im.at[:, 0].add(h0i)
    a_r = jnp.broadcast_to(abar_re, (1, l) + abar_re.shape)
    a_i = jnp.broadcast_to(abar_im, (1, l) + abar_im.shape)
    _, _, s_re, s_im = lax.associative_scan(_ssm_combine, (a_r, a_i, bu_re, bu_im), axis=1)
    y = (jnp.einsum('nlgp,ghp->nlgh', s_re, c_re.astype(f32))
         - jnp.einsum('nlgp,ghp->nlgh', s_im, c_im.astype(f32)))
    y = y.reshape(n, l, D_MODEL) + d_skip.astype(f32) * u.astype(f32)
    g = jax.nn.gelu(y).astype(u.dtype)
    val, gate = jnp.split(g @ w_glu, 2, axis=-1)
    return val * jax.nn.sigmoid(gate), s_re[:, -1], s_im[:, -1]


def fox_project(u, w_qkvf, b_f):
    n, l, _ = u.shape
    proj = u @ w_qkvf
    q = proj[..., :D_MODEL].reshape(n, l, FOX_HEADS, FOX_HEAD_DIM)
    k = proj[..., D_MODEL:2 * D_MODEL].reshape(n, l, FOX_HEADS, FOX_HEAD_DIM)
    v = proj[..., 2 * D_MODEL:3 * D_MODEL].reshape(n, l, FOX_HEADS, FOX_HEAD_DIM)
    logf = jax.nn.log_sigmoid(proj[..., 3 * D_MODEL:].astype(jnp.float32) + b_f.astype(jnp.float32))
    return q, k, v, logf


def fox_attend(q, k, v, c_q, c_k, q_pos, k_pos):
    s = jnp.einsum('nqhe,nkhe->nhqk', q, k, preferred_element_type=jnp.float32) * (FOX_HEAD_DIM ** -0.5)
    s = s + (jnp.transpose(c_q, (0, 2, 1))[..., :, None] - jnp.transpose(c_k, (0, 2, 1))[..., None, :])
    mask = k_pos[None, :] <= q_pos[:, None]
    s = jnp.where(mask, s, NEG_INF)
    p = jax.nn.softmax(s, axis=-1)
    return jnp.einsum('nhqk,nkhe->nqhe', p.astype(v.dtype), v)


def fox_prompt(u, w_qkvf, b_f, w_o):
    n, l, _ = u.shape
    q, k, v, logf = fox_project(u, w_qkvf, b_f)
    c = jnp.cumsum(logf, axis=1)
    nb = l // FOX_Q_BLOCK
    qb = q.reshape(n, nb, FOX_Q_BLOCK, FOX_HEADS, FOX_HEAD_DIM).transpose(1, 0, 2, 3, 4)
    cb = c.reshape(n, nb, FOX_Q_BLOCK, FOX_HEADS).transpose(1, 0, 2, 3)
    pos = jnp.arange(l)
    pb = pos.reshape(nb, FOX_Q_BLOCK)
    ob = lax.map(lambda blk: fox_attend(blk[0], k, v, blk[1], c, blk[2], pos), (qb, cb, pb))
    o = ob.transpose(1, 0, 2, 3, 4).reshape(n, l, D_MODEL)
    return o @ w_o, k, v, logf


def fox_sample(u, cache_k, cache_v, cache_logf, w_qkvf, b_f, w_o):
    n, t, _ = u.shape
    past = cache_k.shape[1]
    q, k, v, logf = fox_project(u, w_qkvf, b_f)
    k_all = jnp.concatenate([cache_k.astype(k.dtype), k], axis=1)
    v_all = jnp.concatenate([cache_v.astype(v.dtype), v], axis=1)
    c = jnp.cumsum(jnp.concatenate([cache_logf.astype(jnp.float32), logf], axis=1), axis=1)
    k_pos = jnp.arange(past + t)
    q_pos = past + jnp.arange(t)
    o = fox_attend(q, k_all, v_all, c[:, past:], c, q_pos, k_pos).reshape(n, t, D_MODEL)
    return o @ w_o, k, v, logf


def conv_ffn(x, hist, w_up, w_gate, conv_w, conv_b, w_down):
    l = x.shape[1]
    a = x @ w_up
    a_pad = jnp.concatenate([hist.astype(a.dtype), a], axis=1)
    conv = a_pad[:, 0:l] * conv_w[0]
    for j in range(1, CONV_WIDTH):
        conv = conv + a_pad[:, j:j + l] * conv_w[j]
    h = jax.nn.gelu(conv + conv_b) * (x @ w_gate)
    return h @ w_down, a_pad[:, -(CONV_WIDTH - 1):]


def setup_inputs(seed: int = 0) -> dict:
    key = jax.random.key(seed)
    ks = jax.random.split(key, 32)
    f32 = jnp.float32

    def nrm(k, shape, scale):
        return scale * jax.random.normal(k, shape, f32)

    n_idx = jnp.arange(SSM_STATE, dtype=f32)
    ssm_shape = (N_SSM_LAYERS, SSM_GROUPS, SSM_STATE)
    return {
        "x_prompt": nrm(ks[0], (BATCH, SEQ, D_MODEL), 1.0),
        "x_sample": nrm(ks[1], (DEC_BATCH, DEC_SEQ, D_MODEL), 1.0),
        "state_ssm_re": nrm(ks[2], (N_SSM_LAYERS, DEC_BATCH, SSM_GROUPS, SSM_STATE), 0.1),
        "state_ssm_im": nrm(ks[3], (N_SSM_LAYERS, DEC_BATCH, SSM_GROUPS, SSM_STATE), 0.1),
        "cache_fox_k": nrm(ks[4], (N_FOX_LAYERS, DEC_BATCH, PAST_LEN, FOX_HEADS, FOX_HEAD_DIM), 1.0),
        "cache_fox_v": nrm(ks[5], (N_FOX_LAYERS, DEC_BATCH, PAST_LEN, FOX_HEADS, FOX_HEAD_DIM), 1.0),
        "cache_fox_logf": jax.nn.log_sigmoid(FOX_BIAS_INIT + nrm(ks[6], (N_FOX_LAYERS, DEC_BATCH, PAST_LEN, FOX_HEADS), 1.0)),
        "state_ffn_conv": nrm(ks[7], (DEPTH, DEC_BATCH, CONV_WIDTH - 1, D_FF), 1.0),
        "norm_mix": 1.0 + nrm(ks[8], (DEPTH, D_MODEL), 0.05),
        "norm_ffn": 1.0 + nrm(ks[9], (DEPTH, D_MODEL), 0.05),
        "norm_final": 1.0 + nrm(ks[10], (D_MODEL,), 0.05),
        "ssm_a_re": -0.5 + nrm(ks[11], ssm_shape, 0.01),
        "ssm_a_im": jnp.broadcast_to(np.pi * n_idx, ssm_shape) + nrm(ks[12], ssm_shape, 0.01),
        "ssm_log_step": jax.random.uniform(ks[13], (N_SSM_LAYERS, SSM_GROUPS), f32,
                                           minval=math.log(SSM_DT_MIN), maxval=math.log(SSM_DT_MAX)),
        "ssm_b_re": nrm(ks[14], (N_SSM_LAYERS, SSM_GROUPS, SSM_STATE, SSM_GROUP), (2 * SSM_GROUP) ** -0.5),
        "ssm_b_im": nrm(ks[15], (N_SSM_LAYERS, SSM_GROUPS, SSM_STATE, SSM_GROUP), (2 * SSM_GROUP) ** -0.5),
        "ssm_c_re": nrm(ks[16], (N_SSM_LAYERS, SSM_GROUPS, SSM_GROUP, SSM_STATE), SSM_STATE ** -0.5),
        "ssm_c_im": nrm(ks[17], (N_SSM_LAYERS, SSM_GROUPS, SSM_GROUP, SSM_STATE), SSM_STATE ** -0.5),
        "ssm_d": nrm(ks[18], (N_SSM_LAYERS, D_MODEL), 1.0),
        "ssm_w_glu": nrm(ks[19], (N_SSM_LAYERS, D_MODEL, 2 * D_MODEL), D_MODEL ** -0.5),
        "fox_w_qkvf": nrm(ks[20], (N_FOX_LAYERS, D_MODEL, 3 * D_MODEL + FOX_HEADS), D_MODEL ** -0.5),
        "fox_b_f": FOX_BIAS_INIT + nrm(ks[21], (N_FOX_LAYERS, FOX_HEADS), 0.5),
        "fox_w_o": nrm(ks[22], (N_FOX_LAYERS, D_MODEL, D_MODEL), D_MODEL ** -0.5),
        "ffn_w_up": nrm(ks[23], (DEPTH, D_MODEL, D_FF), D_MODEL ** -0.5),
        "ffn_w_gate": nrm(ks[24], (DEPTH, D_MODEL, D_FF), D_MODEL ** -0.5),
        "ffn_conv_w": nrm(ks[25], (DEPTH, CONV_WIDTH, D_FF), CONV_WIDTH ** -0.5),
        "ffn_conv_b": nrm(ks[26], (DEPTH, D_FF), 0.01),
        "ffn_w_down": nrm(ks[27], (DEPTH, D_FF, D_MODEL), D_FF ** -0.5),
    }


def reference(x_prompt, x_sample, state_ssm_re, state_ssm_im, cache_fox_k, cache_fox_v,
              cache_fox_logf, state_ffn_conv, norm_mix, norm_ffn, norm_final,
              ssm_a_re, ssm_a_im, ssm_log_step, ssm_b_re, ssm_b_im, ssm_c_re, ssm_c_im,
              ssm_d, ssm_w_glu, fox_w_qkvf, fox_b_f, fox_w_o,
              ffn_w_up, ffn_w_gate, ffn_conv_w, ffn_conv_b, ffn_w_down):
    xp, xs = x_prompt, x_sample
    n_p = xp.shape[0]
    ssm_re_p, ssm_im_p, ssm_re_s, ssm_im_s = [], [], [], []
    k_p, v_p, lf_p, k_s, v_s, lf_s = [], [], [], [], [], []
    conv_p, conv_s = [], []
    for i in range(DEPTH):
        j = i // N_MIXERS
        up = rms_norm(xp, norm_mix[i])
        us = rms_norm(xs, norm_mix[i])
        if i % N_MIXERS == 0:
            ssm_w = (ssm_a_re[j], ssm_a_im[j], ssm_log_step[j], ssm_b_re[j], ssm_b_im[j],
                     ssm_c_re[j], ssm_c_im[j], ssm_d[j], ssm_w_glu[j])
            zero_state = jnp.zeros((n_p, SSM_GROUPS, SSM_STATE), jnp.float32)
            mp, hr_p, hi_p = s5_mixer(up, zero_state, zero_state, *ssm_w)
            ms, hr_s, hi_s = s5_mixer(us, state_ssm_re[j], state_ssm_im[j], *ssm_w)
            ssm_re_p.append(hr_p)
            ssm_im_p.append(hi_p)
            ssm_re_s.append(hr_s)
            ssm_im_s.append(hi_s)
        else:
            mp, kp, vp, lp = fox_prompt(up, fox_w_qkvf[j], fox_b_f[j], fox_w_o[j])
            ms, kn, vn, ln = fox_sample(us, cache_fox_k[j], cache_fox_v[j], cache_fox_logf[j],
                                        fox_w_qkvf[j], fox_b_f[j], fox_w_o[j])
            k_p.append(kp)
            v_p.append(vp)
            lf_p.append(lp)
            k_s.append(kn)
            v_s.append(vn)
            lf_s.append(ln)
        xp = xp + mp
        xs = xs + ms
        hp = rms_norm(xp, norm_ffn[i])
        hs = rms_norm(xs, norm_ffn[i])
        ffn_w = (ffn_w_up[i], ffn_w_gate[i], ffn_conv_w[i], ffn_conv_b[i], ffn_w_down[i])
        fp, cp = conv_ffn(hp, jnp.zeros((n_p, CONV_WIDTH - 1, D_FF), hp.dtype), *ffn_w)
        fs, cs = conv_ffn(hs, state_ffn_conv[i], *ffn_w)
        conv_p.append(cp)
        conv_s.append(cs)
        xp = xp + fp
        xs = xs + fs
    y_prompt = rms_norm(xp, norm_final)
    y_sample = rms_norm(xs, norm_final)
    new_ssm_re_p = jnp.stack(ssm_re_p)
    new_ssm_im_p = jnp.stack(ssm_im_p)
    new_fox_k_p = jnp.stack(k_p)
    new_fox_v_p = jnp.stack(v_p)
    new_fox_logf_p = jnp.stack(lf_p)
    new_ffn_conv_p = jnp.stack(conv_p)
    new_ssm_re_s = jnp.stack(ssm_re_s)
    new_ssm_im_s = jnp.stack(ssm_im_s)
    new_fox_k_s = jnp.stack(k_s)
    new_fox_v_s = jnp.stack(v_s)
    new_fox_logf_s = jnp.stack(lf_s)
    new_ffn_conv_s = jnp.stack(conv_s)
    return (y_prompt, y_sample,
            new_ssm_re_p, new_ssm_im_p, new_fox_k_p, new_fox_v_p, new_fox_logf_p, new_ffn_conv_p,
            new_ssm_re_s, new_ssm_im_s, new_fox_k_s, new_fox_v_s, new_fox_logf_s, new_ffn_conv_s)
```

```python
import functools
import math

import jax
import jax.numpy as jnp
from jax import lax
from jax.experimental import pallas as pl
from jax.experimental.pallas import tpu as pltpu

D_MODEL = 1024
SSM_GROUP = 16
SSM_GROUPS = D_MODEL // SSM_GROUP
SSM_STATE = 64
SSM_BLOCK = 16
SSM_BLOCK_LOG2 = 4
BLOCK_LANES = SSM_BLOCK * SSM_GROUP
FOX_HEADS = 16
FOX_HEAD_DIM = 64
HEAD_PAIRS = FOX_HEADS // 2
PAIR_LANES = 2 * FOX_HEAD_DIM
D_FF = 2816
CONV_WIDTH = 3
NORM_EPS = 1e-6
NEG_INF = -1e30
LOG2E = 1.4426950408889634
LANES = 128
VMEM_LIMIT = 56 * 1024 * 1024

BF16 = jnp.bfloat16
F32 = jnp.float32


def _cparams(n_axes, vmem=VMEM_LIMIT):
    return pltpu.CompilerParams(dimension_semantics=("arbitrary",) * n_axes, vmem_limit_bytes=vmem)


def _full(shape):
    nd = len(shape)
    return pl.BlockSpec(shape, lambda *_: (0,) * nd)


def _rms(x, g):
    ms = jnp.mean(x * x, axis=-1, keepdims=True)
    return x * lax.rsqrt(ms + NORM_EPS) * g


def _gelu(x):
    c = math.sqrt(2.0 / math.pi)
    return 0.5 * x * (1.0 + jnp.tanh(c * (x + 0.044715 * (x * x * x))))


def _sigmoid(x):
    return 1.0 / (1.0 + jnp.exp(-x))


def _log_sigmoid(x):
    return jnp.minimum(x, 0.0) - jnp.log1p(jnp.exp(-jnp.abs(x)))


def _split3(x):
    hi = x.astype(BF16)
    r1 = x - hi.astype(F32)
    mid = r1.astype(BF16)
    lo = (r1 - mid.astype(F32)).astype(BF16)
    return hi, mid, lo


def _dot(a, b):
    return jnp.dot(a, b, preferred_element_type=F32)


def _dot_nt(a, b):
    return lax.dot_general(a, b, (((1,), (1,)), ((), ())), preferred_element_type=F32)


def _dot_nt_f32(a, b):
    a0, a1, a2 = _split3(a)
    b0, b1, b2 = _split3(b)
    return (_dot_nt(a0, b0) + (_dot_nt(a0, b1) + _dot_nt(a1, b0))
            + (_dot_nt(a0, b2) + _dot_nt(a1, b1) + _dot_nt(a2, b0)))


def _tri_cumsum(x, n):
    r = lax.broadcasted_iota(jnp.int32, (n, n), 0)
    c = lax.broadcasted_iota(jnp.int32, (n, n), 1)
    tri = jnp.where(c <= r, 1.0, 0.0).astype(BF16)
    hi, mid, lo = _split3(x)
    return _dot(tri, hi) + _dot(tri, mid) + _dot(tri, lo)


def _s5_param_kernel(are_r, aim_r, ls_ref, btre, btim, ctre, ctim,
                     k_ref, wre_ref, wim_ref, zre_ref, zim_ref, mu_ref):
    dt = jnp.exp(ls_ref[0])
    lre, lim = are_r[0], aim_r[0]
    mag = jnp.exp(lre * dt)
    abr = mag * jnp.cos(lim * dt)
    abi = mag * jnp.sin(lim * dt)
    den = lre * lre + lim * lim
    nr = abr - 1.0
    zre = (nr * lre + abi * lim) / den
    zim = (abi * lre - nr * lim) / den

    def cpow(e):
        m = jnp.exp(lre * dt * e)
        a = lim * dt * e
        return m * jnp.cos(a), m * jnp.sin(a)

    def per_step(e):
        step = lax.broadcasted_iota(jnp.int32, (SSM_BLOCK, SSM_STATE), 0).astype(F32)
        pr, pi = cpow(e(step))
        rep = lambda t: jnp.broadcast_to(t[:, None, :], (SSM_BLOCK, SSM_GROUP, SSM_STATE)).reshape(
            BLOCK_LANES, SSM_STATE)
        return rep(pr), rep(pi)

    bre, bim = btre[0], btim[0]
    bbre = zre * bre - zim * bim
    bbim = zre * bim + zim * bre
    pwr, pwi = per_step(lambda s: (SSM_BLOCK - 1.0) - s)
    wre_ref[0] = (pwr * bbre - pwi * bbim).astype(BF16)
    wim_ref[0] = (pwr * bbim + pwi * bbre).astype(BF16)
    pbr, pbi = per_step(lambda s: -s)
    bxre = pbr * bbre - pbi * bbim
    bxim = pbr * bbim + pbi * bbre

    cre, cim = ctre[0], ctim[0]
    pcr, pci = per_step(lambda t: t)
    cxre = pcr * cre - pci * cim
    cxim = pcr * cim + pci * cre
    pzr, pzi = per_step(lambda t: t + 1.0)
    zre_ref[0] = (pzr * cre - pzi * cim).astype(BF16)
    zim_ref[0] = (-(pzr * cim + pzi * cre)).astype(BF16)

    kfull = _dot_nt_f32(bxre, cxre) - _dot_nt_f32(bxim, cxim)
    rs = lax.broadcasted_iota(jnp.int32, (BLOCK_LANES, BLOCK_LANES), 0) >> SSM_BLOCK_LOG2
    ct = lax.broadcasted_iota(jnp.int32, (BLOCK_LANES, BLOCK_LANES), 1) >> SSM_BLOCK_LOG2
    k_ref[0] = jnp.where(ct >= rs, kfull, 0.0).astype(BF16)

    mur, mui = cpow(float(SSM_BLOCK))
    mu_ref[0, 0:1, :] = mur
    mu_ref[0, 1:2, :] = mui


def _s5_params(a_re, a_im, log_step, b_re, b_im, c_re, c_im):
    g, p = a_re.shape
    bt = jnp.tile(jnp.swapaxes(b_re, 1, 2), (1, SSM_BLOCK, 1)), jnp.tile(jnp.swapaxes(b_im, 1, 2), (1, SSM_BLOCK, 1))
    ct = jnp.tile(c_re, (1, SSM_BLOCK, 1)), jnp.tile(c_im, (1, SSM_BLOCK, 1))
    row = pl.BlockSpec((1, 1, p), lambda i: (i, 0, 0))
    one = pl.BlockSpec((1, 1, 1), lambda i: (i, 0, 0))
    tall = pl.BlockSpec((1, BLOCK_LANES, p), lambda i: (i, 0, 0))
    return pl.pallas_call(
        _s5_param_kernel,
        grid=(g,),
        in_specs=[row, row, one, tall, tall, tall, tall],
        out_specs=[pl.BlockSpec((1, BLOCK_LANES, BLOCK_LANES), lambda i: (i, 0, 0)),
                   tall, tall, tall, tall, pl.BlockSpec((1, 2, p), lambda i: (i, 0, 0))],
        out_shape=[jax.ShapeDtypeStruct((g, BLOCK_LANES, BLOCK_LANES), BF16)]
        + [jax.ShapeDtypeStruct((g, BLOCK_LANES, p), BF16)] * 4
        + [jax.ShapeDtypeStruct((g, 2, p), F32)],
        compiler_params=_cparams(1),
        name="s5_params",
    )(a_re[:, None, :], a_im[:, None, :], log_step[:, None, None], bt[0], bt[1], ct[0], ct[1])


def _norm_kernel(x_ref, g_ref, u_ref):
    u_ref[...] = _rms(x_ref[...], g_ref[...]).astype(u_ref.dtype)


def _norm(x, g, tm):
    t = x.shape[0]
    tile = pl.BlockSpec((tm, D_MODEL), lambda i: (i, 0))
    return pl.pallas_call(
        _norm_kernel, grid=(t // tm,),
        in_specs=[tile, _full((1, D_MODEL))], out_specs=tile,
        out_shape=jax.ShapeDtypeStruct((t, D_MODEL), BF16),
        compiler_params=_cparams(1), name="s5_prenorm",
    )(x, g[None, :])


def _s5_core_kernel(nb, n_seq, has_h0, *refs):
    if has_h0:
        (u_ref, k_ref, wre_ref, wim_ref, zre_ref, zim_ref, mu_ref, h0re_ref, h0im_ref,
         y_ref, fre_ref, fim_ref, sre_scr, sim_scr) = refs
    else:
        (u_ref, k_ref, wre_ref, wim_ref, zre_ref, zim_ref, mu_ref,
         y_ref, fre_ref, fim_ref, sre_scr, sim_scr) = refs
    rows = nb * n_seq
    u = u_ref[0]
    sre = _dot(u, wre_ref[0])
    sim = _dot(u, wim_ref[0])
    mu = mu_ref[0]
    mre, mim = mu[0:1, :], mu[1:2, :]
    blk = lax.broadcasted_iota(jnp.int32, (rows, SSM_STATE), 0) & (nb - 1)
    if has_h0:
        h0re, h0im = h0re_ref[0], h0im_ref[0]
        sre = sre + (mre * h0re - mim * h0im)
        sim = sim + (mre * h0im + mim * h0re)
    d = 1
    while d < nb:
        keep = blk >= d
        shr = jnp.where(keep, pltpu.roll(sre, d, 0), 0.0)
        shi = jnp.where(keep, pltpu.roll(sim, d, 0), 0.0)
        sre, sim = sre + (mre * shr - mim * shi), sim + (mre * shi + mim * shr)
        mre, mim = mre * mre - mim * mim, 2.0 * mre * mim
        d *= 2
    sre_scr[...] = sre
    sim_scr[...] = sim
    fre_ref[0] = sre_scr[pl.ds(nb - 1, n_seq, stride=nb), :]
    fim_ref[0] = sim_scr[pl.ds(nb - 1, n_seq, stride=nb), :]
    first = blk == 0
    inre = jnp.where(first, 0.0, pltpu.roll(sre, 1, 0))
    inim = jnp.where(first, 0.0, pltpu.roll(sim, 1, 0))
    if has_h0:
        inre = inre + h0re
        inim = inim + h0im
    y = _dot(u, k_ref[0]) + _dot_nt(inre.astype(BF16), zre_ref[0]) + _dot_nt(inim.astype(BF16), zim_ref[0])
    y_ref[0] = y


def _s5_core(ublk, params, n_seq, nb, h0):
    kmat, wre, wim, zre, zim, mu = params
    g, rows, _ = ublk.shape
    p = SSM_STATE
    per_g = lambda shape: pl.BlockSpec((1,) + shape, lambda i: (i, 0, 0))
    in_specs = [per_g((rows, BLOCK_LANES)), per_g((BLOCK_LANES, BLOCK_LANES)),
                per_g((BLOCK_LANES, p)), per_g((BLOCK_LANES, p)),
                per_g((BLOCK_LANES, p)), per_g((BLOCK_LANES, p)), per_g((2, p))]
    args = [ublk, kmat, wre, wim, zre, zim, mu]
    if h0 is not None:
        in_specs += [per_g((rows, p)), per_g((rows, p))]
        args += list(h0)
    return pl.pallas_call(
        functools.partial(_s5_core_kernel, nb, n_seq, h0 is not None),
        grid=(g,),
        in_specs=in_specs,
        out_specs=[per_g((rows, BLOCK_LANES)), per_g((n_seq, p)), per_g((n_seq, p))],
        out_shape=[jax.ShapeDtypeStruct((g, rows, BLOCK_LANES), F32),
                   jax.ShapeDtypeStruct((g, n_seq, p), F32),
                   jax.ShapeDtypeStruct((g, n_seq, p), F32)],
        scratch_shapes=[pltpu.VMEM((rows, p), F32), pltpu.VMEM((rows, p), F32)],
        compiler_params=_cparams(1), name="s5_core",
    )(*args)


def _ffn_kernel(pre, final_norm, tiles_per_seq, tm, *refs):
    refs = list(refs)
    x_ref = refs.pop(0)
    if pre == "glu":
        ys_ref, gmix_ref, dskip_ref, wglu_ref = refs[:4]
        refs = refs[4:]
    else:
        o_ref, wo_ref = refs[:2]
        refs = refs[2:]
    (hist_ref, gffn_ref, wup_ref, wgate_ref, convw_ref, convb_ref, wdown_ref) = refs[:7]
    refs = refs[7:]
    if final_norm:
        gfin_ref = refs.pop(0)
    out_ref, cstate_ref, carry = refs

    x = x_ref[...]
    if pre == "glu":
        u = _rms(x, gmix_ref[...])
        g = _gelu(ys_ref[...] + dskip_ref[...] * u).astype(BF16)
        z = _dot(g, wglu_ref[...])
        x = x + z[:, :D_MODEL] * _sigmoid(z[:, D_MODEL:])
    else:
        x = x + _dot(o_ref[...], wo_ref[...])

    h = _rms(x, gffn_ref[...]).astype(BF16)
    a = _dot(h, wup_ref[...])

    @pl.when(pl.program_id(0) % tiles_per_seq == 0)
    def _():
        carry[...] = hist_ref[0]

    prev = carry[...]
    row = lax.broadcasted_iota(jnp.int32, (tm, D_FF), 0)
    a1 = jnp.where(row == 0, prev[1:2, :], pltpu.roll(a, 1, 0))
    a2 = jnp.where(row == 0, prev[0:1, :], jnp.where(row == 1, prev[1:2, :], pltpu.roll(a, 2, 0)))
    cw = convw_ref[...]
    conv = a2 * cw[0:1, :] + a1 * cw[1:2, :] + a * cw[2:3, :] + convb_ref[...]
    last2 = a[tm - (CONV_WIDTH - 1):, :]
    carry[...] = last2
    cstate_ref[0] = last2
    hh = (_gelu(conv) * _dot(h, wgate_ref[...])).astype(BF16)
    x = x + _dot(hh, wdown_ref[...])
    if final_norm:
        x = _rms(x, gfin_ref[...])
    out_ref[...] = x


def _ffn(x, pre, pre_args, hist, g_ffn, w_up, w_gate, conv_w, conv_b, w_down, g_final, seq_len, tm):
    t = x.shape[0]
    n_seq = t // seq_len
    tiles_per_seq = seq_len // tm
    tile = pl.BlockSpec((tm, D_MODEL), lambda i: (i, 0))
    vec = _full((1, D_MODEL))
    const = lambda a: pl.BlockSpec(a.shape, lambda i: (0,) * a.ndim, pipeline_mode=pl.Buffered(1))
    per_seq = pl.BlockSpec((1, CONV_WIDTH - 1, D_FF), lambda i: (i // tiles_per_seq, 0, 0))
    args, in_specs = [x], [tile]
    if pre == "glu":
        ys, g_mix, d_skip, w_glu = pre_args
        args += [ys, g_mix[None, :], d_skip[None, :], w_glu]
        in_specs += [tile, vec, vec, const(w_glu)]
    else:
        o, w_o = pre_args
        args += [o, w_o]
        in_specs += [tile, const(w_o)]
    args += [hist, g_ffn[None, :], w_up, w_gate, conv_w, conv_b[None, :], w_down]
    in_specs += [per_seq, vec, const(w_up), const(w_gate), _full((CONV_WIDTH, D_FF)), _full((1, D_FF)),
                 const(w_down)]
    if g_final is not None:
        args.append(g_final[None, :])
        in_specs.append(vec)
    return pl.pallas_call(
        functools.partial(_ffn_kernel, pre, g_final is not None, tiles_per_seq, tm),
        grid=(t // tm,),
        in_specs=in_specs,
        out_specs=[tile, per_seq],
        out_shape=[jax.ShapeDtypeStruct((t, D_MODEL), F32),
                   jax.ShapeDtypeStruct((n_seq, CONV_WIDTH - 1, D_FF), F32)],
        scratch_shapes=[pltpu.VMEM((CONV_WIDTH - 1, D_FF), F32)],
        compiler_params=_cparams(1), name="ffn_" + pre,
    )(*args)


def _c_pieces(c):
    hi, mid, lo = _split3(c * LOG2E)
    return (hi.astype(F32) + pltpu.roll(mid.astype(F32), FOX_HEADS, 1)
            + pltpu.roll(lo.astype(F32), 2 * FOX_HEADS, 1)).astype(BF16)


def _qkv_kernel(tiles_per_seq, tm, x_ref, g_ref, wq_ref, wk_ref, wv_ref, wf_ref, bf_ref,
                k_ref, v_ref, lf_ref, qb_ref, kb_ref, vb_ref, c3_ref, carry):
    u = _rms(x_ref[...], g_ref[...]).astype(BF16)
    q = _dot(u, wq_ref[...]) * (FOX_HEAD_DIM ** -0.5 * LOG2E)
    k = _dot(u, wk_ref[...])
    v = _dot(u, wv_ref[...])
    k_ref[...] = k
    v_ref[...] = v
    qb_ref[...] = q.astype(BF16)
    kb_ref[...] = k.astype(BF16)
    vb_ref[...] = v.astype(BF16)
    lane = lax.broadcasted_iota(jnp.int32, (tm, LANES), 1)
    logf = jnp.where(lane < FOX_HEADS, _log_sigmoid(_dot(u, wf_ref[...]) + bf_ref[...]), 0.0)
    lf_ref[...] = logf[:, :FOX_HEADS]

    @pl.when(pl.program_id(0) % tiles_per_seq == 0)
    def _():
        carry[...] = jnp.zeros_like(carry)

    c = _tri_cumsum(logf, tm) + carry[0:1, :]
    carry[...] = jnp.broadcast_to(c[tm - 1:tm, :], carry.shape)
    c3_ref[...] = _c_pieces(c)


def _qkv(x, g, wq, wk, wv, wf, bf, seq_len, tm):
    t = x.shape[0]
    tiles_per_seq = seq_len // tm
    tile = pl.BlockSpec((tm, D_MODEL), lambda i: (i, 0))
    const = lambda a: pl.BlockSpec(a.shape, lambda i: (0,) * a.ndim, pipeline_mode=pl.Buffered(1))
    return pl.pallas_call(
        functools.partial(_qkv_kernel, tiles_per_seq, tm),
        grid=(t // tm,),
        in_specs=[tile, _full((1, D_MODEL)), const(wq), const(wk), const(wv), const(wf), _full((1, LANES))],
        out_specs=[tile, tile, pl.BlockSpec((tm, FOX_HEADS), lambda i: (i, 0)), tile, tile, tile,
                   pl.BlockSpec((tm, LANES), lambda i: (i, 0))],
        out_shape=[jax.ShapeDtypeStruct((t, D_MODEL), F32), jax.ShapeDtypeStruct((t, D_MODEL), F32),
                   jax.ShapeDtypeStruct((t, FOX_HEADS), F32),
                   jax.ShapeDtypeStruct((t, D_MODEL), BF16), jax.ShapeDtypeStruct((t, D_MODEL), BF16),
                   jax.ShapeDtypeStruct((t, D_MODEL), BF16), jax.ShapeDtypeStruct((t, LANES), BF16)],
        scratch_shapes=[pltpu.VMEM((8, LANES), F32)],
        compiler_params=_cparams(1), name="fox_qkv",
    )(x, g[None, :], wq, wk, wv, wf, bf)


_CQ_LANE = 3 * FOX_HEADS


def _query_ext(q, c3q, head, lo_half):
    rows = q.shape[0]
    lane = lax.broadcasted_iota(jnp.int32, (rows, LANES), 1)
    own = (lane < FOX_HEAD_DIM) if lo_half else (lane >= FOX_HEAD_DIM)
    qm = jnp.where(own, q, jnp.zeros_like(q))
    pick = lambda j: jnp.sum(jnp.where(lane == head + j * FOX_HEADS, c3q, 0.0), axis=1, keepdims=True)
    onehot = (lane == head) | (lane == head + FOX_HEADS) | (lane == head + 2 * FOX_HEADS)
    qc = jnp.where(lane == _CQ_LANE, pick(0),
                   jnp.where(lane == _CQ_LANE + 1, pick(1),
                             jnp.where(lane == _CQ_LANE + 2, pick(2), jnp.where(onehot, 1.0, 0.0))))
    return jnp.concatenate([qm, qc.astype(BF16)], axis=1)


def _key_ext(k, c3k):
    lane = lax.broadcasted_iota(jnp.int32, c3k.shape, 1)
    ones = ((lane >= _CQ_LANE) & (lane < _CQ_LANE + 3)).astype(F32)
    kc = (ones - c3k.astype(F32)).astype(BF16)
    return jnp.concatenate([k, kc], axis=1)


def _softmax_step(s, vext, m_ref, acc_ref):
    n_keys = s.shape[1]
    m_prev = m_ref[...]
    m_new = jnp.maximum(m_prev, jnp.max(s, axis=1, keepdims=True))
    m_keys = m_new[:, :n_keys] if n_keys < LANES else jnp.tile(m_new, (1, n_keys // LANES))
    p = jnp.exp2(s - m_keys)
    alpha = jnp.exp2(m_prev - m_new)
    acc_ref[...] = acc_ref[...] * jnp.tile(alpha, (1, 2)) + _dot(p.astype(BF16), vext)
    m_ref[...] = m_new


def _attn_prompt_kernel(tq, q_ref, c3q_ref, k_ref, v_ref, c3k_ref, o_ref, qext, m_scr, acc_scr):
    pair = pl.program_id(1)
    qi = pl.program_id(2)
    q = q_ref[...]
    c3q = c3q_ref[...].astype(F32)
    for a in range(2):
        qext[a] = _query_ext(q, c3q, 2 * pair + a, a == 0)
        m_scr[a] = jnp.full((tq, LANES), NEG_INF, F32)
        acc_scr[a] = jnp.zeros((tq, 2 * LANES), F32)
    ones = jnp.ones((tq, LANES), BF16)

    def tile_step(j, masked):
        start = pl.multiple_of(j * tq, tq)
        kext = _key_ext(k_ref[pl.ds(start, tq), :], c3k_ref[pl.ds(start, tq), :])
        vext = jnp.concatenate([v_ref[pl.ds(start, tq), :], ones], axis=1)
        for a in range(2):
            s = _dot_nt(qext[a], kext)
            if masked:
                r = lax.broadcasted_iota(jnp.int32, (tq, tq), 0)
                c = lax.broadcasted_iota(jnp.int32, (tq, tq), 1)
                s = jnp.where(c <= r, s, NEG_INF)
            _softmax_step(s, vext, m_scr.at[a], acc_scr.at[a])

    def body(j, carry):
        tile_step(j, False)
        return carry

    lax.fori_loop(0, qi, body, 0)
    tile_step(qi, True)
    lane = lax.broadcasted_iota(jnp.int32, (tq, LANES), 1)
    outs = [acc_scr[a][:, :LANES] / acc_scr[a][:, LANES:] for a in range(2)]
    o_ref[...] = jnp.where(lane < FOX_HEAD_DIM, outs[0], outs[1]).astype(o_ref.dtype)


def _attn_prompt(qb, kb, vb, c3, n_seq, seq_len, tq):
    t = qb.shape[0]
    nq = seq_len // tq
    qtile = pl.BlockSpec((tq, PAIR_LANES), lambda b, p, i: (b * nq + i, p))
    c3tile = pl.BlockSpec((tq, LANES), lambda b, p, i: (b * nq + i, 0))
    seq_pair = pl.BlockSpec((seq_len, PAIR_LANES), lambda b, p, i: (b, p))
    seq_c3 = pl.BlockSpec((seq_len, LANES), lambda b, p, i: (b, 0))
    return pl.pallas_call(
        functools.partial(_attn_prompt_kernel, tq),
        grid=(n_seq, HEAD_PAIRS, nq),
        in_specs=[qtile, c3tile, seq_pair, seq_pair, seq_c3],
        out_specs=qtile,
        out_shape=jax.ShapeDtypeStruct((t, D_MODEL), BF16),
        scratch_shapes=[pltpu.VMEM((2, tq, 2 * LANES), BF16), pltpu.VMEM((2, tq, LANES), F32),
                        pltpu.VMEM((2, tq, 2 * LANES), F32)],
        compiler_params=_cparams(3), name="fox_attn_prompt",
    )(qb, c3, kb, vb, c3)


def _attn_sample_kernel(n_cache_tiles, tk, t_new, q_ref, c3q_ref, kc_ref, vc_ref, lfc_ref,
                        kn_ref, vn_ref, lfn_ref, o_ref, qext, m_scr, acc_scr, carry):
    j = pl.program_id(1)
    rows = 2 * t_new

    @pl.when(j == 0)
    def _():
        c3q = c3q_ref[...].astype(F32)
        for p in range(HEAD_PAIRS):
            q = q_ref[:, p * PAIR_LANES:(p + 1) * PAIR_LANES]
            qext[p] = jnp.concatenate([_query_ext(q, c3q, 2 * p, True), _query_ext(q, c3q, 2 * p + 1, False)],
                                      axis=0)
        m_scr[...] = jnp.full(m_scr.shape, NEG_INF, F32)
        acc_scr[...] = jnp.zeros(acc_scr.shape, F32)
        carry[...] = jnp.zeros_like(carry)

    def attend(kb, vb, c3k, masked):
        n_keys = kb.shape[0]
        ones = jnp.ones((n_keys, LANES), BF16)
        for p in range(HEAD_PAIRS):
            sl = slice(p * PAIR_LANES, (p + 1) * PAIR_LANES)
            s = _dot_nt(qext[p], _key_ext(kb[:, sl], c3k))
            if masked:
                r = lax.broadcasted_iota(jnp.int32, (rows, n_keys), 0) & (t_new - 1)
                c = lax.broadcasted_iota(jnp.int32, (rows, n_keys), 1)
                s = jnp.where(c <= r, s, NEG_INF)
            _softmax_step(s, jnp.concatenate([vb[:, sl], ones], axis=1), m_scr.at[p], acc_scr.at[p])

    @pl.when(j < n_cache_tiles)
    def _():
        c = _tri_cumsum(lfc_ref[0], tk) + carry[0:1, :]
        carry[...] = jnp.broadcast_to(c[tk - 1:tk, :], carry.shape)
        attend(kc_ref[0].astype(BF16), vc_ref[0].astype(BF16), _c_pieces(c), False)

    @pl.when(j == n_cache_tiles)
    def _():
        c = _tri_cumsum(lfn_ref[...], t_new) + carry[0:1, :]
        attend(kn_ref[...], vn_ref[...], _c_pieces(c), True)
        lane = lax.broadcasted_iota(jnp.int32, (t_new, LANES), 1)
        for p in range(HEAD_PAIRS):
            acc = acc_scr[p]
            oa = acc[:t_new, :LANES] / acc[:t_new, LANES:]
            ob = acc[t_new:, :LANES] / acc[t_new:, LANES:]
            o_ref[:, p * PAIR_LANES:(p + 1) * PAIR_LANES] = jnp.where(lane < FOX_HEAD_DIM, oa, ob).astype(o_ref.dtype)


def _attn_sample(qb, kb, vb, c3, logf, cache_k, cache_v, cache_logf, tk):
    n_seq, past, _ = cache_k.shape
    t_new = qb.shape[0] // n_seq
    nct = past // tk
    pad = lambda a: jnp.pad(a, ((0, 0),) * (a.ndim - 1) + ((0, LANES - FOX_HEADS),))
    new_tile = pl.BlockSpec((t_new, D_MODEL), lambda b, j: (b, 0))
    new_c3 = pl.BlockSpec((t_new, LANES), lambda b, j: (b, 0))
    cache_tile = pl.BlockSpec((1, tk, D_MODEL), lambda b, j: (b, jnp.minimum(j, nct - 1), 0))
    lf_tile = pl.BlockSpec((1, tk, LANES), lambda b, j: (b, jnp.minimum(j, nct - 1), 0))
    return pl.pallas_call(
        functools.partial(_attn_sample_kernel, nct, tk, t_new),
        grid=(n_seq, nct + 1),
        in_specs=[new_tile, new_c3, cache_tile, cache_tile, lf_tile, new_tile, new_tile, new_c3],
        out_specs=new_tile,
        out_shape=jax.ShapeDtypeStruct((n_seq * t_new, D_MODEL), BF16),
        scratch_shapes=[pltpu.VMEM((HEAD_PAIRS, 2 * t_new, 2 * LANES), BF16),
                        pltpu.VMEM((HEAD_PAIRS, 2 * t_new, LANES), F32),
                        pltpu.VMEM((HEAD_PAIRS, 2 * t_new, 2 * LANES), F32),
                        pltpu.VMEM((8, LANES), F32)],
        compiler_params=_cparams(2), name="fox_attn_sample",
    )(qb, c3, cache_k, cache_v, pad(cache_logf), kb, vb, pad(logf))


def _to_blocks(u, n_seq, seq_len):
    nb = seq_len // SSM_BLOCK
    u5 = u.reshape(n_seq, nb, SSM_BLOCK, SSM_GROUPS, SSM_GROUP)
    return jnp.transpose(u5, (3, 0, 1, 2, 4)).reshape(SSM_GROUPS, n_seq * nb, BLOCK_LANES)


def _from_blocks(y, n_seq, seq_len):
    nb = seq_len // SSM_BLOCK
    y5 = y.reshape(SSM_GROUPS, n_seq, nb, SSM_BLOCK, SSM_GROUP)
    return jnp.transpose(y5, (1, 2, 3, 0, 4)).reshape(n_seq * seq_len, D_MODEL)


def _trunk(x, n_seq, seq_len, ssm_h0, fox_cache, conv_hist, w, tm, tq):
    t = n_seq * seq_len
    nb = seq_len // SSM_BLOCK
    x = x.reshape(t, D_MODEL)

    u0 = _norm(x, w["norm_mix"][0], tm)
    if ssm_h0 is None:
        h0 = None
    else:
        def expand(h):
            hz = jnp.zeros((SSM_GROUPS, n_seq, nb, SSM_STATE), F32)
            return hz.at[:, :, 0, :].set(jnp.swapaxes(h, 0, 1)).reshape(SSM_GROUPS, n_seq * nb, SSM_STATE)
        h0 = (expand(ssm_h0[0]), expand(ssm_h0[1]))
    yblk, fre, fim = _s5_core(_to_blocks(u0, n_seq, seq_len), w["s5"], n_seq, nb, h0)
    ys = _from_blocks(yblk, n_seq, seq_len)
    x, conv0 = _ffn(x, "glu", (ys, w["norm_mix"][0], w["ssm_d"], w["w_glu"]), conv_hist[0],
                    w["norm_ffn"][0], w["w_up"][0], w["w_gate"][0], w["conv_w"][0], w["conv_b"][0],
                    w["w_down"][0], None, seq_len, tm)

    k, v, logf, qb, kb, vb, c3 = _qkv(x, w["norm_mix"][1], w["wq"], w["wk"], w["wv"], w["wf"], w["bf"],
                                      seq_len, tm)
    if fox_cache is None:
        o = _attn_prompt(qb, kb, vb, c3, n_seq, seq_len, tq)
    else:
        o = _attn_sample(qb, kb, vb, c3, logf, *fox_cache, tk=min(512, fox_cache[0].shape[1]))
    y, conv1 = _ffn(x, "oproj", (o, w["w_o"]), conv_hist[1],
                    w["norm_ffn"][1], w["w_up"][1], w["w_gate"][1], w["conv_w"][1], w["conv_b"][1],
                    w["w_down"][1], w["norm_final"], seq_len, tm)

    shape4 = (1, n_seq, seq_len, FOX_HEADS, FOX_HEAD_DIM)
    return (y.reshape(n_seq, seq_len, D_MODEL),
            jnp.swapaxes(fre, 0, 1)[None], jnp.swapaxes(fim, 0, 1)[None],
            k.reshape(shape4), v.reshape(shape4), logf.reshape(1, n_seq, seq_len, FOX_HEADS),
            jnp.stack([conv0, conv1]))


def kernel(x_prompt, x_sample, state_ssm_re, state_ssm_im, cache_fox_k, cache_fox_v, cache_fox_logf, state_ffn_conv, norm_mix, norm_ffn, norm_final, ssm_a_re, ssm_a_im, ssm_log_step, ssm_b_re, ssm_b_im, ssm_c_re, ssm_c_im, ssm_d, ssm_w_glu, fox_w_qkvf, fox_b_f, fox_w_o, ffn_w_up, ffn_w_gate, ffn_conv_w, ffn_conv_b, ffn_w_down):
    n_p, l_p, _ = x_prompt.shape
    n_s, l_s, _ = x_sample.shape
    past = cache_fox_k.shape[2]
    wqkvf = fox_w_qkvf[0]
    w = {
        "norm_mix": norm_mix, "norm_ffn": norm_ffn, "norm_final": norm_final,
        "s5": _s5_params(ssm_a_re[0], ssm_a_im[0], ssm_log_step[0], ssm_b_re[0], ssm_b_im[0],
                         ssm_c_re[0], ssm_c_im[0]),
        "ssm_d": ssm_d[0], "w_glu": ssm_w_glu[0].astype(BF16),
        "wq": wqkvf[:, :D_MODEL].astype(BF16), "wk": wqkvf[:, D_MODEL:2 * D_MODEL].astype(BF16),
        "wv": wqkvf[:, 2 * D_MODEL:3 * D_MODEL].astype(BF16),
        "wf": jnp.pad(wqkvf[:, 3 * D_MODEL:], ((0, 0), (0, LANES - FOX_HEADS))).astype(BF16),
        "bf": jnp.pad(fox_b_f[0], (0, LANES - FOX_HEADS))[None, :],
        "w_o": fox_w_o[0].astype(BF16),
        "w_up": ffn_w_up.astype(BF16), "w_gate": ffn_w_gate.astype(BF16), "w_down": ffn_w_down.astype(BF16),
        "conv_w": ffn_conv_w, "conv_b": ffn_conv_b,
    }
    zero_hist = jnp.zeros((2, n_p, CONV_WIDTH - 1, D_FF), F32)
    outs_p = _trunk(x_prompt, n_p, l_p, None, None, zero_hist, w, tm=256, tq=256)
    cache = (cache_fox_k[0].reshape(n_s, past, D_MODEL), cache_fox_v[0].reshape(n_s, past, D_MODEL),
             cache_fox_logf[0])
    outs_s = _trunk(x_sample, n_s, l_s, (state_ssm_re[0], state_ssm_im[0]), cache, state_ffn_conv, w,
                    tm=l_s, tq=l_s)
    return (outs_p[0], outs_s[0]) + outs_p[1:] + outs_s[1:]
```

```python
import functools
import math

import jax
import jax.numpy as jnp
from jax import lax
from jax.experimental import pallas as pl
from jax.experimental.pallas import tpu as pltpu

D_MODEL = 1024
SSM_GROUP = 16
SSM_GROUPS = D_MODEL // SSM_GROUP
SSM_STATE = 64
SSM_BLOCK = 16
SSM_BLOCK_LOG2 = 4
BLOCK_LANES = SSM_BLOCK * SSM_GROUP
FOX_HEADS = 16
FOX_HEAD_DIM = 64
HEAD_PAIRS = FOX_HEADS // 2
PAIR_LANES = 2 * FOX_HEAD_DIM
D_FF = 2816
CONV_WIDTH = 3
NORM_EPS = 1e-6
NEG_INF = -1e30
LOG2E = 1.4426950408889634
LANES = 128
SUBLANES = 8
_CQ_LANE = 3 * FOX_HEADS
VMEM_LIMIT = 56 * 1024 * 1024

BF16 = jnp.bfloat16
F32 = jnp.float32


def _cparams(n_axes, vmem=VMEM_LIMIT):
    return pltpu.CompilerParams(dimension_semantics=("arbitrary",) * n_axes, vmem_limit_bytes=vmem)


def _full(shape):
    nd = len(shape)
    return pl.BlockSpec(shape, lambda *_: (0,) * nd)


def _rms(x, g):
    ms = jnp.mean(x * x, axis=-1, keepdims=True)
    return x * lax.rsqrt(ms + NORM_EPS) * g


def _gelu(x):
    c = math.sqrt(2.0 / math.pi)
    return 0.5 * x * (1.0 + jnp.tanh(c * (x + 0.044715 * (x * x * x))))


def _sigmoid(x):
    return 1.0 / (1.0 + jnp.exp(-x))


def _log_sigmoid(x):
    return jnp.minimum(x, 0.0) - jnp.log1p(jnp.exp(-jnp.abs(x)))


def _split3(x):
    hi = x.astype(BF16)
    r1 = x - hi.astype(F32)
    mid = r1.astype(BF16)
    lo = (r1 - mid.astype(F32)).astype(BF16)
    return hi, mid, lo


def _dot(a, b):
    return jnp.dot(a, b, preferred_element_type=F32)


def _dot_nt(a, b):
    return lax.dot_general(a, b, (((1,), (1,)), ((), ())), preferred_element_type=F32)


def _dot_nt_f32(a, b):
    a0, a1, a2 = _split3(a)
    b0, b1, b2 = _split3(b)
    return (_dot_nt(a0, b0) + (_dot_nt(a0, b1) + _dot_nt(a1, b0))
            + (_dot_nt(a0, b2) + _dot_nt(a1, b1) + _dot_nt(a2, b0)))


def _tri_cumsum(x, n):
    r = lax.broadcasted_iota(jnp.int32, (n, n), 0)
    c = lax.broadcasted_iota(jnp.int32, (n, n), 1)
    tri = jnp.where(c <= r, 1.0, 0.0).astype(BF16)
    hi, mid, lo = _split3(x)
    return _dot(tri, hi) + _dot(tri, mid) + _dot(tri, lo)


def _s5_param_kernel(are_r, aim_r, ls_ref, btre, btim, ctre, ctim,
                     k_ref, wre_ref, wim_ref, zre_ref, zim_ref, mu_ref):
    dt = jnp.exp(ls_ref[0])
    lre, lim = are_r[0], aim_r[0]
    mag = jnp.exp(lre * dt)
    abr = mag * jnp.cos(lim * dt)
    abi = mag * jnp.sin(lim * dt)
    den = lre * lre + lim * lim
    nr = abr - 1.0
    zre = (nr * lre + abi * lim) / den
    zim = (abi * lre - nr * lim) / den

    def cpow(e):
        m = jnp.exp(lre * dt * e)
        a = lim * dt * e
        return m * jnp.cos(a), m * jnp.sin(a)

    def per_step(e):
        step = lax.broadcasted_iota(jnp.int32, (SSM_BLOCK, SSM_STATE), 0).astype(F32)
        pr, pi = cpow(e(step))
        rep = lambda t: jnp.broadcast_to(t[:, None, :], (SSM_BLOCK, SSM_GROUP, SSM_STATE)).reshape(
            BLOCK_LANES, SSM_STATE)
        return rep(pr), rep(pi)

    bre, bim = btre[0], btim[0]
    bbre = zre * bre - zim * bim
    bbim = zre * bim + zim * bre
    pwr, pwi = per_step(lambda s: (SSM_BLOCK - 1.0) - s)
    wre_ref[0] = (pwr * bbre - pwi * bbim).astype(BF16)
    wim_ref[0] = (pwr * bbim + pwi * bbre).astype(BF16)
    pbr, pbi = per_step(lambda s: -s)
    bxre = pbr * bbre - pbi * bbim
    bxim = pbr * bbim + pbi * bbre

    cre, cim = ctre[0], ctim[0]
    pcr, pci = per_step(lambda t: t)
    cxre = pcr * cre - pci * cim
    cxim = pcr * cim + pci * cre
    pzr, pzi = per_step(lambda t: t + 1.0)
    zre_ref[0] = (pzr * cre - pzi * cim).astype(BF16)
    zim_ref[0] = (-(pzr * cim + pzi * cre)).astype(BF16)

    kfull = _dot_nt_f32(bxre, cxre) - _dot_nt_f32(bxim, cxim)
    rs = lax.broadcasted_iota(jnp.int32, (BLOCK_LANES, BLOCK_LANES), 0) >> SSM_BLOCK_LOG2
    ct = lax.broadcasted_iota(jnp.int32, (BLOCK_LANES, BLOCK_LANES), 1) >> SSM_BLOCK_LOG2
    k_ref[0] = jnp.where(ct >= rs, kfull, 0.0).astype(BF16)

    mur, mui = cpow(float(SSM_BLOCK))
    mu_ref[0, 0:1, :] = mur
    mu_ref[0, 1:2, :] = mui


def _s5_params(a_re, a_im, log_step, b_re, b_im, c_re, c_im):
    g, p = a_re.shape
    bt = jnp.tile(jnp.swapaxes(b_re, 1, 2), (1, SSM_BLOCK, 1)), jnp.tile(jnp.swapaxes(b_im, 1, 2), (1, SSM_BLOCK, 1))
    ct = jnp.tile(c_re, (1, SSM_BLOCK, 1)), jnp.tile(c_im, (1, SSM_BLOCK, 1))
    row = pl.BlockSpec((1, 1, p), lambda i: (i, 0, 0))
    one = pl.BlockSpec((1, 1, 1), lambda i: (i, 0, 0))
    tall = pl.BlockSpec((1, BLOCK_LANES, p), lambda i: (i, 0, 0))
    return pl.pallas_call(
        _s5_param_kernel,
        grid=(g,),
        in_specs=[row, row, one, tall, tall, tall, tall],
        out_specs=[pl.BlockSpec((1, BLOCK_LANES, BLOCK_LANES), lambda i: (i, 0, 0)),
                   tall, tall, tall, tall, pl.BlockSpec((1, 2, p), lambda i: (i, 0, 0))],
        out_shape=[jax.ShapeDtypeStruct((g, BLOCK_LANES, BLOCK_LANES), BF16)]
        + [jax.ShapeDtypeStruct((g, BLOCK_LANES, p), BF16)] * 4
        + [jax.ShapeDtypeStruct((g, 2, p), F32)],
        compiler_params=_cparams(1),
        name="s5_params",
    )(a_re[:, None, :], a_im[:, None, :], log_step[:, None, None], bt[0], bt[1], ct[0], ct[1])


def _norm_kernel(x_ref, g_ref, u_ref):
    u_ref[...] = _rms(x_ref[...], g_ref[...]).astype(u_ref.dtype)


def _norm(x, g, tm):
    t = x.shape[0]
    tile = pl.BlockSpec((tm, D_MODEL), lambda i: (i, 0))
    return pl.pallas_call(
        _norm_kernel, grid=(t // tm,),
        in_specs=[tile, _full((1, D_MODEL))], out_specs=tile,
        out_shape=jax.ShapeDtypeStruct((t, D_MODEL), BF16),
        compiler_params=_cparams(1), name="s5_prenorm",
    )(x, g[None, :])


def _s5_core_kernel(nb, n_seq, has_h0, *refs):
    if has_h0:
        (u_ref, k_ref, wre_ref, wim_ref, zre_ref, zim_ref, mu_ref, h0re_ref, h0im_ref,
         y_ref, fre_ref, fim_ref, sre_scr, sim_scr) = refs
    else:
        (u_ref, k_ref, wre_ref, wim_ref, zre_ref, zim_ref, mu_ref,
         y_ref, fre_ref, fim_ref, sre_scr, sim_scr) = refs
    rows = nb * n_seq
    u = u_ref[0]
    sre = _dot(u, wre_ref[0])
    sim = _dot(u, wim_ref[0])
    mu = mu_ref[0]
    mre, mim = mu[0:1, :], mu[1:2, :]
    blk = lax.broadcasted_iota(jnp.int32, (rows, SSM_STATE), 0) & (nb - 1)
    if has_h0:
        h0re, h0im = h0re_ref[0], h0im_ref[0]
        sre = sre + (mre * h0re - mim * h0im)
        sim = sim + (mre * h0im + mim * h0re)
    d = 1
    while d < nb:
        keep = blk >= d
        shr = jnp.where(keep, pltpu.roll(sre, d, 0), 0.0)
        shi = jnp.where(keep, pltpu.roll(sim, d, 0), 0.0)
        sre, sim = sre + (mre * shr - mim * shi), sim + (mre * shi + mim * shr)
        mre, mim = mre * mre - mim * mim, 2.0 * mre * mim
        d *= 2
    sre_scr[...] = sre
    sim_scr[...] = sim
    fre_ref[0] = sre_scr[pl.ds(nb - 1, n_seq, stride=nb), :]
    fim_ref[0] = sim_scr[pl.ds(nb - 1, n_seq, stride=nb), :]
    first = blk == 0
    inre = jnp.where(first, 0.0, pltpu.roll(sre, 1, 0))
    inim = jnp.where(first, 0.0, pltpu.roll(sim, 1, 0))
    if has_h0:
        inre = inre + h0re
        inim = inim + h0im
    y = _dot(u, k_ref[0]) + _dot_nt(inre.astype(BF16), zre_ref[0]) + _dot_nt(inim.astype(BF16), zim_ref[0])
    y_ref[0] = y


def _s5_core(ublk, params, n_seq, nb, h0):
    kmat, wre, wim, zre, zim, mu = params
    g, rows, _ = ublk.shape
    p = SSM_STATE
    per_g = lambda shape: pl.BlockSpec((1,) + shape, lambda i: (i, 0, 0))
    in_specs = [per_g((rows, BLOCK_LANES)), per_g((BLOCK_LANES, BLOCK_LANES)),
                per_g((BLOCK_LANES, p)), per_g((BLOCK_LANES, p)),
                per_g((BLOCK_LANES, p)), per_g((BLOCK_LANES, p)), per_g((2, p))]
    args = [ublk, kmat, wre, wim, zre, zim, mu]
    if h0 is not None:
        in_specs += [per_g((rows, p)), per_g((rows, p))]
        args += list(h0)
    return pl.pallas_call(
        functools.partial(_s5_core_kernel, nb, n_seq, h0 is not None),
        grid=(g,),
        in_specs=in_specs,
        out_specs=[per_g((rows, BLOCK_LANES)), per_g((n_seq, p)), per_g((n_seq, p))],
        out_shape=[jax.ShapeDtypeStruct((g, rows, BLOCK_LANES), F32),
                   jax.ShapeDtypeStruct((g, n_seq, p), F32),
                   jax.ShapeDtypeStruct((g, n_seq, p), F32)],
        scratch_shapes=[pltpu.VMEM((rows, p), F32), pltpu.VMEM((rows, p), F32)],
        compiler_params=_cparams(1), name="s5_core",
    )(*args)


def _ffn_kernel(pre, final_norm, tiles_per_seq, tm, *refs):
    refs = list(refs)
    x_ref = refs.pop(0)
    if pre == "glu":
        ys_ref, gmix_ref, dskip_ref, wglu_ref = refs[:4]
        refs = refs[4:]
    else:
        o_ref, wo_ref = refs[:2]
        refs = refs[2:]
    (hist_ref, gffn_ref, wup_ref, wgate_ref, convw_ref, convb_ref, wdown_ref) = refs[:7]
    refs = refs[7:]
    if final_norm:
        gfin_ref = refs.pop(0)
    out_ref, cstate_ref, carry = refs

    x = x_ref[...]
    if pre == "glu":
        u = _rms(x, gmix_ref[...])
        g = _gelu(ys_ref[...] + dskip_ref[...] * u).astype(BF16)
        z = _dot(g, wglu_ref[...])
        x = x + z[:, :D_MODEL] * _sigmoid(z[:, D_MODEL:])
    else:
        x = x + _dot(o_ref[...], wo_ref[...])

    h = _rms(x, gffn_ref[...]).astype(BF16)
    a = _dot(h, wup_ref[...])

    @pl.when(pl.program_id(0) % tiles_per_seq == 0)
    def _():
        carry[...] = hist_ref[0]

    prev = carry[...]
    row = lax.broadcasted_iota(jnp.int32, (tm, D_FF), 0)
    a1 = jnp.where(row == 0, prev[1:2, :], pltpu.roll(a, 1, 0))
    a2 = jnp.where(row == 0, prev[0:1, :], jnp.where(row == 1, prev[1:2, :], pltpu.roll(a, 2, 0)))
    cw = convw_ref[...]
    conv = a2 * cw[0:1, :] + a1 * cw[1:2, :] + a * cw[2:3, :] + convb_ref[...]
    last2 = a[tm - (CONV_WIDTH - 1):, :]
    carry[...] = last2
    cstate_ref[0] = last2
    hh = (_gelu(conv) * _dot(h, wgate_ref[...])).astype(BF16)
    x = x + _dot(hh, wdown_ref[...])
    if final_norm:
        x = _rms(x, gfin_ref[...])
    out_ref[...] = x


def _ffn(x, pre, pre_args, hist, g_ffn, w_up, w_gate, conv_w, conv_b, w_down, g_final, seq_len, tm):
    t = x.shape[0]
    n_seq = t // seq_len
    tiles_per_seq = seq_len // tm
    tile = pl.BlockSpec((tm, D_MODEL), lambda i: (i, 0))
    vec = _full((1, D_MODEL))
    const = lambda a: pl.BlockSpec(a.shape, lambda i: (0,) * a.ndim, pipeline_mode=pl.Buffered(1))
    per_seq = pl.BlockSpec((1, CONV_WIDTH - 1, D_FF), lambda i: (i // tiles_per_seq, 0, 0))
    args, in_specs = [x], [tile]
    if pre == "glu":
        ys, g_mix, d_skip, w_glu = pre_args
        args += [ys, g_mix[None, :], d_skip[None, :], w_glu]
        in_specs += [tile, vec, vec, const(w_glu)]
    else:
        o, w_o = pre_args
        args += [o, w_o]
        in_specs += [tile, const(w_o)]
    args += [hist, g_ffn[None, :], w_up, w_gate, conv_w, conv_b[None, :], w_down]
    in_specs += [per_seq, vec, const(w_up), const(w_gate), _full((CONV_WIDTH, D_FF)), _full((1, D_FF)),
                 const(w_down)]
    if g_final is not None:
        args.append(g_final[None, :])
        in_specs.append(vec)
    return pl.pallas_call(
        functools.partial(_ffn_kernel, pre, g_final is not None, tiles_per_seq, tm),
        grid=(t // tm,),
        in_specs=in_specs,
        out_specs=[tile, per_seq],
        out_shape=[jax.ShapeDtypeStruct((t, D_MODEL), F32),
                   jax.ShapeDtypeStruct((n_seq, CONV_WIDTH - 1, D_FF), F32)],
        scratch_shapes=[pltpu.VMEM((CONV_WIDTH - 1, D_FF), F32)],
        compiler_params=_cparams(1), name="ffn_" + pre,
    )(*args)


def _c_pieces(c):
    hi, mid, lo = _split3(c * LOG2E)
    return (hi.astype(F32) + pltpu.roll(mid.astype(F32), FOX_HEADS, 1)
            + pltpu.roll(lo.astype(F32), 2 * FOX_HEADS, 1)).astype(BF16)


def _key_bias_lanes(c3):
    lane = lax.broadcasted_iota(jnp.int32, c3.shape, 1)
    ones = ((lane >= _CQ_LANE) & (lane < _CQ_LANE + 3)).astype(F32)
    return (ones - c3.astype(F32)).astype(BF16)


def _qkv_kernel(tiles_per_seq, tm, transposed_v, x_ref, g_ref, wq_ref, wk_ref, wv_ref, wf_ref, bf_ref,
                k_ref, v_ref, lf_ref, qb_ref, kb_ref, vb_ref, c3_ref, kc_ref, carry):
    u = _rms(x_ref[...], g_ref[...]).astype(BF16)
    q = _dot(u, wq_ref[...]) * (FOX_HEAD_DIM ** -0.5 * LOG2E)
    k = _dot(u, wk_ref[...])
    v = _dot(u, wv_ref[...])
    k_ref[...] = k
    v_ref[...] = v
    qb_ref[...] = q.astype(BF16)
    kb_ref[...] = k.astype(BF16)
    if transposed_v:
        vb_ref[0, 0] = jnp.transpose(v).astype(BF16)
    else:
        vb_ref[...] = v.astype(BF16)
    lane = lax.broadcasted_iota(jnp.int32, (tm, LANES), 1)
    logf = jnp.where(lane < FOX_HEADS, _log_sigmoid(_dot(u, wf_ref[...]) + bf_ref[...]), 0.0)
    lf_ref[...] = logf[:, :FOX_HEADS]

    @pl.when(pl.program_id(0) % tiles_per_seq == 0)
    def _():
        carry[...] = jnp.zeros_like(carry)

    c = _tri_cumsum(logf, tm) + carry[0:1, :]
    carry[...] = jnp.broadcast_to(c[tm - 1:tm, :], carry.shape)
    c3 = _c_pieces(c)
    c3_ref[...] = c3
    kc_ref[...] = _key_bias_lanes(c3)


def _qkv(x, g, wq, wk, wv, wf, bf, seq_len, tm, transposed_v):
    t = x.shape[0]
    tiles_per_seq = seq_len // tm
    tile = pl.BlockSpec((tm, D_MODEL), lambda i: (i, 0))
    narrow = pl.BlockSpec((tm, LANES), lambda i: (i, 0))
    const = lambda a: pl.BlockSpec(a.shape, lambda i: (0,) * a.ndim, pipeline_mode=pl.Buffered(1))
    if transposed_v:
        vb_spec = pl.BlockSpec((1, 1, D_MODEL, tm), lambda i: (i // tiles_per_seq, i % tiles_per_seq, 0, 0))
        vb_shape = jax.ShapeDtypeStruct((t // seq_len, tiles_per_seq, D_MODEL, tm), BF16)
    else:
        vb_spec, vb_shape = tile, jax.ShapeDtypeStruct((t, D_MODEL), BF16)
    return pl.pallas_call(
        functools.partial(_qkv_kernel, tiles_per_seq, tm, transposed_v),
        grid=(t // tm,),
        in_specs=[tile, _full((1, D_MODEL)), const(wq), const(wk), const(wv), const(wf), _full((1, LANES))],
        out_specs=[tile, tile, pl.BlockSpec((tm, FOX_HEADS), lambda i: (i, 0)), tile, tile, vb_spec,
                   narrow, narrow],
        out_shape=[jax.ShapeDtypeStruct((t, D_MODEL), F32), jax.ShapeDtypeStruct((t, D_MODEL), F32),
                   jax.ShapeDtypeStruct((t, FOX_HEADS), F32),
                   jax.ShapeDtypeStruct((t, D_MODEL), BF16), jax.ShapeDtypeStruct((t, D_MODEL), BF16),
                   vb_shape, jax.ShapeDtypeStruct((t, LANES), BF16), jax.ShapeDtypeStruct((t, LANES), BF16)],
        scratch_shapes=[pltpu.VMEM((8, LANES), F32)],
        compiler_params=_cparams(1), name="fox_qkv",
    )(x, g[None, :], wq, wk, wv, wf, bf)


def _query_ext(q, c3q, head, lo_half):
    rows = q.shape[0]
    lane = lax.broadcasted_iota(jnp.int32, (rows, LANES), 1)
    own = (lane < FOX_HEAD_DIM) if lo_half else (lane >= FOX_HEAD_DIM)
    qm = jnp.where(own, q, jnp.zeros_like(q))
    pick = lambda j: jnp.sum(jnp.where(lane == head + j * FOX_HEADS, c3q, 0.0), axis=1, keepdims=True)
    onehot = (lane == head) | (lane == head + FOX_HEADS) | (lane == head + 2 * FOX_HEADS)
    qc = jnp.where(lane == _CQ_LANE, pick(0),
                   jnp.where(lane == _CQ_LANE + 1, pick(1),
                             jnp.where(lane == _CQ_LANE + 2, pick(2), jnp.where(onehot, 1.0, 0.0))))
    return jnp.concatenate([qm, qc.astype(BF16)], axis=1)


def _key_ext(k, c3k):
    return jnp.concatenate([k, _key_bias_lanes(c3k)], axis=1)


def _softmax_step(s, vext, m_ref, acc_ref):
    n_keys = s.shape[1]
    m_prev = m_ref[...]
    m_new = jnp.maximum(m_prev, jnp.max(s, axis=1, keepdims=True))
    m_keys = m_new[:, :n_keys] if n_keys < LANES else jnp.tile(m_new, (1, n_keys // LANES))
    p = jnp.exp2(s - m_keys)
    alpha = jnp.exp2(m_prev - m_new)
    acc_ref[...] = acc_ref[...] * jnp.tile(alpha, (1, 2)) + _dot(p.astype(BF16), vext)
    m_ref[...] = m_new


def _sublane_allreduce(x, op):
    for sh in (4, 2, 1):
        x = op(x, pltpu.roll(x, sh, 0))
    return x


def _fold_rows(x, op):
    parts = [x[i * SUBLANES:(i + 1) * SUBLANES] for i in range(x.shape[0] // SUBLANES)]
    while len(parts) > 1:
        parts = [op(parts[i], parts[i + 1]) for i in range(0, len(parts), 2)]
    return parts[0]


ONES_ROWS = 16


def _attn_prompt_kernel(t, n_items, qi_tab, j_tab, q_ref, c3_ref, k_ref, kc_ref, vt_ref, o_ref,
                        qt, s_buf, cm_buf, p_buf, al_buf, acc):
    pair = pl.program_id(1)
    big = -NEG_INF
    for a in range(2):
        acc[a] = jnp.zeros(acc.shape[1:], F32)
        p_buf[1, a] = jnp.zeros((t, t), BF16)
        al_buf[1, a] = jnp.ones((SUBLANES, t), F32)
    ones_rows = jnp.ones((ONES_ROWS, t), BF16)

    def load_q(qi):
        start = pl.multiple_of(qi * t, t)
        q = q_ref[pl.ds(start, t), :]
        c3q = c3_ref[pl.ds(start, t), :].astype(F32)
        for a in range(2):
            qe = _query_ext(q, c3q, 2 * pair + a, a == 0)
            qt[a] = jnp.transpose(qe.astype(F32)).astype(BF16)

    def stage_a(w, slot, diagonal):
        start = pl.multiple_of(j_tab[w] * t, t)
        kext = jnp.concatenate([k_ref[pl.ds(start, t), :], kc_ref[pl.ds(start, t), :]], axis=1)
        for a in range(2):
            s = _dot(kext, qt[a])
            if diagonal:
                r = lax.broadcasted_iota(jnp.int32, (t, t), 0)
                c = lax.broadcasted_iota(jnp.int32, (t, t), 1)
                s = jnp.where(r <= c, s, NEG_INF)
            s_buf[slot, a] = s
            cm_buf[slot, a] = _sublane_allreduce(_fold_rows(s, jnp.maximum), jnp.maximum)

    def stage_b(w, ms, slot):
        m_floor = jnp.where(j_tab[w] == 0, NEG_INF, big)
        new_ms = []
        for a in range(2):
            m_old = jnp.minimum(ms[a], m_floor)
            m_new = jnp.maximum(m_old, cm_buf[slot, a])
            p_buf[slot, a] = jnp.exp2(s_buf[slot, a] - jnp.tile(m_new, (t // SUBLANES, 1))).astype(BF16)
            al_buf[slot, a] = jnp.exp2(m_old - m_new)
            new_ms.append(m_new)
        return tuple(new_ms)

    def stage_c(w, slot):
        vt = jnp.concatenate([vt_ref[0, j_tab[w]], ones_rows], axis=0)
        for a in range(2):
            acc[a] = (acc[a] * jnp.tile(al_buf[slot, a], (acc.shape[1] // SUBLANES, 1))
                      + _dot(vt, p_buf[slot, a]))

    def finalize(w):
        outs = []
        for a in range(2):
            full = acc[a]
            denom = jnp.tile(full[PAIR_LANES:PAIR_LANES + SUBLANES], (PAIR_LANES // SUBLANES, 1))
            outs.append(full[:PAIR_LANES] / denom)
        row = lax.broadcasted_iota(jnp.int32, (PAIR_LANES, t), 0)
        o = jnp.transpose(jnp.where(row < FOX_HEAD_DIM, outs[0], outs[1]))
        o_ref[pl.ds(pl.multiple_of(qi_tab[w] * t, t), t), :] = o.astype(o_ref.dtype)

    load_q(0)
    stage_a(0, 0, True)

    def item(w, ms, slot):
        nxt = jnp.minimum(w + 1, n_items - 1)
        prev = jnp.maximum(w - 1, 0)

        @pl.when(j_tab[nxt] == 0)
        def _():
            load_q(qi_tab[nxt])

        def stages(diagonal, ms):
            stage_c(prev, 1 - slot)
            ms = stage_b(w, ms, slot)
            stage_a(nxt, 1 - slot, diagonal)
            return ms

        ms = lax.cond(j_tab[nxt] == qi_tab[nxt], functools.partial(stages, True),
                      functools.partial(stages, False), ms)

        @pl.when((w >= 1) & (j_tab[prev] == qi_tab[prev]))
        def _():
            finalize(prev)

        return ms

    def body(i, ms):
        return item(2 * i + 1, item(2 * i, ms, 0), 1)

    ms = lax.fori_loop(0, n_items // 2, body, tuple(jnp.full((SUBLANES, t), NEG_INF, F32) for _ in range(2)))
    if n_items % 2:
        item(n_items - 1, ms, 0)
    stage_c(n_items - 1, (n_items - 1) % 2)
    finalize(n_items - 1)


def _attn_prompt(qb, kb, kc, vt, c3, n_seq, seq_len, t):
    total = qb.shape[0]
    nt = seq_len // t
    items = [(qi, j) for qi in range(nt) for j in range(qi + 1)]
    qi_tab = jnp.asarray([qi for qi, _ in items], jnp.int32)
    j_tab = jnp.asarray([j for _, j in items], jnp.int32)
    seq_pair = pl.BlockSpec((seq_len, PAIR_LANES), lambda b, p, *_: (b, p))
    seq_bias = pl.BlockSpec((seq_len, LANES), lambda b, p, *_: (b, 0))
    vt_pair = pl.BlockSpec((1, nt, PAIR_LANES, t), lambda b, p, *_: (b, 0, p, 0))
    return pl.pallas_call(
        functools.partial(_attn_prompt_kernel, t, len(items)),
        grid_spec=pltpu.PrefetchScalarGridSpec(
            num_scalar_prefetch=2,
            grid=(n_seq, HEAD_PAIRS),
            in_specs=[seq_pair, seq_bias, seq_pair, seq_bias, vt_pair],
            out_specs=seq_pair,
            scratch_shapes=[pltpu.VMEM((2, 2 * LANES, t), BF16), pltpu.VMEM((2, 2, t, t), F32),
                            pltpu.VMEM((2, 2, SUBLANES, t), F32),
                            pltpu.VMEM((2, 2, t, t), BF16), pltpu.VMEM((2, 2, SUBLANES, t), F32),
                            pltpu.VMEM((2, PAIR_LANES + ONES_ROWS, t), F32)]),
        out_shape=jax.ShapeDtypeStruct((total, D_MODEL), BF16),
        compiler_params=_cparams(2), name="fox_attn_prompt",
    )(qi_tab, j_tab, qb, c3, kb, kc, vt)


def _attn_sample_kernel(n_cache_tiles, tk, t_new, q_ref, c3q_ref, kc_ref, vc_ref, lfc_ref,
                        kn_ref, vn_ref, lfn_ref, o_ref, qext, m_scr, acc_scr, carry):
    j = pl.program_id(1)
    rows = 2 * t_new

    @pl.when(j == 0)
    def _():
        c3q = c3q_ref[...].astype(F32)
        for p in range(HEAD_PAIRS):
            q = q_ref[:, p * PAIR_LANES:(p + 1) * PAIR_LANES]
            qext[p] = jnp.concatenate([_query_ext(q, c3q, 2 * p, True), _query_ext(q, c3q, 2 * p + 1, False)],
                                      axis=0)
        m_scr[...] = jnp.full(m_scr.shape, NEG_INF, F32)
        acc_scr[...] = jnp.zeros(acc_scr.shape, F32)
        carry[...] = jnp.zeros_like(carry)

    def attend(kb, vb, c3k, masked):
        n_keys = kb.shape[0]
        ones = jnp.ones((n_keys, LANES), BF16)
        for p in range(HEAD_PAIRS):
            sl = slice(p * PAIR_LANES, (p + 1) * PAIR_LANES)
            s = _dot_nt(qext[p], _key_ext(kb[:, sl], c3k))
            if masked:
                r = lax.broadcasted_iota(jnp.int32, (rows, n_keys), 0) & (t_new - 1)
                c = lax.broadcasted_iota(jnp.int32, (rows, n_keys), 1)
                s = jnp.where(c <= r, s, NEG_INF)
            _softmax_step(s, jnp.concatenate([vb[:, sl], ones], axis=1), m_scr.at[p], acc_scr.at[p])

    @pl.when(j < n_cache_tiles)
    def _():
        c = _tri_cumsum(lfc_ref[0], tk) + carry[0:1, :]
        carry[...] = jnp.broadcast_to(c[tk - 1:tk, :], carry.shape)
        attend(kc_ref[0].astype(BF16), vc_ref[0].astype(BF16), _c_pieces(c), False)

    @pl.when(j == n_cache_tiles)
    def _():
        c = _tri_cumsum(lfn_ref[...], t_new) + carry[0:1, :]
        attend(kn_ref[...], vn_ref[...], _c_pieces(c), True)
        lane = lax.broadcasted_iota(jnp.int32, (t_new, LANES), 1)
        for p in range(HEAD_PAIRS):
            acc = acc_scr[p]
            oa = acc[:t_new, :LANES] / acc[:t_new, LANES:]
            ob = acc[t_new:, :LANES] / acc[t_new:, LANES:]
            o_ref[:, p * PAIR_LANES:(p + 1) * PAIR_LANES] = jnp.where(lane < FOX_HEAD_DIM, oa, ob).astype(o_ref.dtype)


def _attn_sample(qb, kb, vb, c3, logf, cache_k, cache_v, cache_logf, tk):
    n_seq, past, _ = cache_k.shape
    t_new = qb.shape[0] // n_seq
    nct = past // tk
    pad = lambda a: jnp.pad(a, ((0, 0),) * (a.ndim - 1) + ((0, LANES - FOX_HEADS),))
    new_tile = pl.BlockSpec((t_new, D_MODEL), lambda b, j: (b, 0))
    new_c3 = pl.BlockSpec((t_new, LANES), lambda b, j: (b, 0))
    cache_tile = pl.BlockSpec((1, tk, D_MODEL), lambda b, j: (b, jnp.minimum(j, nct - 1), 0))
    lf_tile = pl.BlockSpec((1, tk, LANES), lambda b, j: (b, jnp.minimum(j, nct - 1), 0))
    return pl.pallas_call(
        functools.partial(_attn_sample_kernel, nct, tk, t_new),
        grid=(n_seq, nct + 1),
        in_specs=[new_tile, new_c3, cache_tile, cache_tile, lf_tile, new_tile, new_tile, new_c3],
        out_specs=new_tile,
        out_shape=jax.ShapeDtypeStruct((n_seq * t_new, D_MODEL), BF16),
        scratch_shapes=[pltpu.VMEM((HEAD_PAIRS, 2 * t_new, 2 * LANES), BF16),
                        pltpu.VMEM((HEAD_PAIRS, 2 * t_new, LANES), F32),
                        pltpu.VMEM((HEAD_PAIRS, 2 * t_new, 2 * LANES), F32),
                        pltpu.VMEM((8, LANES), F32)],
        compiler_params=_cparams(2), name="fox_attn_sample",
    )(qb, c3, cache_k, cache_v, pad(cache_logf), kb, vb, pad(logf))


def _to_blocks(u, n_seq, seq_len):
    nb = seq_len // SSM_BLOCK
    u5 = u.reshape(n_seq, nb, SSM_BLOCK, SSM_GROUPS, SSM_GROUP)
    return jnp.transpose(u5, (3, 0, 1, 2, 4)).reshape(SSM_GROUPS, n_seq * nb, BLOCK_LANES)


def _from_blocks(y, n_seq, seq_len):
    nb = seq_len // SSM_BLOCK
    y5 = y.reshape(SSM_GROUPS, n_seq, nb, SSM_BLOCK, SSM_GROUP)
    return jnp.transpose(y5, (1, 2, 3, 0, 4)).reshape(n_seq * seq_len, D_MODEL)


def _trunk(x, n_seq, seq_len, ssm_h0, fox_cache, conv_hist, w, tm, tq):
    t = n_seq * seq_len
    nb = seq_len // SSM_BLOCK
    x = x.reshape(t, D_MODEL)

    u0 = _norm(x, w["norm_mix"][0], tm)
    if ssm_h0 is None:
        h0 = None
    else:
        def expand(h):
            hz = jnp.zeros((SSM_GROUPS, n_seq, nb, SSM_STATE), F32)
            return hz.at[:, :, 0, :].set(jnp.swapaxes(h, 0, 1)).reshape(SSM_GROUPS, n_seq * nb, SSM_STATE)
        h0 = (expand(ssm_h0[0]), expand(ssm_h0[1]))
    yblk, fre, fim = _s5_core(_to_blocks(u0, n_seq, seq_len), w["s5"], n_seq, nb, h0)
    ys = _from_blocks(yblk, n_seq, seq_len)
    x, conv0 = _ffn(x, "glu", (ys, w["norm_mix"][0], w["ssm_d"], w["w_glu"]), conv_hist[0],
                    w["norm_ffn"][0], w["w_up"][0], w["w_gate"][0], w["conv_w"][0], w["conv_b"][0],
                    w["w_down"][0], None, seq_len, tm)

    k, v, logf, qb, kb, vb, c3, kc = _qkv(x, w["norm_mix"][1], w["wq"], w["wk"], w["wv"], w["wf"], w["bf"],
                                          seq_len, tq, transposed_v=fox_cache is None)
    if fox_cache is None:
        o = _attn_prompt(qb, kb, kc, vb, c3, n_seq, seq_len, tq)
    else:
        o = _attn_sample(qb, kb, vb, c3, logf, *fox_cache, tk=min(512, fox_cache[0].shape[1]))
    y, conv1 = _ffn(x, "oproj", (o, w["w_o"]), conv_hist[1],
                    w["norm_ffn"][1], w["w_up"][1], w["w_gate"][1], w["conv_w"][1], w["conv_b"][1],
                    w["w_down"][1], w["norm_final"], seq_len, tm)

    shape4 = (1, n_seq, seq_len, FOX_HEADS, FOX_HEAD_DIM)
    return (y.reshape(n_seq, seq_len, D_MODEL),
            jnp.swapaxes(fre, 0, 1)[None], jnp.swapaxes(fim, 0, 1)[None],
            k.reshape(shape4), v.reshape(shape4), logf.reshape(1, n_seq, seq_len, FOX_HEADS),
            jnp.stack([conv0, conv1]))


def kernel(x_prompt, x_sample, state_ssm_re, state_ssm_im, cache_fox_k, cache_fox_v, cache_fox_logf, state_ffn_conv, norm_mix, norm_ffn, norm_final, ssm_a_re, ssm_a_im, ssm_log_step, ssm_b_re, ssm_b_im, ssm_c_re, ssm_c_im, ssm_d, ssm_w_glu, fox_w_qkvf, fox_b_f, fox_w_o, ffn_w_up, ffn_w_gate, ffn_conv_w, ffn_conv_b, ffn_w_down):
    n_p, l_p, _ = x_prompt.shape
    n_s, l_s, _ = x_sample.shape
    past = cache_fox_k.shape[2]
    wqkvf = fox_w_qkvf[0]
    w = {
        "norm_mix": norm_mix, "norm_ffn": norm_ffn, "norm_final": norm_final,
        "s5": _s5_params(ssm_a_re[0], ssm_a_im[0], ssm_log_step[0], ssm_b_re[0], ssm_b_im[0],
                         ssm_c_re[0], ssm_c_im[0]),
        "ssm_d": ssm_d[0], "w_glu": ssm_w_glu[0].astype(BF16),
        "wq": wqkvf[:, :D_MODEL].astype(BF16), "wk": wqkvf[:, D_MODEL:2 * D_MODEL].astype(BF16),
        "wv": wqkvf[:, 2 * D_MODEL:3 * D_MODEL].astype(BF16),
        "wf": jnp.pad(wqkvf[:, 3 * D_MODEL:], ((0, 0), (0, LANES - FOX_HEADS))).astype(BF16),
        "bf": jnp.pad(fox_b_f[0], (0, LANES - FOX_HEADS))[None, :],
        "w_o": fox_w_o[0].astype(BF16),
        "w_up": ffn_w_up.astype(BF16), "w_gate": ffn_w_gate.astype(BF16), "w_down": ffn_w_down.astype(BF16),
        "conv_w": ffn_conv_w, "conv_b": ffn_conv_b,
    }
    zero_hist = jnp.zeros((2, n_p, CONV_WIDTH - 1, D_FF), F32)
    outs_p = _trunk(x_prompt, n_p, l_p, None, None, zero_hist, w, tm=256, tq=512)
    cache = (cache_fox_k[0].reshape(n_s, past, D_MODEL), cache_fox_v[0].reshape(n_s, past, D_MODEL),
             cache_fox_logf[0])
    outs_s = _trunk(x_sample, n_s, l_s, (state_ssm_re[0], state_ssm_im[0]), cache, state_ffn_conv, w,
                    tm=l_s, tq=l_s)
    return (outs_p[0], outs_s[0]) + outs_p[1:] + outs_s[1:]
```

```python
import functools
import math

import jax
import jax.numpy as jnp
from jax import lax
from jax.experimental import pallas as pl
from jax.experimental.pallas import tpu as pltpu

D_MODEL = 1024
SSM_GROUP = 16
SSM_GROUPS = D_MODEL // SSM_GROUP
SSM_STATE = 64
SSM_BLOCK = 16
SSM_BLOCK_LOG2 = 4
BLOCK_LANES = SSM_BLOCK * SSM_GROUP
COLUMN_LANES = 128
COLUMN_GROUPS = COLUMN_LANES // SSM_GROUP
STEP_PAIRS = SSM_BLOCK // 2
FOX_HEADS = 16
FOX_HEAD_DIM = 64
HEAD_PAIRS = FOX_HEADS // 2
PAIR_LANES = 2 * FOX_HEAD_DIM
D_FF = 2816
CONV_WIDTH = 3
FFN_CHUNKS = 2
NORM_EPS = 1e-6
NEG_INF = -1e30
LOG2E = 1.4426950408889634
LANES = 128
SUBLANES = 8
_CQ_LANE = 3 * FOX_HEADS
VMEM_LIMIT = 56 * 1024 * 1024

BF16 = jnp.bfloat16
F32 = jnp.float32


def _cparams(n_axes, vmem=VMEM_LIMIT):
    return pltpu.CompilerParams(dimension_semantics=("arbitrary",) * n_axes, vmem_limit_bytes=vmem)


def _full(shape):
    nd = len(shape)
    return pl.BlockSpec(shape, lambda *_: (0,) * nd)


def _rms(x, g):
    ms = jnp.mean(x * x, axis=-1, keepdims=True)
    return x * lax.rsqrt(ms + NORM_EPS) * g


def _gelu(x):
    c = math.sqrt(2.0 / math.pi)
    return 0.5 * x * (1.0 + jnp.tanh(c * (x + 0.044715 * (x * x * x))))


def _sigmoid(x):
    return 1.0 / (1.0 + jnp.exp(-x))


def _log_sigmoid(x):
    return jnp.minimum(x, 0.0) - jnp.log1p(jnp.exp(-jnp.abs(x)))


def _split3(x):
    hi = x.astype(BF16)
    r1 = x - hi.astype(F32)
    mid = r1.astype(BF16)
    lo = (r1 - mid.astype(F32)).astype(BF16)
    return hi, mid, lo


def _dot(a, b):
    return jnp.dot(a, b, preferred_element_type=F32)


def _dot_nt(a, b):
    return lax.dot_general(a, b, (((1,), (1,)), ((), ())), preferred_element_type=F32)


def _dot_nt_f32(a, b):
    a0, a1, a2 = _split3(a)
    b0, b1, b2 = _split3(b)
    return (_dot_nt(a0, b0) + (_dot_nt(a0, b1) + _dot_nt(a1, b0))
            + (_dot_nt(a0, b2) + _dot_nt(a1, b1) + _dot_nt(a2, b0)))


def _tri_cumsum(x, n):
    r = lax.broadcasted_iota(jnp.int32, (n, n), 0)
    c = lax.broadcasted_iota(jnp.int32, (n, n), 1)
    tri = jnp.where(c <= r, 1.0, 0.0).astype(BF16)
    hi, mid, lo = _split3(x)
    return _dot(tri, hi) + _dot(tri, mid) + _dot(tri, lo)


def _s5_param_kernel(are_r, aim_r, ls_ref, btre, btim, ctre, ctim,
                     k_ref, wre_ref, wim_ref, zre_ref, zim_ref, mu_ref):
    dt = jnp.exp(ls_ref[0])
    lre, lim = are_r[0], aim_r[0]
    mag = jnp.exp(lre * dt)
    abr = mag * jnp.cos(lim * dt)
    abi = mag * jnp.sin(lim * dt)
    den = lre * lre + lim * lim
    nr = abr - 1.0
    zre = (nr * lre + abi * lim) / den
    zim = (abi * lre - nr * lim) / den

    def cpow(e):
        m = jnp.exp(lre * dt * e)
        a = lim * dt * e
        return m * jnp.cos(a), m * jnp.sin(a)

    def per_step(e):
        step = lax.broadcasted_iota(jnp.int32, (SSM_BLOCK, SSM_STATE), 0).astype(F32)
        pr, pi = cpow(e(step))
        rep = lambda t: jnp.broadcast_to(t[:, None, :], (SSM_BLOCK, SSM_GROUP, SSM_STATE)).reshape(
            BLOCK_LANES, SSM_STATE)
        return rep(pr), rep(pi)

    bre, bim = btre[0], btim[0]
    bbre = zre * bre - zim * bim
    bbim = zre * bim + zim * bre
    pwr, pwi = per_step(lambda s: (SSM_BLOCK - 1.0) - s)
    wre_ref[0] = (pwr * bbre - pwi * bbim).astype(BF16)
    wim_ref[0] = (pwr * bbim + pwi * bbre).astype(BF16)
    pbr, pbi = per_step(lambda s: -s)
    bxre = pbr * bbre - pbi * bbim
    bxim = pbr * bbim + pbi * bbre

    cre, cim = ctre[0], ctim[0]
    pcr, pci = per_step(lambda t: t)
    cxre = pcr * cre - pci * cim
    cxim = pcr * cim + pci * cre
    pzr, pzi = per_step(lambda t: t + 1.0)
    zre_ref[0] = (pzr * cre - pzi * cim).astype(BF16)
    zim_ref[0] = (-(pzr * cim + pzi * cre)).astype(BF16)

    kfull = _dot_nt_f32(bxre, cxre) - _dot_nt_f32(bxim, cxim)
    rs = lax.broadcasted_iota(jnp.int32, (BLOCK_LANES, BLOCK_LANES), 0) >> SSM_BLOCK_LOG2
    ct = lax.broadcasted_iota(jnp.int32, (BLOCK_LANES, BLOCK_LANES), 1) >> SSM_BLOCK_LOG2
    k_ref[0] = jnp.where(ct >= rs, kfull, 0.0).astype(BF16)

    mur, mui = cpow(float(SSM_BLOCK))
    mu_ref[0, 0:1, :] = mur
    mu_ref[0, 1:2, :] = mui


def _s5_params(a_re, a_im, log_step, b_re, b_im, c_re, c_im):
    g, p = a_re.shape
    bt = jnp.tile(jnp.swapaxes(b_re, 1, 2), (1, SSM_BLOCK, 1)), jnp.tile(jnp.swapaxes(b_im, 1, 2), (1, SSM_BLOCK, 1))
    ct = jnp.tile(c_re, (1, SSM_BLOCK, 1)), jnp.tile(c_im, (1, SSM_BLOCK, 1))
    row = pl.BlockSpec((1, 1, p), lambda i: (i, 0, 0))
    one = pl.BlockSpec((1, 1, 1), lambda i: (i, 0, 0))
    tall = pl.BlockSpec((1, BLOCK_LANES, p), lambda i: (i, 0, 0))
    return pl.pallas_call(
        _s5_param_kernel,
        grid=(g,),
        in_specs=[row, row, one, tall, tall, tall, tall],
        out_specs=[pl.BlockSpec((1, BLOCK_LANES, BLOCK_LANES), lambda i: (i, 0, 0)),
                   tall, tall, tall, tall, pl.BlockSpec((1, 2, p), lambda i: (i, 0, 0))],
        out_shape=[jax.ShapeDtypeStruct((g, BLOCK_LANES, BLOCK_LANES), BF16)]
        + [jax.ShapeDtypeStruct((g, BLOCK_LANES, p), BF16)] * 4
        + [jax.ShapeDtypeStruct((g, 2, p), F32)],
        compiler_params=_cparams(1),
        name="s5_params",
    )(a_re[:, None, :], a_im[:, None, :], log_step[:, None, None], bt[0], bt[1], ct[0], ct[1])


def _norm_kernel(x_ref, g_ref, u_ref):
    u_ref[...] = _rms(x_ref[...], g_ref[...]).astype(u_ref.dtype)


def _norm(x, g, tm):
    t = x.shape[0]
    tile = pl.BlockSpec((tm, D_MODEL), lambda i: (i, 0))
    return pl.pallas_call(
        _norm_kernel, grid=(t // tm,),
        in_specs=[tile, _full((1, D_MODEL))], out_specs=tile,
        out_shape=jax.ShapeDtypeStruct((t, D_MODEL), F32),
        compiler_params=_cparams(1), name="s5_prenorm",
    )(x, g[None, :])


def _s5_core_kernel(nb, n_seq, has_h0, *refs):
    if has_h0:
        u_ref, kk_ref, ww_ref, zz_ref, mu_ref, h0_ref, y_ref, f_ref, s_scr = refs
    else:
        u_ref, kk_ref, ww_ref, zz_ref, mu_ref, y_ref, f_ref, s_scr = refs
    rows = nb * n_seq
    half = COLUMN_GROUPS * SSM_STATE
    xp = [jnp.concatenate([u_ref[pl.ds(2 * p, rows, stride=SSM_BLOCK), :],
                           u_ref[pl.ds(2 * p + 1, rows, stride=SSM_BLOCK), :]], axis=1).astype(BF16)
          for p in range(STEP_PAIRS)]
    s = _dot(xp[0], ww_ref[0, 0])
    for p in range(1, STEP_PAIRS):
        s = s + _dot(xp[p], ww_ref[0, p])
    sre, sim = s[:, :half], s[:, half:]
    mu = mu_ref[0]
    mre, mim = mu[0:1, :], mu[1:2, :]
    blk = lax.broadcasted_iota(jnp.int32, (rows, half), 0) & (nb - 1)
    if has_h0:
        h0 = h0_ref[0]
        h0re, h0im = h0[:, :half], h0[:, half:]
        sre = sre + (mre * h0re - mim * h0im)
        sim = sim + (mre * h0im + mim * h0re)
    d = 1
    while d < nb:
        keep = blk >= d
        shr = jnp.where(keep, pltpu.roll(sre, d, 0), 0.0)
        shi = jnp.where(keep, pltpu.roll(sim, d, 0), 0.0)
        sre, sim = sre + (mre * shr - mim * shi), sim + (mre * shi + mim * shr)
        mre, mim = mre * mre - mim * mim, 2.0 * mre * mim
        d *= 2
    n_chunks = half // LANES
    for i in range(n_chunks):
        s_scr[i] = sre[:, i * LANES:(i + 1) * LANES]
        s_scr[n_chunks + i] = sim[:, i * LANES:(i + 1) * LANES]
    f_ref[0, 0] = jnp.concatenate([s_scr[i, pl.ds(nb - 1, n_seq, stride=nb), :] for i in range(2 * n_chunks)],
                                  axis=1)
    first = blk == 0
    inre = jnp.where(first, 0.0, pltpu.roll(sre, 1, 0))
    inim = jnp.where(first, 0.0, pltpu.roll(sim, 1, 0))
    if has_h0:
        inre = inre + h0re
        inim = inim + h0im
    s_in = jnp.concatenate([inre, inim], axis=1).astype(BF16)
    for p in range(STEP_PAIRS):
        y = _dot(s_in, zz_ref[0, p])
        for q in range(p + 1):
            y = y + _dot(xp[q], kk_ref[0, p - q])
        y_ref[pl.ds(2 * p, rows, stride=SSM_BLOCK), :] = y[:, :COLUMN_LANES]
        y_ref[pl.ds(2 * p + 1, rows, stride=SSM_BLOCK), :] = y[:, COLUMN_LANES:]


def _s5_block_diagonal(params):
    kmat, wre, wim, zre, zim, mu = params
    g, cg, nc = SSM_GROUPS, COLUMN_GROUPS, SSM_GROUPS // COLUMN_GROUPS
    eye = jnp.eye(cg, dtype=BF16)
    ktau = jnp.transpose(kmat[:, :SSM_GROUP, :].reshape(g, SSM_GROUP, SSM_BLOCK, SSM_GROUP), (0, 2, 1, 3))
    kpad = jnp.concatenate([jnp.zeros_like(ktau[:, :1]), ktau], axis=1)
    blocks = jnp.stack([jnp.stack([jnp.stack([kpad[:, 2 * d + t_lo - s_lo + 1] for t_lo in range(2)], axis=1)
                                   for s_lo in range(2)], axis=1) for d in range(STEP_PAIRS)], axis=1)
    blocks = blocks.reshape(nc, cg, STEP_PAIRS, 2, 2, SSM_GROUP, SSM_GROUP)
    kk = jnp.einsum('cgdsthk,gj->cdsghtjk', blocks, eye).reshape(nc, STEP_PAIRS, 2 * COLUMN_LANES, 2 * COLUMN_LANES)

    def expand_w(w):
        w6 = w.reshape(nc, cg, STEP_PAIRS, 2, SSM_GROUP, SSM_STATE)
        return jnp.einsum('cgpshn,gj->cpsghjn', w6, eye).reshape(nc, STEP_PAIRS, 2 * COLUMN_LANES, cg * SSM_STATE)

    def expand_z(z):
        z6 = z.reshape(nc, cg, STEP_PAIRS, 2, SSM_GROUP, SSM_STATE)
        return jnp.einsum('cgpthn,gj->cpgntjh', z6, eye).reshape(nc, STEP_PAIRS, cg * SSM_STATE, 2 * COLUMN_LANES)

    ww = jnp.concatenate([expand_w(wre), expand_w(wim)], axis=3)
    zz = jnp.concatenate([expand_z(zre), expand_z(zim)], axis=2)
    mu_c = jnp.transpose(mu.reshape(nc, cg, 2, SSM_STATE), (0, 2, 1, 3)).reshape(nc, 2, cg * SSM_STATE)
    return kk, ww, zz, mu_c


def _s5_core(u, bd, n_seq, seq_len, h0, fold_sequences):
    kk, ww, zz, mu_c = bd
    nc = kk.shape[0]
    nb = seq_len // SSM_BLOCK
    seqs = n_seq if fold_sequences else 1
    n_steps = n_seq // seqs
    rows_tok = seqs * seq_len
    rows = seqs * nb
    state = 2 * COLUMN_GROUPS * SSM_STATE
    col = pl.BlockSpec((rows_tok, COLUMN_LANES), lambda c, i: (i, c))
    per_c = lambda a: pl.BlockSpec((1,) + a.shape[1:], lambda c, i: (c,) + (0,) * (a.ndim - 1))
    in_specs = [col, per_c(kk), per_c(ww), per_c(zz), per_c(mu_c)]
    args = [u, kk, ww, zz, mu_c]
    if h0 is not None:
        in_specs.append(pl.BlockSpec((1, rows, state), lambda c, i: (c, i, 0)))
        args.append(h0)
    y, fin = pl.pallas_call(
        functools.partial(_s5_core_kernel, nb, seqs, h0 is not None),
        grid=(nc, n_steps),
        in_specs=in_specs,
        out_specs=[col, pl.BlockSpec((1, 1, seqs, state), lambda c, i: (c, i, 0, 0))],
        out_shape=[jax.ShapeDtypeStruct(u.shape, F32), jax.ShapeDtypeStruct((nc, n_steps, seqs, state), F32)],
        scratch_shapes=[pltpu.VMEM((state // LANES, rows, LANES), F32)],
        compiler_params=_cparams(2), name="s5_core",
    )(*args)
    return y, fin.reshape(nc, n_seq, state)


def _ffn_kernel(pre, final_norm, tiles_per_seq, tm, *refs):
    refs = list(refs)
    x_ref = refs.pop(0)
    if pre == "glu":
        ys_ref, gmix_ref, dskip_ref, wglu_ref = refs[:4]
        refs = refs[4:]
    else:
        o_ref, wo_ref = refs[:2]
        refs = refs[2:]
    (hist_ref, gffn_ref, wup_ref, wgate_ref, convw_ref, convb_ref, wdown_ref) = refs[:7]
    refs = refs[7:]
    if final_norm:
        gfin_ref = refs.pop(0)
    out_ref, cstate_ref, carry = refs

    x = x_ref[...]
    if pre == "glu":
        u = _rms(x, gmix_ref[...])
        g = _gelu(ys_ref[...] + dskip_ref[...] * u).astype(BF16)
        z = _dot(g, wglu_ref[...])
        x = x + z[:, :D_MODEL] * _sigmoid(z[:, D_MODEL:])
    else:
        x = x + _dot(o_ref[...], wo_ref[...])

    h = _rms(x, gffn_ref[...]).astype(BF16)

    @pl.when(pl.program_id(0) % tiles_per_seq == 0)
    def _():
        carry[...] = hist_ref[0]

    width = D_FF // FFN_CHUNKS
    row = lax.broadcasted_iota(jnp.int32, (tm, width), 0)
    for ch in range(FFN_CHUNKS):
        cs = slice(ch * width, (ch + 1) * width)
        a = _dot(h, wup_ref[:, cs])
        prev = carry[:, cs]
        a1 = jnp.where(row == 0, prev[1:2, :], pltpu.roll(a, 1, 0))
        a2 = jnp.where(row == 0, prev[0:1, :], jnp.where(row == 1, prev[1:2, :], pltpu.roll(a, 2, 0)))
        conv = a2 * convw_ref[0:1, cs] + a1 * convw_ref[1:2, cs] + a * convw_ref[2:3, cs] + convb_ref[:, cs]
        last2 = a[tm - (CONV_WIDTH - 1):, :]
        carry[:, cs] = last2
        cstate_ref[0, :, cs] = last2
        hh = (_gelu(conv) * _dot(h, wgate_ref[:, cs])).astype(BF16)
        x = x + _dot(hh, wdown_ref[cs, :])
    if final_norm:
        x = _rms(x, gfin_ref[...])
    out_ref[...] = x


def _ffn(x, pre, pre_args, hist, g_ffn, w_up, w_gate, conv_w, conv_b, w_down, g_final, seq_len, tm):
    t = x.shape[0]
    n_seq = t // seq_len
    tiles_per_seq = seq_len // tm
    tile = pl.BlockSpec((tm, D_MODEL), lambda i: (i, 0))
    vec = _full((1, D_MODEL))
    const = lambda a: pl.BlockSpec(a.shape, lambda i: (0,) * a.ndim, pipeline_mode=pl.Buffered(1))
    per_seq = pl.BlockSpec((1, CONV_WIDTH - 1, D_FF), lambda i: (i // tiles_per_seq, 0, 0))
    args, in_specs = [x], [tile]
    if pre == "glu":
        ys, g_mix, d_skip, w_glu = pre_args
        args += [ys, g_mix[None, :], d_skip[None, :], w_glu]
        in_specs += [tile, vec, vec, const(w_glu)]
    else:
        o, w_o = pre_args
        args += [o, w_o]
        in_specs += [tile, const(w_o)]
    args += [hist, g_ffn[None, :], w_up, w_gate, conv_w, conv_b[None, :], w_down]
    in_specs += [per_seq, vec, const(w_up), const(w_gate), _full((CONV_WIDTH, D_FF)), _full((1, D_FF)),
                 const(w_down)]
    if g_final is not None:
        args.append(g_final[None, :])
        in_specs.append(vec)
    return pl.pallas_call(
        functools.partial(_ffn_kernel, pre, g_final is not None, tiles_per_seq, tm),
        grid=(t // tm,),
        in_specs=in_specs,
        out_specs=[tile, per_seq],
        out_shape=[jax.ShapeDtypeStruct((t, D_MODEL), F32),
                   jax.ShapeDtypeStruct((n_seq, CONV_WIDTH - 1, D_FF), F32)],
        scratch_shapes=[pltpu.VMEM((CONV_WIDTH - 1, D_FF), F32)],
        compiler_params=_cparams(1), name="ffn_" + pre,
    )(*args)


def _c_pieces(c):
    hi, mid, lo = _split3(c * LOG2E)
    return (hi.astype(F32) + pltpu.roll(mid.astype(F32), FOX_HEADS, 1)
            + pltpu.roll(lo.astype(F32), 2 * FOX_HEADS, 1)).astype(BF16)


def _key_bias_lanes(c3):
    lane = lax.broadcasted_iota(jnp.int32, c3.shape, 1)
    ones = ((lane >= _CQ_LANE) & (lane < _CQ_LANE + 3)).astype(F32)
    return (ones - c3.astype(F32)).astype(BF16)


def _qkv_kernel(tiles_per_seq, tm, transposed_v, x_ref, g_ref, wq_ref, wk_ref, wv_ref, wf_ref, bf_ref,
                k_ref, v_ref, lf_ref, qb_ref, kb_ref, vb_ref, c3_ref, kc_ref, carry):
    u = _rms(x_ref[...], g_ref[...]).astype(BF16)
    q = _dot(u, wq_ref[...]) * (FOX_HEAD_DIM ** -0.5 * LOG2E)
    k = _dot(u, wk_ref[...])
    v = _dot(u, wv_ref[...])
    k_ref[...] = k
    v_ref[...] = v
    qb_ref[...] = q.astype(BF16)
    kb_ref[...] = k.astype(BF16)
    if transposed_v:
        vb_ref[0, 0] = jnp.transpose(v).astype(BF16)
    else:
        vb_ref[...] = v.astype(BF16)
    lane = lax.broadcasted_iota(jnp.int32, (tm, LANES), 1)
    logf = jnp.where(lane < FOX_HEADS, _log_sigmoid(_dot(u, wf_ref[...]) + bf_ref[...]), 0.0)
    lf_ref[...] = logf[:, :FOX_HEADS]

    @pl.when(pl.program_id(0) % tiles_per_seq == 0)
    def _():
        carry[...] = jnp.zeros_like(carry)

    c = _tri_cumsum(logf, tm) + carry[0:1, :]
    carry[...] = jnp.broadcast_to(c[tm - 1:tm, :], carry.shape)
    c3 = _c_pieces(c)
    c3_ref[...] = c3
    kc_ref[...] = _key_bias_lanes(c3)


def _qkv(x, g, wq, wk, wv, wf, bf, seq_len, tm, transposed_v):
    t = x.shape[0]
    tiles_per_seq = seq_len // tm
    tile = pl.BlockSpec((tm, D_MODEL), lambda i: (i, 0))
    narrow = pl.BlockSpec((tm, LANES), lambda i: (i, 0))
    const = lambda a: pl.BlockSpec(a.shape, lambda i: (0,) * a.ndim, pipeline_mode=pl.Buffered(1))
    if transposed_v:
        vb_spec = pl.BlockSpec((1, 1, D_MODEL, tm), lambda i: (i // tiles_per_seq, i % tiles_per_seq, 0, 0))
        vb_shape = jax.ShapeDtypeStruct((t // seq_len, tiles_per_seq, D_MODEL, tm), BF16)
    else:
        vb_spec, vb_shape = tile, jax.ShapeDtypeStruct((t, D_MODEL), BF16)
    return pl.pallas_call(
        functools.partial(_qkv_kernel, tiles_per_seq, tm, transposed_v),
        grid=(t // tm,),
        in_specs=[tile, _full((1, D_MODEL)), const(wq), const(wk), const(wv), const(wf), _full((1, LANES))],
        out_specs=[tile, tile, pl.BlockSpec((tm, FOX_HEADS), lambda i: (i, 0)), tile, tile, vb_spec,
                   narrow, narrow],
        out_shape=[jax.ShapeDtypeStruct((t, D_MODEL), F32), jax.ShapeDtypeStruct((t, D_MODEL), F32),
                   jax.ShapeDtypeStruct((t, FOX_HEADS), F32),
                   jax.ShapeDtypeStruct((t, D_MODEL), BF16), jax.ShapeDtypeStruct((t, D_MODEL), BF16),
                   vb_shape, jax.ShapeDtypeStruct((t, LANES), BF16), jax.ShapeDtypeStruct((t, LANES), BF16)],
        scratch_shapes=[pltpu.VMEM((8, LANES), F32)],
        compiler_params=_cparams(1), name="fox_qkv",
    )(x, g[None, :], wq, wk, wv, wf, bf)


def _query_ext(q, c3q, head, lo_half):
    rows = q.shape[0]
    lane = lax.broadcasted_iota(jnp.int32, (rows, LANES), 1)
    own = (lane < FOX_HEAD_DIM) if lo_half else (lane >= FOX_HEAD_DIM)
    qm = jnp.where(own, q, jnp.zeros_like(q))
    pick = lambda j: jnp.sum(jnp.where(lane == head + j * FOX_HEADS, c3q, 0.0), axis=1, keepdims=True)
    onehot = (lane == head) | (lane == head + FOX_HEADS) | (lane == head + 2 * FOX_HEADS)
    qc = jnp.where(lane == _CQ_LANE, pick(0),
                   jnp.where(lane == _CQ_LANE + 1, pick(1),
                             jnp.where(lane == _CQ_LANE + 2, pick(2), jnp.where(onehot, 1.0, 0.0))))
    return jnp.concatenate([qm, qc.astype(BF16)], axis=1)


def _key_ext(k, c3k):
    return jnp.concatenate([k, _key_bias_lanes(c3k)], axis=1)


def _softmax_step(s, vext, m_ref, acc_ref):
    n_keys = s.shape[1]
    m_prev = m_ref[...]
    m_new = jnp.maximum(m_prev, jnp.max(s, axis=1, keepdims=True))
    m_keys = m_new[:, :n_keys] if n_keys < LANES else jnp.tile(m_new, (1, n_keys // LANES))
    p = jnp.exp2(s - m_keys)
    alpha = jnp.exp2(m_prev - m_new)
    acc_ref[...] = acc_ref[...] * jnp.tile(alpha, (1, 2)) + _dot(p.astype(BF16), vext)
    m_ref[...] = m_new


def _sublane_allreduce(x, op):
    for sh in (4, 2, 1):
        x = op(x, pltpu.roll(x, sh, 0))
    return x


def _fold_rows(x, op):
    parts = [x[i * SUBLANES:(i + 1) * SUBLANES] for i in range(x.shape[0] // SUBLANES)]
    while len(parts) > 1:
        parts = [op(parts[i], parts[i + 1]) for i in range(0, len(parts), 2)]
    return parts[0]


ONES_ROWS = 16


def _attn_prompt_kernel(t, n_items, qi_tab, j_tab, q_ref, c3_ref, k_ref, kc_ref, vt_ref, o_ref,
                        qt, s_buf, cm_buf, p_buf, al_buf, acc):
    pair = pl.program_id(1)
    big = -NEG_INF
    for a in range(2):
        acc[a] = jnp.zeros(acc.shape[1:], F32)
        p_buf[1, a] = jnp.zeros((t, t), BF16)
        al_buf[1, a] = jnp.ones((SUBLANES, t), F32)
    ones_rows = jnp.ones((ONES_ROWS, t), BF16)

    def load_q(qi):
        start = pl.multiple_of(qi * t, t)
        q = q_ref[pl.ds(start, t), :]
        c3q = c3_ref[pl.ds(start, t), :].astype(F32)
        for a in range(2):
            qe = _query_ext(q, c3q, 2 * pair + a, a == 0)
            qt[a] = jnp.transpose(qe.astype(F32)).astype(BF16)

    def stage_a(w, slot, diagonal):
        start = pl.multiple_of(j_tab[w] * t, t)
        kext = jnp.concatenate([k_ref[pl.ds(start, t), :], kc_ref[pl.ds(start, t), :]], axis=1)
        width = 2 * LANES
        for a in range(2):
            for q0 in range(0, t, width):
                s = _dot(kext, qt[a, :, q0:q0 + width])
                if diagonal:
                    r = lax.broadcasted_iota(jnp.int32, (t, width), 0)
                    c = lax.broadcasted_iota(jnp.int32, (t, width), 1) + q0
                    s = jnp.where(r <= c, s, NEG_INF)
                s_buf[slot, a, :, q0:q0 + width] = s
                cm_buf[slot, a, :, q0:q0 + width] = _sublane_allreduce(_fold_rows(s, jnp.maximum), jnp.maximum)

    def stage_b(w, ms, slot):
        m_floor = jnp.where(j_tab[w] == 0, NEG_INF, big)
        new_ms = []
        for a in range(2):
            m_old = jnp.minimum(ms[a], m_floor)
            m_new = jnp.maximum(m_old, cm_buf[slot, a])
            p_buf[slot, a] = jnp.exp2(s_buf[slot, a] - jnp.tile(m_new, (t // SUBLANES, 1))).astype(BF16)
            al_buf[slot, a] = jnp.exp2(m_old - m_new)
            new_ms.append(m_new)
        return tuple(new_ms)

    def stage_c(w, slot):
        vt = jnp.concatenate([vt_ref[0, j_tab[w]], ones_rows], axis=0)
        for a in range(2):
            acc[a] = (acc[a] * jnp.tile(al_buf[slot, a], (acc.shape[1] // SUBLANES, 1))
                      + _dot(vt, p_buf[slot, a]))

    def finalize(w):
        outs = []
        for a in range(2):
            full = acc[a]
            denom = jnp.tile(full[PAIR_LANES:PAIR_LANES + SUBLANES], (PAIR_LANES // SUBLANES, 1))
            outs.append(full[:PAIR_LANES] / denom)
        row = lax.broadcasted_iota(jnp.int32, (PAIR_LANES, t), 0)
        o = jnp.transpose(jnp.where(row < FOX_HEAD_DIM, outs[0], outs[1]))
        o_ref[pl.ds(pl.multiple_of(qi_tab[w] * t, t), t), :] = o.astype(o_ref.dtype)

    load_q(0)
    stage_a(0, 0, True)

    def item(w, ms, slot):
        nxt = jnp.minimum(w + 1, n_items - 1)
        prev = jnp.maximum(w - 1, 0)

        @pl.when(j_tab[nxt] == 0)
        def _():
            load_q(qi_tab[nxt])

        def stages(diagonal, ms):
            stage_c(prev, 1 - slot)
            ms = stage_b(w, ms, slot)
            stage_a(nxt, 1 - slot, diagonal)
            return ms

        ms = lax.cond(j_tab[nxt] == qi_tab[nxt], functools.partial(stages, True),
                      functools.partial(stages, False), ms)

        @pl.when((w >= 1) & (j_tab[prev] == qi_tab[prev]))
        def _():
            finalize(prev)

        return ms

    def body(i, ms):
        return item(2 * i + 1, item(2 * i, ms, 0), 1)

    ms = lax.fori_loop(0, n_items // 2, body, tuple(jnp.full((SUBLANES, t), NEG_INF, F32) for _ in range(2)))
    if n_items % 2:
        item(n_items - 1, ms, 0)
    stage_c(n_items - 1, (n_items - 1) % 2)
    finalize(n_items - 1)


def _attn_prompt(qb, kb, kc, vt, c3, n_seq, seq_len, t):
    total = qb.shape[0]
    nt = seq_len // t
    items = [(qi, j) for qi in range(nt) for j in range(qi + 1)]
    qi_tab = jnp.asarray([qi for qi, _ in items], jnp.int32)
    j_tab = jnp.asarray([j for _, j in items], jnp.int32)
    seq_pair = pl.BlockSpec((seq_len, PAIR_LANES), lambda b, p, *_: (b, p))
    seq_bias = pl.BlockSpec((seq_len, LANES), lambda b, p, *_: (b, 0))
    vt_pair = pl.BlockSpec((1, nt, PAIR_LANES, t), lambda b, p, *_: (b, 0, p, 0))
    return pl.pallas_call(
        functools.partial(_attn_prompt_kernel, t, len(items)),
        grid_spec=pltpu.PrefetchScalarGridSpec(
            num_scalar_prefetch=2,
            grid=(n_seq, HEAD_PAIRS),
            in_specs=[seq_pair, seq_bias, seq_pair, seq_bias, vt_pair],
            out_specs=seq_pair,
            scratch_shapes=[pltpu.VMEM((2, 2 * LANES, t), BF16), pltpu.VMEM((2, 2, t, t), F32),
                            pltpu.VMEM((2, 2, SUBLANES, t), F32),
                            pltpu.VMEM((2, 2, t, t), BF16), pltpu.VMEM((2, 2, SUBLANES, t), F32),
                            pltpu.VMEM((2, PAIR_LANES + ONES_ROWS, t), F32)]),
        out_shape=jax.ShapeDtypeStruct((total, D_MODEL), BF16),
        compiler_params=_cparams(2), name="fox_attn_prompt",
    )(qi_tab, j_tab, qb, c3, kb, kc, vt)


def _attn_sample_kernel(n_cache_tiles, tk, t_new, q_ref, c3q_ref, kc_ref, vc_ref, lfc_ref,
                        kn_ref, vn_ref, lfn_ref, o_ref, qext, m_scr, acc_scr, carry):
    j = pl.program_id(1)
    rows = 2 * t_new

    @pl.when(j == 0)
    def _():
        c3q = c3q_ref[...].astype(F32)
        for p in range(HEAD_PAIRS):
            q = q_ref[:, p * PAIR_LANES:(p + 1) * PAIR_LANES]
            qext[p] = jnp.concatenate([_query_ext(q, c3q, 2 * p, True), _query_ext(q, c3q, 2 * p + 1, False)],
                                      axis=0)
        m_scr[...] = jnp.full(m_scr.shape, NEG_INF, F32)
        acc_scr[...] = jnp.zeros(acc_scr.shape, F32)
        carry[...] = jnp.zeros_like(carry)

    def attend(kb, vb, c3k, masked):
        n_keys = kb.shape[0]
        ones = jnp.ones((n_keys, LANES), BF16)
        for p in range(HEAD_PAIRS):
            sl = slice(p * PAIR_LANES, (p + 1) * PAIR_LANES)
            s = _dot_nt(qext[p], _key_ext(kb[:, sl], c3k))
            if masked:
                r = lax.broadcasted_iota(jnp.int32, (rows, n_keys), 0) & (t_new - 1)
                c = lax.broadcasted_iota(jnp.int32, (rows, n_keys), 1)
                s = jnp.where(c <= r, s, NEG_INF)
            _softmax_step(s, jnp.concatenate([vb[:, sl], ones], axis=1), m_scr.at[p], acc_scr.at[p])

    @pl.when(j < n_cache_tiles)
    def _():
        c = _tri_cumsum(lfc_ref[0], tk) + carry[0:1, :]
        carry[...] = jnp.broadcast_to(c[tk - 1:tk, :], carry.shape)
        attend(kc_ref[0].astype(BF16), vc_ref[0].astype(BF16), _c_pieces(c), False)

    @pl.when(j == n_cache_tiles)
    def _():
        c = _tri_cumsum(lfn_ref[...], t_new) + carry[0:1, :]
        attend(kn_ref[...], vn_ref[...], _c_pieces(c), True)
        lane = lax.broadcasted_iota(jnp.int32, (t_new, LANES), 1)
        for p in range(HEAD_PAIRS):
            acc = acc_scr[p]
            oa = acc[:t_new, :LANES] / acc[:t_new, LANES:]
            ob = acc[t_new:, :LANES] / acc[t_new:, LANES:]
            o_ref[:, p * PAIR_LANES:(p + 1) * PAIR_LANES] = jnp.where(lane < FOX_HEAD_DIM, oa, ob).astype(o_ref.dtype)


def _attn_sample(qb, kb, vb, c3, logf, cache_k, cache_v, cache_logf, tk):
    n_seq, past, _ = cache_k.shape
    t_new = qb.shape[0] // n_seq
    nct = past // tk
    pad = lambda a: jnp.pad(a, ((0, 0),) * (a.ndim - 1) + ((0, LANES - FOX_HEADS),))
    new_tile = pl.BlockSpec((t_new, D_MODEL), lambda b, j: (b, 0))
    new_c3 = pl.BlockSpec((t_new, LANES), lambda b, j: (b, 0))
    cache_tile = pl.BlockSpec((1, tk, D_MODEL), lambda b, j: (b, jnp.minimum(j, nct - 1), 0))
    lf_tile = pl.BlockSpec((1, tk, LANES), lambda b, j: (b, jnp.minimum(j, nct - 1), 0))
    return pl.pallas_call(
        functools.partial(_attn_sample_kernel, nct, tk, t_new),
        grid=(n_seq, nct + 1),
        in_specs=[new_tile, new_c3, cache_tile, cache_tile, lf_tile, new_tile, new_tile, new_c3],
        out_specs=new_tile,
        out_shape=jax.ShapeDtypeStruct((n_seq * t_new, D_MODEL), BF16),
        scratch_shapes=[pltpu.VMEM((HEAD_PAIRS, 2 * t_new, 2 * LANES), BF16),
                        pltpu.VMEM((HEAD_PAIRS, 2 * t_new, LANES), F32),
                        pltpu.VMEM((HEAD_PAIRS, 2 * t_new, 2 * LANES), F32),
                        pltpu.VMEM((8, LANES), F32)],
        compiler_params=_cparams(2), name="fox_attn_sample",
    )(qb, c3, cache_k, cache_v, pad(cache_logf), kb, vb, pad(logf))


def _trunk(x, n_seq, seq_len, ssm_h0, fox_cache, conv_hist, w, tm, tq):
    t = n_seq * seq_len
    nb = seq_len // SSM_BLOCK
    nc = SSM_GROUPS // COLUMN_GROUPS
    half = COLUMN_GROUPS * SSM_STATE
    x = x.reshape(t, D_MODEL)

    u0 = _norm(x, w["norm_mix"][0], tm)
    if ssm_h0 is None:
        h0 = None
    else:
        per_c = lambda h: jnp.swapaxes(h.reshape(n_seq, nc, half), 0, 1)
        h0c = jnp.concatenate([per_c(ssm_h0[0]), per_c(ssm_h0[1])], axis=2)
        h0 = jnp.zeros((nc, n_seq, nb, 2 * half), F32).at[:, :, 0, :].set(h0c).reshape(nc, n_seq * nb, 2 * half)
    ys, fin = _s5_core(u0, w["s5"], n_seq, seq_len, h0, fold_sequences=ssm_h0 is not None)
    per_g = lambda f: jnp.swapaxes(f.reshape(nc, n_seq, COLUMN_GROUPS, SSM_STATE), 0, 1).reshape(
        n_seq, SSM_GROUPS, SSM_STATE)
    fre, fim = per_g(fin[:, :, :half]), per_g(fin[:, :, half:])
    x, conv0 = _ffn(x, "glu", (ys, w["norm_mix"][0], w["ssm_d"], w["w_glu"]), conv_hist[0],
                    w["norm_ffn"][0], w["w_up"][0], w["w_gate"][0], w["conv_w"][0], w["conv_b"][0],
                    w["w_down"][0], None, seq_len, tm)

    k, v, logf, qb, kb, vb, c3, kc = _qkv(x, w["norm_mix"][1], w["wq"], w["wk"], w["wv"], w["wf"], w["bf"],
                                          seq_len, tq, transposed_v=fox_cache is None)
    if fox_cache is None:
        o = _attn_prompt(qb, kb, kc, vb, c3, n_seq, seq_len, tq)
    else:
        o = _attn_sample(qb, kb, vb, c3, logf, *fox_cache, tk=min(512, fox_cache[0].shape[1]))
    y, conv1 = _ffn(x, "oproj", (o, w["w_o"]), conv_hist[1],
                    w["norm_ffn"][1], w["w_up"][1], w["w_gate"][1], w["conv_w"][1], w["conv_b"][1],
                    w["w_down"][1], w["norm_final"], seq_len, tm)

    shape4 = (1, n_seq, seq_len, FOX_HEADS, FOX_HEAD_DIM)
    return (y.reshape(n_seq, seq_len, D_MODEL),
            fre[None], fim[None],
            k.reshape(shape4), v.reshape(shape4), logf.reshape(1, n_seq, seq_len, FOX_HEADS),
            jnp.stack([conv0, conv1]))


def kernel(x_prompt, x_sample, state_ssm_re, state_ssm_im, cache_fox_k, cache_fox_v, cache_fox_logf, state_ffn_conv, norm_mix, norm_ffn, norm_final, ssm_a_re, ssm_a_im, ssm_log_step, ssm_b_re, ssm_b_im, ssm_c_re, ssm_c_im, ssm_d, ssm_w_glu, fox_w_qkvf, fox_b_f, fox_w_o, ffn_w_up, ffn_w_gate, ffn_conv_w, ffn_conv_b, ffn_w_down):
    n_p, l_p, _ = x_prompt.shape
    n_s, l_s, _ = x_sample.shape
    past = cache_fox_k.shape[2]
    wqkvf = fox_w_qkvf[0]
    w = {
        "norm_mix": norm_mix, "norm_ffn": norm_ffn, "norm_final": norm_final,
        "s5": _s5_block_diagonal(_s5_params(ssm_a_re[0], ssm_a_im[0], ssm_log_step[0], ssm_b_re[0],
                                            ssm_b_im[0], ssm_c_re[0], ssm_c_im[0])),
        "ssm_d": ssm_d[0], "w_glu": ssm_w_glu[0].astype(BF16),
        "wq": wqkvf[:, :D_MODEL].astype(BF16), "wk": wqkvf[:, D_MODEL:2 * D_MODEL].astype(BF16),
        "wv": wqkvf[:, 2 * D_MODEL:3 * D_MODEL].astype(BF16),
        "wf": jnp.pad(wqkvf[:, 3 * D_MODEL:], ((0, 0), (0, LANES - FOX_HEADS))).astype(BF16),
        "bf": jnp.pad(fox_b_f[0], (0, LANES - FOX_HEADS))[None, :],
        "w_o": fox_w_o[0].astype(BF16),
        "w_up": ffn_w_up.astype(BF16), "w_gate": ffn_w_gate.astype(BF16), "w_down": ffn_w_down.astype(BF16),
        "conv_w": ffn_conv_w, "conv_b": ffn_conv_b,
    }
    zero_hist = jnp.zeros((2, n_p, CONV_WIDTH - 1, D_FF), F32)
    outs_p = _trunk(x_prompt, n_p, l_p, None, None, zero_hist, w, tm=512, tq=512)
    cache = (cache_fox_k[0].reshape(n_s, past, D_MODEL), cache_fox_v[0].reshape(n_s, past, D_MODEL),
             cache_fox_logf[0])
    outs_s = _trunk(x_sample, n_s, l_s, (state_ssm_re[0], state_ssm_im[0]), cache, state_ffn_conv, w,
                    tm=l_s, tq=l_s)
    return (outs_p[0], outs_s[0]) + outs_p[1:] + outs_s[1:]
```

```python
import functools
import math

import jax
import jax.numpy as jnp
from jax import lax
from jax.experimental import pallas as pl
from jax.experimental.pallas import tpu as pltpu

D_MODEL = 1024
SSM_GROUP = 16
SSM_GROUPS = D_MODEL // SSM_GROUP
SSM_STATE = 64
SSM_BLOCK = 16
SSM_BLOCK_LOG2 = 4
BLOCK_LANES = SSM_BLOCK * SSM_GROUP
COLUMN_LANES = 128
COLUMN_GROUPS = COLUMN_LANES // SSM_GROUP
STEP_PAIRS = SSM_BLOCK // 2
FOX_HEADS = 16
FOX_HEAD_DIM = 64
HEAD_PAIRS = FOX_HEADS // 2
PAIR_LANES = 2 * FOX_HEAD_DIM
D_FF = 2816
CONV_WIDTH = 3
FFN_CHUNK_EDGES = (0, 1536, D_FF)
NORM_EPS = 1e-6
NEG_INF = -1e30
LOG2E = 1.4426950408889634
LANES = 128
SUBLANES = 8
_CQ_LANE = 3 * FOX_HEADS
VMEM_LIMIT = 56 * 1024 * 1024

BF16 = jnp.bfloat16
F32 = jnp.float32


def _cparams(n_axes, vmem=VMEM_LIMIT):
    return pltpu.CompilerParams(dimension_semantics=("arbitrary",) * n_axes, vmem_limit_bytes=vmem)


def _full(shape):
    nd = len(shape)
    return pl.BlockSpec(shape, lambda *_: (0,) * nd)


def _rms(x, g):
    ms = jnp.mean(x * x, axis=-1, keepdims=True)
    return x * lax.rsqrt(ms + NORM_EPS) * g


def _gelu(x):
    c = math.sqrt(2.0 / math.pi)
    return 0.5 * x * (1.0 + jnp.tanh(c * (x + 0.044715 * (x * x * x))))


def _sigmoid(x):
    return 1.0 / (1.0 + jnp.exp(-x))


def _log_sigmoid(x):
    return jnp.minimum(x, 0.0) - jnp.log1p(jnp.exp(-jnp.abs(x)))


def _split3(x):
    hi = x.astype(BF16)
    r1 = x - hi.astype(F32)
    mid = r1.astype(BF16)
    lo = (r1 - mid.astype(F32)).astype(BF16)
    return hi, mid, lo


def _dot(a, b):
    return jnp.dot(a, b, preferred_element_type=F32)


def _dot_nt(a, b):
    return lax.dot_general(a, b, (((1,), (1,)), ((), ())), preferred_element_type=F32)


def _dot_nt_f32(a, b):
    a0, a1, a2 = _split3(a)
    b0, b1, b2 = _split3(b)
    return (_dot_nt(a0, b0) + (_dot_nt(a0, b1) + _dot_nt(a1, b0))
            + (_dot_nt(a0, b2) + _dot_nt(a1, b1) + _dot_nt(a2, b0)))


def _tri_cumsum(x, n):
    r = lax.broadcasted_iota(jnp.int32, (n, n), 0)
    c = lax.broadcasted_iota(jnp.int32, (n, n), 1)
    tri = jnp.where(c <= r, 1.0, 0.0).astype(BF16)
    hi, mid, lo = _split3(x)
    return _dot(tri, hi) + _dot(tri, mid) + _dot(tri, lo)


def _s5_param_kernel(are_r, aim_r, ls_ref, btre, btim, ctre, ctim,
                     k_ref, wre_ref, wim_ref, zre_ref, zim_ref, mu_ref):
    dt = jnp.exp(ls_ref[0])
    lre, lim = are_r[0], aim_r[0]
    mag = jnp.exp(lre * dt)
    abr = mag * jnp.cos(lim * dt)
    abi = mag * jnp.sin(lim * dt)
    den = lre * lre + lim * lim
    nr = abr - 1.0
    zre = (nr * lre + abi * lim) / den
    zim = (abi * lre - nr * lim) / den

    def cpow(e):
        m = jnp.exp(lre * dt * e)
        a = lim * dt * e
        return m * jnp.cos(a), m * jnp.sin(a)

    def per_step(e):
        step = lax.broadcasted_iota(jnp.int32, (SSM_BLOCK, SSM_STATE), 0).astype(F32)
        pr, pi = cpow(e(step))
        rep = lambda t: jnp.broadcast_to(t[:, None, :], (SSM_BLOCK, SSM_GROUP, SSM_STATE)).reshape(
            BLOCK_LANES, SSM_STATE)
        return rep(pr), rep(pi)

    bre, bim = btre[0], btim[0]
    bbre = zre * bre - zim * bim
    bbim = zre * bim + zim * bre
    pwr, pwi = per_step(lambda s: (SSM_BLOCK - 1.0) - s)
    wre_ref[0] = (pwr * bbre - pwi * bbim).astype(BF16)
    wim_ref[0] = (pwr * bbim + pwi * bbre).astype(BF16)
    pbr, pbi = per_step(lambda s: -s)
    bxre = pbr * bbre - pbi * bbim
    bxim = pbr * bbim + pbi * bbre

    cre, cim = ctre[0], ctim[0]
    pcr, pci = per_step(lambda t: t)
    cxre = pcr * cre - pci * cim
    cxim = pcr * cim + pci * cre
    pzr, pzi = per_step(lambda t: t + 1.0)
    zre_ref[0] = (pzr * cre - pzi * cim).astype(BF16)
    zim_ref[0] = (-(pzr * cim + pzi * cre)).astype(BF16)

    kfull = _dot_nt_f32(bxre, cxre) - _dot_nt_f32(bxim, cxim)
    rs = lax.broadcasted_iota(jnp.int32, (BLOCK_LANES, BLOCK_LANES), 0) >> SSM_BLOCK_LOG2
    ct = lax.broadcasted_iota(jnp.int32, (BLOCK_LANES, BLOCK_LANES), 1) >> SSM_BLOCK_LOG2
    k_ref[0] = jnp.where(ct >= rs, kfull, 0.0).astype(BF16)

    mur, mui = cpow(float(SSM_BLOCK))
    mu_ref[0, 0:1, :] = mur
    mu_ref[0, 1:2, :] = mui


def _s5_params(a_re, a_im, log_step, b_re, b_im, c_re, c_im):
    g, p = a_re.shape
    bt = jnp.tile(jnp.swapaxes(b_re, 1, 2), (1, SSM_BLOCK, 1)), jnp.tile(jnp.swapaxes(b_im, 1, 2), (1, SSM_BLOCK, 1))
    ct = jnp.tile(c_re, (1, SSM_BLOCK, 1)), jnp.tile(c_im, (1, SSM_BLOCK, 1))
    row = pl.BlockSpec((1, 1, p), lambda i: (i, 0, 0))
    one = pl.BlockSpec((1, 1, 1), lambda i: (i, 0, 0))
    tall = pl.BlockSpec((1, BLOCK_LANES, p), lambda i: (i, 0, 0))
    return pl.pallas_call(
        _s5_param_kernel,
        grid=(g,),
        in_specs=[row, row, one, tall, tall, tall, tall],
        out_specs=[pl.BlockSpec((1, BLOCK_LANES, BLOCK_LANES), lambda i: (i, 0, 0)),
                   tall, tall, tall, tall, pl.BlockSpec((1, 2, p), lambda i: (i, 0, 0))],
        out_shape=[jax.ShapeDtypeStruct((g, BLOCK_LANES, BLOCK_LANES), BF16)]
        + [jax.ShapeDtypeStruct((g, BLOCK_LANES, p), BF16)] * 4
        + [jax.ShapeDtypeStruct((g, 2, p), F32)],
        compiler_params=_cparams(1),
        name="s5_params",
    )(a_re[:, None, :], a_im[:, None, :], log_step[:, None, None], bt[0], bt[1], ct[0], ct[1])


def _norm_kernel(x_ref, g_ref, u_ref):
    u_ref[...] = _rms(x_ref[...], g_ref[...]).astype(u_ref.dtype)


def _norm(x, g, tm):
    t = x.shape[0]
    tile = pl.BlockSpec((tm, D_MODEL), lambda i: (i, 0))
    return pl.pallas_call(
        _norm_kernel, grid=(t // tm,),
        in_specs=[tile, _full((1, D_MODEL))], out_specs=tile,
        out_shape=jax.ShapeDtypeStruct((t, D_MODEL), F32),
        compiler_params=_cparams(1), name="s5_prenorm",
    )(x, g[None, :])


def _s5_core_kernel(nb, n_seq, has_h0, *refs):
    if has_h0:
        u_ref, kk_ref, ww_ref, zz_ref, mu_ref, h0_ref, y_ref, f_ref, s_scr = refs
    else:
        u_ref, kk_ref, ww_ref, zz_ref, mu_ref, y_ref, f_ref, s_scr = refs
    rows = nb * n_seq
    half = COLUMN_GROUPS * SSM_STATE
    xp = [jnp.concatenate([u_ref[pl.ds(2 * p, rows, stride=SSM_BLOCK), :],
                           u_ref[pl.ds(2 * p + 1, rows, stride=SSM_BLOCK), :]], axis=1).astype(BF16)
          for p in range(STEP_PAIRS)]
    s = _dot(xp[0], ww_ref[0, 0])
    for p in range(1, STEP_PAIRS):
        s = s + _dot(xp[p], ww_ref[0, p])
    sre, sim = s[:, :half], s[:, half:]
    mu = mu_ref[0]
    mre, mim = mu[0:1, :], mu[1:2, :]
    blk = lax.broadcasted_iota(jnp.int32, (rows, half), 0) & (nb - 1)
    if has_h0:
        h0 = h0_ref[0]
        h0re, h0im = h0[:, :half], h0[:, half:]
        sre = sre + (mre * h0re - mim * h0im)
        sim = sim + (mre * h0im + mim * h0re)
    d = 1
    while d < nb:
        keep = blk >= d
        shr = jnp.where(keep, pltpu.roll(sre, d, 0), 0.0)
        shi = jnp.where(keep, pltpu.roll(sim, d, 0), 0.0)
        sre, sim = sre + (mre * shr - mim * shi), sim + (mre * shi + mim * shr)
        mre, mim = mre * mre - mim * mim, 2.0 * mre * mim
        d *= 2
    n_chunks = half // LANES
    for i in range(n_chunks):
        s_scr[i] = sre[:, i * LANES:(i + 1) * LANES]
        s_scr[n_chunks + i] = sim[:, i * LANES:(i + 1) * LANES]
    f_ref[0, 0] = jnp.concatenate([s_scr[i, pl.ds(nb - 1, n_seq, stride=nb), :] for i in range(2 * n_chunks)],
                                  axis=1)
    first = blk == 0
    inre = jnp.where(first, 0.0, pltpu.roll(sre, 1, 0))
    inim = jnp.where(first, 0.0, pltpu.roll(sim, 1, 0))
    if has_h0:
        inre = inre + h0re
        inim = inim + h0im
    s_in = jnp.concatenate([inre, inim], axis=1).astype(BF16)
    for p in range(STEP_PAIRS):
        y = _dot(s_in, zz_ref[0, p])
        for q in range(p + 1):
            y = y + _dot(xp[q], kk_ref[0, p - q])
        y_ref[pl.ds(2 * p, rows, stride=SSM_BLOCK), :] = y[:, :COLUMN_LANES]
        y_ref[pl.ds(2 * p + 1, rows, stride=SSM_BLOCK), :] = y[:, COLUMN_LANES:]


def _s5_block_diagonal(params):
    kmat, wre, wim, zre, zim, mu = params
    g, cg, nc = SSM_GROUPS, COLUMN_GROUPS, SSM_GROUPS // COLUMN_GROUPS
    eye = jnp.eye(cg, dtype=BF16)
    ktau = jnp.transpose(kmat[:, :SSM_GROUP, :].reshape(g, SSM_GROUP, SSM_BLOCK, SSM_GROUP), (0, 2, 1, 3))
    kpad = jnp.concatenate([jnp.zeros_like(ktau[:, :1]), ktau], axis=1)
    blocks = jnp.stack([jnp.stack([jnp.stack([kpad[:, 2 * d + t_lo - s_lo + 1] for t_lo in range(2)], axis=1)
                                   for s_lo in range(2)], axis=1) for d in range(STEP_PAIRS)], axis=1)
    blocks = blocks.reshape(nc, cg, STEP_PAIRS, 2, 2, SSM_GROUP, SSM_GROUP)
    kk = jnp.einsum('cgdsthk,gj->cdsghtjk', blocks, eye).reshape(nc, STEP_PAIRS, 2 * COLUMN_LANES, 2 * COLUMN_LANES)

    def expand_w(w):
        w6 = w.reshape(nc, cg, STEP_PAIRS, 2, SSM_GROUP, SSM_STATE)
        return jnp.einsum('cgpshn,gj->cpsghjn', w6, eye).reshape(nc, STEP_PAIRS, 2 * COLUMN_LANES, cg * SSM_STATE)

    def expand_z(z):
        z6 = z.reshape(nc, cg, STEP_PAIRS, 2, SSM_GROUP, SSM_STATE)
        return jnp.einsum('cgpthn,gj->cpgntjh', z6, eye).reshape(nc, STEP_PAIRS, cg * SSM_STATE, 2 * COLUMN_LANES)

    ww = jnp.concatenate([expand_w(wre), expand_w(wim)], axis=3)
    zz = jnp.concatenate([expand_z(zre), expand_z(zim)], axis=2)
    mu_c = jnp.transpose(mu.reshape(nc, cg, 2, SSM_STATE), (0, 2, 1, 3)).reshape(nc, 2, cg * SSM_STATE)
    return kk, ww, zz, mu_c


def _s5_core(u, bd, n_seq, seq_len, h0, fold_sequences):
    kk, ww, zz, mu_c = bd
    nc = kk.shape[0]
    nb = seq_len // SSM_BLOCK
    seqs = n_seq if fold_sequences else 1
    n_steps = n_seq // seqs
    rows_tok = seqs * seq_len
    rows = seqs * nb
    state = 2 * COLUMN_GROUPS * SSM_STATE
    col = pl.BlockSpec((rows_tok, COLUMN_LANES), lambda c, i: (i, c))
    per_c = lambda a: pl.BlockSpec((1,) + a.shape[1:], lambda c, i: (c,) + (0,) * (a.ndim - 1))
    in_specs = [col, per_c(kk), per_c(ww), per_c(zz), per_c(mu_c)]
    args = [u, kk, ww, zz, mu_c]
    if h0 is not None:
        in_specs.append(pl.BlockSpec((1, rows, state), lambda c, i: (c, i, 0)))
        args.append(h0)
    y, fin = pl.pallas_call(
        functools.partial(_s5_core_kernel, nb, seqs, h0 is not None),
        grid=(nc, n_steps),
        in_specs=in_specs,
        out_specs=[col, pl.BlockSpec((1, 1, seqs, state), lambda c, i: (c, i, 0, 0))],
        out_shape=[jax.ShapeDtypeStruct(u.shape, F32), jax.ShapeDtypeStruct((nc, n_steps, seqs, state), F32)],
        scratch_shapes=[pltpu.VMEM((state // LANES, rows, LANES), F32)],
        compiler_params=_cparams(2), name="s5_core",
    )(*args)
    return y, fin.reshape(nc, n_seq, state)


def _ffn_kernel(pre, final_norm, tiles_per_seq, tm, *refs):
    refs = list(refs)
    x_ref = refs.pop(0)
    if pre == "glu":
        ys_ref, gmix_ref, dskip_ref, wglu_ref = refs[:4]
        refs = refs[4:]
    else:
        o_ref, wo_ref = refs[:2]
        refs = refs[2:]
    (hist_ref, gffn_ref, wup_ref, wgate_ref, convw_ref, convb_ref, wdown_ref) = refs[:7]
    refs = refs[7:]
    if final_norm:
        gfin_ref = refs.pop(0)
    out_ref, cstate_ref, carry = refs

    x = x_ref[...]
    if pre == "glu":
        u = _rms(x, gmix_ref[...])
        g = _gelu(ys_ref[...] + dskip_ref[...] * u).astype(BF16)
        z = _dot(g, wglu_ref[...])
        x = x + z[:, :D_MODEL] * _sigmoid(z[:, D_MODEL:])
    else:
        x = x + _dot(o_ref[...], wo_ref[...])

    h = _rms(x, gffn_ref[...]).astype(BF16)

    @pl.when(pl.program_id(0) % tiles_per_seq == 0)
    def _():
        carry[...] = hist_ref[0]

    for lo, hi in zip(FFN_CHUNK_EDGES[:-1], FFN_CHUNK_EDGES[1:]):
        cs = slice(lo, hi)
        row = lax.broadcasted_iota(jnp.int32, (tm, hi - lo), 0)
        a = _dot(h, wup_ref[:, cs])
        prev = carry[:, cs]
        a1 = jnp.where(row == 0, prev[1:2, :], pltpu.roll(a, 1, 0))
        a2 = jnp.where(row == 0, prev[0:1, :], jnp.where(row == 1, prev[1:2, :], pltpu.roll(a, 2, 0)))
        conv = a2 * convw_ref[0:1, cs] + a1 * convw_ref[1:2, cs] + a * convw_ref[2:3, cs] + convb_ref[:, cs]
        last2 = a[tm - (CONV_WIDTH - 1):, :]
        carry[:, cs] = last2
        cstate_ref[0, :, cs] = last2
        hh = (_gelu(conv) * _dot(h, wgate_ref[:, cs])).astype(BF16)
        x = x + _dot(hh, wdown_ref[cs, :])
    if final_norm:
        x = _rms(x, gfin_ref[...])
    out_ref[...] = x


def _ffn(x, pre, pre_args, hist, g_ffn, w_up, w_gate, conv_w, conv_b, w_down, g_final, seq_len, tm):
    t = x.shape[0]
    n_seq = t // seq_len
    tiles_per_seq = seq_len // tm
    tile = pl.BlockSpec((tm, D_MODEL), lambda i: (i, 0))
    vec = _full((1, D_MODEL))
    const = lambda a: pl.BlockSpec(a.shape, lambda i: (0,) * a.ndim, pipeline_mode=pl.Buffered(1))
    per_seq = pl.BlockSpec((1, CONV_WIDTH - 1, D_FF), lambda i: (i // tiles_per_seq, 0, 0))
    args, in_specs = [x], [tile]
    if pre == "glu":
        ys, g_mix, d_skip, w_glu = pre_args
        args += [ys, g_mix[None, :], d_skip[None, :], w_glu]
        in_specs += [tile, vec, vec, const(w_glu)]
    else:
        o, w_o = pre_args
        args += [o, w_o]
        in_specs += [tile, const(w_o)]
    args += [hist, g_ffn[None, :], w_up, w_gate, conv_w, conv_b[None, :], w_down]
    in_specs += [per_seq, vec, const(w_up), const(w_gate), _full((CONV_WIDTH, D_FF)), _full((1, D_FF)),
                 const(w_down)]
    if g_final is not None:
        args.append(g_final[None, :])
        in_specs.append(vec)
    return pl.pallas_call(
        functools.partial(_ffn_kernel, pre, g_final is not None, tiles_per_seq, tm),
        grid=(t // tm,),
        in_specs=in_specs,
        out_specs=[tile, per_seq],
        out_shape=[jax.ShapeDtypeStruct((t, D_MODEL), F32),
                   jax.ShapeDtypeStruct((n_seq, CONV_WIDTH - 1, D_FF), F32)],
        scratch_shapes=[pltpu.VMEM((CONV_WIDTH - 1, D_FF), F32)],
        compiler_params=_cparams(1), name="ffn_" + pre,
    )(*args)


def _c_pieces(c):
    hi, mid, lo = _split3(c * LOG2E)
    return (hi.astype(F32) + pltpu.roll(mid.astype(F32), FOX_HEADS, 1)
            + pltpu.roll(lo.astype(F32), 2 * FOX_HEADS, 1)).astype(BF16)


def _key_bias_lanes(c3):
    lane = lax.broadcasted_iota(jnp.int32, c3.shape, 1)
    ones = ((lane >= _CQ_LANE) & (lane < _CQ_LANE + 3)).astype(F32)
    return (ones - c3.astype(F32)).astype(BF16)


def _qkv_kernel(tiles_per_seq, tm, transposed_v, x_ref, g_ref, wq_ref, wk_ref, wv_ref, wf_ref, bf_ref,
                k_ref, v_ref, lf_ref, qb_ref, kb_ref, vb_ref, c3_ref, kc_ref, carry):
    u = _rms(x_ref[...], g_ref[...]).astype(BF16)
    q = _dot(u, wq_ref[...]) * (FOX_HEAD_DIM ** -0.5 * LOG2E)
    k = _dot(u, wk_ref[...])
    v = _dot(u, wv_ref[...])
    k_ref[...] = k
    v_ref[...] = v
    qb_ref[...] = q.astype(BF16)
    kb_ref[...] = k.astype(BF16)
    if transposed_v:
        vb_ref[0, 0] = jnp.transpose(v).astype(BF16)
    else:
        vb_ref[...] = v.astype(BF16)
    lane = lax.broadcasted_iota(jnp.int32, (tm, LANES), 1)
    logf = jnp.where(lane < FOX_HEADS, _log_sigmoid(_dot(u, wf_ref[...]) + bf_ref[...]), 0.0)
    lf_ref[...] = logf[:, :FOX_HEADS]

    @pl.when(pl.program_id(0) % tiles_per_seq == 0)
    def _():
        carry[...] = jnp.zeros_like(carry)

    c = _tri_cumsum(logf, tm) + carry[0:1, :]
    carry[...] = jnp.broadcast_to(c[tm - 1:tm, :], carry.shape)
    c3 = _c_pieces(c)
    c3_ref[...] = c3
    kc_ref[...] = _key_bias_lanes(c3)


def _qkv(x, g, wq, wk, wv, wf, bf, seq_len, tm, transposed_v):
    t = x.shape[0]
    tiles_per_seq = seq_len // tm
    tile = pl.BlockSpec((tm, D_MODEL), lambda i: (i, 0))
    narrow = pl.BlockSpec((tm, LANES), lambda i: (i, 0))
    const = lambda a: pl.BlockSpec(a.shape, lambda i: (0,) * a.ndim, pipeline_mode=pl.Buffered(1))
    if transposed_v:
        vb_spec = pl.BlockSpec((1, 1, D_MODEL, tm), lambda i: (i // tiles_per_seq, i % tiles_per_seq, 0, 0))
        vb_shape = jax.ShapeDtypeStruct((t // seq_len, tiles_per_seq, D_MODEL, tm), BF16)
    else:
        vb_spec, vb_shape = tile, jax.ShapeDtypeStruct((t, D_MODEL), BF16)
    return pl.pallas_call(
        functools.partial(_qkv_kernel, tiles_per_seq, tm, transposed_v),
        grid=(t // tm,),
        in_specs=[tile, _full((1, D_MODEL)), const(wq), const(wk), const(wv), const(wf), _full((1, LANES))],
        out_specs=[tile, tile, pl.BlockSpec((tm, FOX_HEADS), lambda i: (i, 0)), tile, tile, vb_spec,
                   narrow, narrow],
        out_shape=[jax.ShapeDtypeStruct((t, D_MODEL), F32), jax.ShapeDtypeStruct((t, D_MODEL), F32),
                   jax.ShapeDtypeStruct((t, FOX_HEADS), F32),
                   jax.ShapeDtypeStruct((t, D_MODEL), BF16), jax.ShapeDtypeStruct((t, D_MODEL), BF16),
                   vb_shape, jax.ShapeDtypeStruct((t, LANES), BF16), jax.ShapeDtypeStruct((t, LANES), BF16)],
        scratch_shapes=[pltpu.VMEM((8, LANES), F32)],
        compiler_params=_cparams(1), name="fox_qkv",
    )(x, g[None, :], wq, wk, wv, wf, bf)


def _query_ext(q, c3q, head, lo_half):
    rows = q.shape[0]
    lane = lax.broadcasted_iota(jnp.int32, (rows, LANES), 1)
    own = (lane < FOX_HEAD_DIM) if lo_half else (lane >= FOX_HEAD_DIM)
    qm = jnp.where(own, q, jnp.zeros_like(q))
    pick = lambda j: jnp.sum(jnp.where(lane == head + j * FOX_HEADS, c3q, 0.0), axis=1, keepdims=True)
    onehot = (lane == head) | (lane == head + FOX_HEADS) | (lane == head + 2 * FOX_HEADS)
    qc = jnp.where(lane == _CQ_LANE, pick(0),
                   jnp.where(lane == _CQ_LANE + 1, pick(1),
                             jnp.where(lane == _CQ_LANE + 2, pick(2), jnp.where(onehot, 1.0, 0.0))))
    return jnp.concatenate([qm, qc.astype(BF16)], axis=1)


def _sublane_allreduce(x, op):
    for sh in (4, 2, 1):
        x = op(x, pltpu.roll(x, sh, 0))
    return x


def _fold_rows(x, op):
    parts = [x[i * SUBLANES:(i + 1) * SUBLANES] for i in range(x.shape[0] // SUBLANES)]
    while len(parts) > 1:
        parts = [op(parts[i], parts[i + 1]) for i in range(0, len(parts), 2)]
    return parts[0]


ATTN_PAIRS_PER_STEP = 2
ONES_ROWS = 16


def _attn_prompt_kernel(t, n_items, qi_tab, j_tab, q_ref, c3_ref, k_ref, kc_ref, vt_ref, o_ref,
                        qt, s_buf, cm_buf, p_buf, al_buf, acc):
    n_heads = acc.shape[0]
    first_head = pl.program_id(1) * n_heads
    pair_lanes = lambda a: slice((a // 2) * PAIR_LANES, (a // 2 + 1) * PAIR_LANES)
    big = -NEG_INF
    for a in range(n_heads):
        acc[a] = jnp.zeros(acc.shape[1:], F32)
        p_buf[1, a] = jnp.zeros((t, t), BF16)
        al_buf[1, a] = jnp.ones((SUBLANES, t), F32)
    ones_rows = jnp.ones((ONES_ROWS, t), BF16)

    def load_q(qi):
        start = pl.multiple_of(qi * t, t)
        c3q = c3_ref[pl.ds(start, t), :].astype(F32)
        for a in range(n_heads):
            qe = _query_ext(q_ref[pl.ds(start, t), pair_lanes(a)], c3q, first_head + a, a % 2 == 0)
            qt[a] = jnp.transpose(qe.astype(F32)).astype(BF16)

    def stage_a(w, slot, diagonal):
        start = pl.multiple_of(j_tab[w] * t, t)
        kc = kc_ref[pl.ds(start, t), :]
        width = 2 * LANES
        for a in range(n_heads):
            kext = jnp.concatenate([k_ref[pl.ds(start, t), pair_lanes(a)], kc], axis=1)
            for q0 in range(0, t, width):
                s = _dot(kext, qt[a, :, q0:q0 + width])
                if diagonal:
                    r = lax.broadcasted_iota(jnp.int32, (t, width), 0)
                    c = lax.broadcasted_iota(jnp.int32, (t, width), 1) + q0
                    s = jnp.where(r <= c, s, NEG_INF)
                s_buf[slot, a, :, q0:q0 + width] = s
                cm_buf[slot, a, :, q0:q0 + width] = _sublane_allreduce(_fold_rows(s, jnp.maximum), jnp.maximum)

    def stage_b(w, ms, slot):
        m_floor = jnp.where(j_tab[w] == 0, NEG_INF, big)
        new_ms = []
        for a in range(n_heads):
            m_old = jnp.minimum(ms[a], m_floor)
            m_new = jnp.maximum(m_old, cm_buf[slot, a])
            p_buf[slot, a] = jnp.exp2(s_buf[slot, a] - jnp.tile(m_new, (t // SUBLANES, 1))).astype(BF16)
            al_buf[slot, a] = jnp.exp2(m_old - m_new)
            new_ms.append(m_new)
        return tuple(new_ms)

    def stage_c(w, slot):
        for a in range(n_heads):
            vt = jnp.concatenate([vt_ref[0, j_tab[w], pair_lanes(a), :], ones_rows], axis=0)
            acc[a] = (acc[a] * jnp.tile(al_buf[slot, a], (acc.shape[1] // SUBLANES, 1))
                      + _dot(vt, p_buf[slot, a]))

    def finalize(w):
        outs = []
        for a in range(n_heads):
            full = acc[a]
            denom = jnp.tile(full[PAIR_LANES:PAIR_LANES + SUBLANES], (PAIR_LANES // SUBLANES, 1))
            outs.append(full[:PAIR_LANES] / denom)
        row = lax.broadcasted_iota(jnp.int32, (PAIR_LANES, t), 0)
        rows = pl.ds(pl.multiple_of(qi_tab[w] * t, t), t)
        for a in range(0, n_heads, 2):
            o = jnp.transpose(jnp.where(row < FOX_HEAD_DIM, outs[a], outs[a + 1]))
            o_ref[rows, pair_lanes(a)] = o.astype(o_ref.dtype)

    load_q(0)
    stage_a(0, 0, True)

    def item(w, ms, slot):
        nxt = jnp.minimum(w + 1, n_items - 1)
        prev = jnp.maximum(w - 1, 0)

        @pl.when(j_tab[nxt] == 0)
        def _():
            load_q(qi_tab[nxt])

        def stages(diagonal, ms):
            stage_c(prev, 1 - slot)
            ms = stage_b(w, ms, slot)
            stage_a(nxt, 1 - slot, diagonal)
            return ms

        ms = lax.cond(j_tab[nxt] == qi_tab[nxt], functools.partial(stages, True),
                      functools.partial(stages, False), ms)

        @pl.when((w >= 1) & (j_tab[prev] == qi_tab[prev]))
        def _():
            finalize(prev)

        return ms

    def body(i, ms):
        return item(2 * i + 1, item(2 * i, ms, 0), 1)

    ms = lax.fori_loop(0, n_items // 2, body, tuple(jnp.full((SUBLANES, t), NEG_INF, F32) for _ in range(n_heads)))
    if n_items % 2:
        item(n_items - 1, ms, 0)
    stage_c(n_items - 1, (n_items - 1) % 2)
    finalize(n_items - 1)


def _attn_prompt(qb, kb, kc, vt, c3, n_seq, seq_len, t):
    total = qb.shape[0]
    nt = seq_len // t
    items = [(qi, j) for qi in range(nt) for j in range(qi + 1)]
    qi_tab = jnp.asarray([qi for qi, _ in items], jnp.int32)
    j_tab = jnp.asarray([j for _, j in items], jnp.int32)
    nh = 2 * ATTN_PAIRS_PER_STEP
    width = ATTN_PAIRS_PER_STEP * PAIR_LANES
    once = pl.Buffered(1)
    seq_in = pl.BlockSpec((seq_len, width), lambda b, p, *_: (b, p), pipeline_mode=once)
    seq_bias = pl.BlockSpec((seq_len, LANES), lambda b, p, *_: (b, 0), pipeline_mode=once)
    vt_in = pl.BlockSpec((1, nt, width, t), lambda b, p, *_: (b, 0, p, 0), pipeline_mode=once)
    return pl.pallas_call(
        functools.partial(_attn_prompt_kernel, t, len(items)),
        grid_spec=pltpu.PrefetchScalarGridSpec(
            num_scalar_prefetch=2,
            grid=(n_seq, HEAD_PAIRS // ATTN_PAIRS_PER_STEP),
            in_specs=[seq_in, seq_bias, seq_in, seq_bias, vt_in],
            out_specs=pl.BlockSpec((seq_len, width), lambda b, p, *_: (b, p)),
            scratch_shapes=[pltpu.VMEM((nh, 2 * LANES, t), BF16), pltpu.VMEM((2, nh, t, t), F32),
                            pltpu.VMEM((2, nh, SUBLANES, t), F32),
                            pltpu.VMEM((2, nh, t, t), BF16), pltpu.VMEM((2, nh, SUBLANES, t), F32),
                            pltpu.VMEM((nh, PAIR_LANES + ONES_ROWS, t), F32)]),
        out_shape=jax.ShapeDtypeStruct((total, D_MODEL), BF16),
        compiler_params=_cparams(2), name="fox_attn_prompt",
    )(qi_tab, j_tab, qb, c3, kb, kc, vt)


def _attn_sample_kernel(n_cache_tiles, tk, t_new, q_ref, c3q_ref, kc_ref, vc_ref, lfc_ref,
                        kn_ref, vn_ref, lfn_ref, o_ref, qt, m_scr, acc_scr, carry, s_buf, p_buf, al_buf):
    j = pl.program_id(1)
    rows = 2 * t_new

    @pl.when(j == 0)
    def _():
        c3q = c3q_ref[...].astype(F32)
        for p in range(HEAD_PAIRS):
            q = q_ref[:, p * PAIR_LANES:(p + 1) * PAIR_LANES]
            qe = jnp.concatenate([_query_ext(q, c3q, 2 * p, True), _query_ext(q, c3q, 2 * p + 1, False)], axis=0)
            qt[p] = jnp.transpose(qe.astype(F32)).astype(BF16)
        m_scr[...] = jnp.full(m_scr.shape, NEG_INF, F32)
        acc_scr[...] = jnp.zeros(acc_scr.shape, F32)
        carry[...] = jnp.zeros_like(carry)

    def attend(k_pair, v_pair, n_keys, c3k, masked):
        kc = _key_bias_lanes(c3k)
        for p in range(HEAD_PAIRS):
            s = _dot(jnp.concatenate([k_pair(p), kc], axis=1), qt[p])
            if masked:
                r = lax.broadcasted_iota(jnp.int32, (n_keys, rows), 0)
                c = lax.broadcasted_iota(jnp.int32, (n_keys, rows), 1) & (t_new - 1)
                s = jnp.where(r <= c, s, NEG_INF)
            s_buf[p, :n_keys] = s
        for p in range(HEAD_PAIRS):
            s = s_buf[p, :n_keys]
            m_old = m_scr[p]
            m_new = jnp.maximum(m_old, _sublane_allreduce(_fold_rows(s, jnp.maximum), jnp.maximum))
            p_buf[p, :n_keys] = jnp.exp2(s - jnp.tile(m_new, (n_keys // SUBLANES, 1))).astype(BF16)
            alpha = jnp.exp2(m_old - m_new)
            al_buf[p] = jnp.transpose(jnp.tile(alpha, (rows // SUBLANES, 1)))
            m_scr[p] = m_new
        ones = jnp.ones((n_keys, LANES), BF16)
        for p in range(HEAD_PAIRS):
            pv = lax.dot_general(p_buf[p, :n_keys], jnp.concatenate([v_pair(p), ones], axis=1),
                                 (((0,), (0,)), ((), ())), preferred_element_type=F32)
            acc_scr[p] = acc_scr[p] * jnp.tile(al_buf[p], (1, 2)) + pv

    def cache_pair(ref):
        return lambda p: ref[0, :, p * PAIR_LANES:(p + 1) * PAIR_LANES].astype(BF16)

    def new_pair(ref):
        return lambda p: ref[:, p * PAIR_LANES:(p + 1) * PAIR_LANES]

    @pl.when(j < n_cache_tiles)
    def _():
        c = _tri_cumsum(lfc_ref[0], tk) + carry[0:1, :]
        carry[...] = jnp.broadcast_to(c[tk - 1:tk, :], carry.shape)
        attend(cache_pair(kc_ref), cache_pair(vc_ref), tk, _c_pieces(c), False)

    @pl.when(j == n_cache_tiles)
    def _():
        c = _tri_cumsum(lfn_ref[...], t_new) + carry[0:1, :]
        attend(new_pair(kn_ref), new_pair(vn_ref), t_new, _c_pieces(c), True)
        lane = lax.broadcasted_iota(jnp.int32, (t_new, LANES), 1)
        for p in range(HEAD_PAIRS):
            acc = acc_scr[p]
            oa = acc[:t_new, :LANES] / acc[:t_new, LANES:]
            ob = acc[t_new:, :LANES] / acc[t_new:, LANES:]
            o_ref[:, p * PAIR_LANES:(p + 1) * PAIR_LANES] = jnp.where(lane < FOX_HEAD_DIM, oa, ob).astype(o_ref.dtype)


def _attn_sample(qb, kb, vb, c3, logf, cache_k, cache_v, cache_logf, tk):
    n_seq, past, _ = cache_k.shape
    t_new = qb.shape[0] // n_seq
    nct = past // tk
    pad = lambda a: jnp.pad(a, ((0, 0),) * (a.ndim - 1) + ((0, LANES - FOX_HEADS),))
    new_tile = pl.BlockSpec((t_new, D_MODEL), lambda b, j: (b, 0))
    new_c3 = pl.BlockSpec((t_new, LANES), lambda b, j: (b, 0))
    cache_tile = pl.BlockSpec((1, tk, D_MODEL), lambda b, j: (b, jnp.minimum(j, nct - 1), 0))
    lf_tile = pl.BlockSpec((1, tk, LANES), lambda b, j: (b, jnp.minimum(j, nct - 1), 0))
    return pl.pallas_call(
        functools.partial(_attn_sample_kernel, nct, tk, t_new),
        grid=(n_seq, nct + 1),
        in_specs=[new_tile, new_c3, cache_tile, cache_tile, lf_tile, new_tile, new_tile, new_c3],
        out_specs=new_tile,
        out_shape=jax.ShapeDtypeStruct((n_seq * t_new, D_MODEL), BF16),
        scratch_shapes=[pltpu.VMEM((HEAD_PAIRS, 2 * LANES, 2 * t_new), BF16),
                        pltpu.VMEM((HEAD_PAIRS, SUBLANES, 2 * t_new), F32),
                        pltpu.VMEM((HEAD_PAIRS, 2 * t_new, 2 * LANES), F32),
                        pltpu.VMEM((8, LANES), F32),
                        pltpu.VMEM((HEAD_PAIRS, tk, 2 * t_new), F32),
                        pltpu.VMEM((HEAD_PAIRS, tk, 2 * t_new), BF16),
                        pltpu.VMEM((HEAD_PAIRS, 2 * t_new, 2 * t_new), F32)],
        compiler_params=_cparams(2), name="fox_attn_sample",
    )(qb, c3, cache_k, cache_v, pad(cache_logf), kb, vb, pad(logf))


def _trunk(x, n_seq, seq_len, ssm_h0, fox_cache, conv_hist, w, tm, tq):
    t = n_seq * seq_len
    nb = seq_len // SSM_BLOCK
    nc = SSM_GROUPS // COLUMN_GROUPS
    half = COLUMN_GROUPS * SSM_STATE
    x = x.reshape(t, D_MODEL)

    u0 = _norm(x, w["norm_mix"][0], tm)
    if ssm_h0 is None:
        h0 = None
    else:
        per_c = lambda h: jnp.swapaxes(h.reshape(n_seq, nc, half), 0, 1)
        h0c = jnp.concatenate([per_c(ssm_h0[0]), per_c(ssm_h0[1])], axis=2)
        h0 = jnp.zeros((nc, n_seq, nb, 2 * half), F32).at[:, :, 0, :].set(h0c).reshape(nc, n_seq * nb, 2 * half)
    ys, fin = _s5_core(u0, w["s5"], n_seq, seq_len, h0, fold_sequences=ssm_h0 is not None)
    per_g = lambda f: jnp.swapaxes(f.reshape(nc, n_seq, COLUMN_GROUPS, SSM_STATE), 0, 1).reshape(
        n_seq, SSM_GROUPS, SSM_STATE)
    fre, fim = per_g(fin[:, :, :half]), per_g(fin[:, :, half:])
    x, conv0 = _ffn(x, "glu", (ys, w["norm_mix"][0], w["ssm_d"], w["w_glu"]), conv_hist[0],
                    w["norm_ffn"][0], w["w_up"][0], w["w_gate"][0], w["conv_w"][0], w["conv_b"][0],
                    w["w_down"][0], None, seq_len, tm)

    k, v, logf, qb, kb, vb, c3, kc = _qkv(x, w["norm_mix"][1], w["wq"], w["wk"], w["wv"], w["wf"], w["bf"],
                                          seq_len, tq, transposed_v=fox_cache is None)
    if fox_cache is None:
        o = _attn_prompt(qb, kb, kc, vb, c3, n_seq, seq_len, tq)
    else:
        o = _attn_sample(qb, kb, vb, c3, logf, *fox_cache, tk=min(512, fox_cache[0].shape[1]))
    y, conv1 = _ffn(x, "oproj", (o, w["w_o"]), conv_hist[1],
                    w["norm_ffn"][1], w["w_up"][1], w["w_gate"][1], w["conv_w"][1], w["conv_b"][1],
                    w["w_down"][1], w["norm_final"], seq_len, tm)

    shape4 = (1, n_seq, seq_len, FOX_HEADS, FOX_HEAD_DIM)
    return (y.reshape(n_seq, seq_len, D_MODEL),
            fre[None], fim[None],
            k.reshape(shape4), v.reshape(shape4), logf.reshape(1, n_seq, seq_len, FOX_HEADS),
            jnp.stack([conv0, conv1]))


def kernel(x_prompt, x_sample, state_ssm_re, state_ssm_im, cache_fox_k, cache_fox_v, cache_fox_logf, state_ffn_conv, norm_mix, norm_ffn, norm_final, ssm_a_re, ssm_a_im, ssm_log_step, ssm_b_re, ssm_b_im, ssm_c_re, ssm_c_im, ssm_d, ssm_w_glu, fox_w_qkvf, fox_b_f, fox_w_o, ffn_w_up, ffn_w_gate, ffn_conv_w, ffn_conv_b, ffn_w_down):
    n_p, l_p, _ = x_prompt.shape
    n_s, l_s, _ = x_sample.shape
    past = cache_fox_k.shape[2]
    wqkvf = fox_w_qkvf[0]
    w = {
        "norm_mix": norm_mix, "norm_ffn": norm_ffn, "norm_final": norm_final,
        "s5": _s5_block_diagonal(_s5_params(ssm_a_re[0], ssm_a_im[0], ssm_log_step[0], ssm_b_re[0],
                                            ssm_b_im[0], ssm_c_re[0], ssm_c_im[0])),
        "ssm_d": ssm_d[0], "w_glu": ssm_w_glu[0].astype(BF16),
        "wq": wqkvf[:, :D_MODEL].astype(BF16), "wk": wqkvf[:, D_MODEL:2 * D_MODEL].astype(BF16),
        "wv": wqkvf[:, 2 * D_MODEL:3 * D_MODEL].astype(BF16),
        "wf": jnp.pad(wqkvf[:, 3 * D_MODEL:], ((0, 0), (0, LANES - FOX_HEADS))).astype(BF16),
        "bf": jnp.pad(fox_b_f[0], (0, LANES - FOX_HEADS))[None, :],
        "w_o": fox_w_o[0].astype(BF16),
        "w_up": ffn_w_up.astype(BF16), "w_gate": ffn_w_gate.astype(BF16), "w_down": ffn_w_down.astype(BF16),
        "conv_w": ffn_conv_w, "conv_b": ffn_conv_b,
    }
    zero_hist = jnp.zeros((2, n_p, CONV_WIDTH - 1, D_FF), F32)
    outs_p = _trunk(x_prompt, n_p, l_p, None, None, zero_hist, w, tm=512, tq=512)
    cache = (cache_fox_k[0].reshape(n_s, past, D_MODEL), cache_fox_v[0].reshape(n_s, past, D_MODEL),
             cache_fox_logf[0])
    outs_s = _trunk(x_sample, n_s, l_s, (state_ssm_re[0], state_ssm_im[0]), cache, state_ffn_conv, w,
                    tm=l_s, tq=l_s)
    return (outs_p[0], outs_s[0]) + outs_p[1:] + outs_s[1:]
```

```python
import functools
import math

import jax
import jax.numpy as jnp
from jax import lax
from jax.experimental import pallas as pl
from jax.experimental.pallas import tpu as pltpu

D_MODEL = 1024
SSM_GROUP = 16
SSM_GROUPS = D_MODEL // SSM_GROUP
SSM_STATE = 64
SSM_BLOCK = 16
SSM_BLOCK_LOG2 = 4
BLOCK_LANES = SSM_BLOCK * SSM_GROUP
COLUMN_LANES = 128
COLUMN_GROUPS = COLUMN_LANES // SSM_GROUP
STEP_PAIRS = SSM_BLOCK // 2
FOX_HEADS = 16
FOX_HEAD_DIM = 64
HEAD_PAIRS = FOX_HEADS // 2
PAIR_LANES = 2 * FOX_HEAD_DIM
D_FF = 2816
CONV_WIDTH = 3
FFN_CHUNK_EDGES = (0, 1536, D_FF)
NORM_EPS = 1e-6
NEG_INF = -1e30
LOG2E = 1.4426950408889634
LANES = 128
SUBLANES = 8
_CQ_LANE = 3 * FOX_HEADS
VMEM_LIMIT = 56 * 1024 * 1024

BF16 = jnp.bfloat16
F32 = jnp.float32


def _cparams(n_axes, vmem=VMEM_LIMIT):
    return pltpu.CompilerParams(dimension_semantics=("arbitrary",) * n_axes, vmem_limit_bytes=vmem)


def _full(shape):
    nd = len(shape)
    return pl.BlockSpec(shape, lambda *_: (0,) * nd)


def _rms(x, g):
    ms = jnp.mean(x * x, axis=-1, keepdims=True)
    return x * lax.rsqrt(ms + NORM_EPS) * g


def _gelu(x):
    c = math.sqrt(2.0 / math.pi)
    return 0.5 * x * (1.0 + jnp.tanh(c * (x + 0.044715 * (x * x * x))))


def _sigmoid(x):
    return 1.0 / (1.0 + jnp.exp(-x))


def _log_sigmoid(x):
    return jnp.minimum(x, 0.0) - jnp.log1p(jnp.exp(-jnp.abs(x)))


def _split3(x):
    hi = x.astype(BF16)
    r1 = x - hi.astype(F32)
    mid = r1.astype(BF16)
    lo = (r1 - mid.astype(F32)).astype(BF16)
    return hi, mid, lo


def _dot(a, b):
    return jnp.dot(a, b, preferred_element_type=F32)


def _dot_nt(a, b):
    return lax.dot_general(a, b, (((1,), (1,)), ((), ())), preferred_element_type=F32)


def _dot_nt_f32(a, b):
    a0, a1, a2 = _split3(a)
    b0, b1, b2 = _split3(b)
    return (_dot_nt(a0, b0) + (_dot_nt(a0, b1) + _dot_nt(a1, b0))
            + (_dot_nt(a0, b2) + _dot_nt(a1, b1) + _dot_nt(a2, b0)))


def _tri_cumsum(x, n):
    r = lax.broadcasted_iota(jnp.int32, (n, n), 0)
    c = lax.broadcasted_iota(jnp.int32, (n, n), 1)
    tri = jnp.where(c <= r, 1.0, 0.0).astype(BF16)
    hi, mid, lo = _split3(x)
    return _dot(tri, hi) + _dot(tri, mid) + _dot(tri, lo)


def _s5_param_kernel(are_r, aim_r, ls_ref, btre, btim, ctre, ctim,
                     k_ref, wre_ref, wim_ref, zre_ref, zim_ref, mu_ref):
    dt = jnp.exp(ls_ref[0])
    lre, lim = are_r[0], aim_r[0]
    mag = jnp.exp(lre * dt)
    abr = mag * jnp.cos(lim * dt)
    abi = mag * jnp.sin(lim * dt)
    den = lre * lre + lim * lim
    nr = abr - 1.0
    zre = (nr * lre + abi * lim) / den
    zim = (abi * lre - nr * lim) / den

    def cpow(e):
        m = jnp.exp(lre * dt * e)
        a = lim * dt * e
        return m * jnp.cos(a), m * jnp.sin(a)

    def per_step(e):
        step = lax.broadcasted_iota(jnp.int32, (SSM_BLOCK, SSM_STATE), 0).astype(F32)
        pr, pi = cpow(e(step))
        rep = lambda t: jnp.broadcast_to(t[:, None, :], (SSM_BLOCK, SSM_GROUP, SSM_STATE)).reshape(
            BLOCK_LANES, SSM_STATE)
        return rep(pr), rep(pi)

    bre, bim = btre[0], btim[0]
    bbre = zre * bre - zim * bim
    bbim = zre * bim + zim * bre
    pwr, pwi = per_step(lambda s: (SSM_BLOCK - 1.0) - s)
    wre_ref[0] = (pwr * bbre - pwi * bbim).astype(BF16)
    wim_ref[0] = (pwr * bbim + pwi * bbre).astype(BF16)
    pbr, pbi = per_step(lambda s: -s)
    bxre = pbr * bbre - pbi * bbim
    bxim = pbr * bbim + pbi * bbre

    cre, cim = ctre[0], ctim[0]
    pcr, pci = per_step(lambda t: t)
    cxre = pcr * cre - pci * cim
    cxim = pcr * cim + pci * cre
    pzr, pzi = per_step(lambda t: t + 1.0)
    zre_ref[0] = (pzr * cre - pzi * cim).astype(BF16)
    zim_ref[0] = (-(pzr * cim + pzi * cre)).astype(BF16)

    kfull = _dot_nt_f32(bxre, cxre) - _dot_nt_f32(bxim, cxim)
    rs = lax.broadcasted_iota(jnp.int32, (BLOCK_LANES, BLOCK_LANES), 0) >> SSM_BLOCK_LOG2
    ct = lax.broadcasted_iota(jnp.int32, (BLOCK_LANES, BLOCK_LANES), 1) >> SSM_BLOCK_LOG2
    k_ref[0] = jnp.where(ct >= rs, kfull, 0.0).astype(BF16)

    mur, mui = cpow(float(SSM_BLOCK))
    mu_ref[0, 0:1, :] = mur
    mu_ref[0, 1:2, :] = mui


def _s5_params(a_re, a_im, log_step, b_re, b_im, c_re, c_im):
    g, p = a_re.shape
    bt = jnp.tile(jnp.swapaxes(b_re, 1, 2), (1, SSM_BLOCK, 1)), jnp.tile(jnp.swapaxes(b_im, 1, 2), (1, SSM_BLOCK, 1))
    ct = jnp.tile(c_re, (1, SSM_BLOCK, 1)), jnp.tile(c_im, (1, SSM_BLOCK, 1))
    row = pl.BlockSpec((1, 1, p), lambda i: (i, 0, 0))
    one = pl.BlockSpec((1, 1, 1), lambda i: (i, 0, 0))
    tall = pl.BlockSpec((1, BLOCK_LANES, p), lambda i: (i, 0, 0))
    return pl.pallas_call(
        _s5_param_kernel,
        grid=(g,),
        in_specs=[row, row, one, tall, tall, tall, tall],
        out_specs=[pl.BlockSpec((1, BLOCK_LANES, BLOCK_LANES), lambda i: (i, 0, 0)),
                   tall, tall, tall, tall, pl.BlockSpec((1, 2, p), lambda i: (i, 0, 0))],
        out_shape=[jax.ShapeDtypeStruct((g, BLOCK_LANES, BLOCK_LANES), BF16)]
        + [jax.ShapeDtypeStruct((g, BLOCK_LANES, p), BF16)] * 4
        + [jax.ShapeDtypeStruct((g, 2, p), F32)],
        compiler_params=_cparams(1),
        name="s5_params",
    )(a_re[:, None, :], a_im[:, None, :], log_step[:, None, None], bt[0], bt[1], ct[0], ct[1])


def _norm_kernel(x_ref, g_ref, u_ref):
    u_ref[...] = _rms(x_ref[...], g_ref[...]).astype(u_ref.dtype)


def _norm(x, g, tm):
    t = x.shape[0]
    tile = pl.BlockSpec((tm, D_MODEL), lambda i: (i, 0))
    return pl.pallas_call(
        _norm_kernel, grid=(t // tm,),
        in_specs=[tile, _full((1, D_MODEL))], out_specs=tile,
        out_shape=jax.ShapeDtypeStruct((t, D_MODEL), F32),
        compiler_params=_cparams(1), name="s5_prenorm",
    )(x, g[None, :])


def _s5_core_kernel(nb, n_seq, has_h0, *refs):
    if has_h0:
        u_ref, kk_ref, ww_ref, zz_ref, mu_ref, h0_ref, y_ref, f_ref, s_scr = refs
    else:
        u_ref, kk_ref, ww_ref, zz_ref, mu_ref, y_ref, f_ref, s_scr = refs
    rows = nb * n_seq
    half = COLUMN_GROUPS * SSM_STATE
    xp = [jnp.concatenate([u_ref[pl.ds(2 * p, rows, stride=SSM_BLOCK), :],
                           u_ref[pl.ds(2 * p + 1, rows, stride=SSM_BLOCK), :]], axis=1).astype(BF16)
          for p in range(STEP_PAIRS)]
    s = _dot(xp[0], ww_ref[0, 0])
    for p in range(1, STEP_PAIRS):
        s = s + _dot(xp[p], ww_ref[0, p])
    sre, sim = s[:, :half], s[:, half:]
    mu = mu_ref[0]
    mre, mim = mu[0:1, :], mu[1:2, :]
    blk = lax.broadcasted_iota(jnp.int32, (rows, half), 0) & (nb - 1)
    if has_h0:
        h0 = h0_ref[0]
        h0re, h0im = h0[:, :half], h0[:, half:]
        sre = sre + (mre * h0re - mim * h0im)
        sim = sim + (mre * h0im + mim * h0re)
    d = 1
    while d < nb:
        keep = blk >= d
        shr = jnp.where(keep, pltpu.roll(sre, d, 0), 0.0)
        shi = jnp.where(keep, pltpu.roll(sim, d, 0), 0.0)
        sre, sim = sre + (mre * shr - mim * shi), sim + (mre * shi + mim * shr)
        mre, mim = mre * mre - mim * mim, 2.0 * mre * mim
        d *= 2
    n_chunks = half // LANES
    for i in range(n_chunks):
        s_scr[i] = sre[:, i * LANES:(i + 1) * LANES]
        s_scr[n_chunks + i] = sim[:, i * LANES:(i + 1) * LANES]
    f_ref[0, 0] = jnp.concatenate([s_scr[i, pl.ds(nb - 1, n_seq, stride=nb), :] for i in range(2 * n_chunks)],
                                  axis=1)
    first = blk == 0
    inre = jnp.where(first, 0.0, pltpu.roll(sre, 1, 0))
    inim = jnp.where(first, 0.0, pltpu.roll(sim, 1, 0))
    if has_h0:
        inre = inre + h0re
        inim = inim + h0im
    s_in = jnp.concatenate([inre, inim], axis=1).astype(BF16)
    for p in range(STEP_PAIRS):
        y = _dot_nt(s_in, zz_ref[0, p])
        for q in range(p + 1):
            y = y + _dot(xp[q], kk_ref[0, p - q])
        y_ref[pl.ds(2 * p, rows, stride=SSM_BLOCK), :] = y[:, :COLUMN_LANES]
        y_ref[pl.ds(2 * p + 1, rows, stride=SSM_BLOCK), :] = y[:, COLUMN_LANES:]


def _s5_block_diagonal(params):
    kmat, wre, wim, zre, zim, mu = params
    cg, nc = COLUMN_GROUPS, SSM_GROUPS // COLUMN_GROUPS
    rows = 2 * COLUMN_LANES
    grp = lambda n, width: (jnp.arange(n) // width) % cg

    def stack_rows(m, pairs_on_rows):
        minor = m.shape[-1]
        if pairs_on_rows:
            m6 = m.reshape(nc, cg, STEP_PAIRS, 2, SSM_GROUP, minor)
            return jnp.transpose(m6, (0, 2, 3, 1, 4, 5)).reshape(nc, STEP_PAIRS, rows, minor)
        m6 = m.reshape(nc, cg, 2, SSM_GROUP, STEP_PAIRS, minor // STEP_PAIRS)
        return jnp.transpose(m6, (0, 4, 2, 1, 3, 5)).reshape(nc, STEP_PAIRS, rows, minor // STEP_PAIRS)

    def spread(stacked, width):
        n_in = stacked.shape[-1]
        n_out = n_in * cg
        src = (jnp.arange(n_out) // (width * cg)) * width + jnp.arange(n_out) % width
        place = (jnp.arange(n_in)[:, None] == src[None, :]).astype(BF16)
        own = (grp(rows, SSM_GROUP)[:, None] == grp(n_out, width)[None, :]).astype(BF16)
        return jnp.einsum('cprk,kn->cprn', stacked, place, preferred_element_type=BF16) * own

    kk = spread(stack_rows(kmat[:, :2 * SSM_GROUP, :], False), SSM_GROUP)
    ww = spread(stack_rows(jnp.concatenate([wre, wim], axis=2), True), SSM_STATE)
    zz_t = spread(stack_rows(jnp.concatenate([zre, zim], axis=2), True), SSM_STATE)
    mu_c = jnp.transpose(mu.reshape(nc, cg, 2, SSM_STATE), (0, 2, 1, 3)).reshape(nc, 2, cg * SSM_STATE)
    return kk, ww, zz_t, mu_c


def _s5_core(u, bd, n_seq, seq_len, h0, fold_sequences):
    kk, ww, zz, mu_c = bd
    nc = kk.shape[0]
    nb = seq_len // SSM_BLOCK
    seqs = n_seq if fold_sequences else 1
    n_steps = n_seq // seqs
    rows_tok = seqs * seq_len
    rows = seqs * nb
    state = 2 * COLUMN_GROUPS * SSM_STATE
    col = pl.BlockSpec((rows_tok, COLUMN_LANES), lambda c, i: (i, c))
    per_c = lambda a: pl.BlockSpec((1,) + a.shape[1:], lambda c, i: (c,) + (0,) * (a.ndim - 1))
    in_specs = [col, per_c(kk), per_c(ww), per_c(zz), per_c(mu_c)]
    args = [u, kk, ww, zz, mu_c]
    if h0 is not None:
        in_specs.append(pl.BlockSpec((1, rows, state), lambda c, i: (c, i, 0)))
        args.append(h0)
    y, fin = pl.pallas_call(
        functools.partial(_s5_core_kernel, nb, seqs, h0 is not None),
        grid=(nc, n_steps),
        in_specs=in_specs,
        out_specs=[col, pl.BlockSpec((1, 1, seqs, state), lambda c, i: (c, i, 0, 0))],
        out_shape=[jax.ShapeDtypeStruct(u.shape, F32), jax.ShapeDtypeStruct((nc, n_steps, seqs, state), F32)],
        scratch_shapes=[pltpu.VMEM((state // LANES, rows, LANES), F32)],
        compiler_params=_cparams(2), name="s5_core",
    )(*args)
    return y, fin.reshape(nc, n_seq, state)


def _ffn_kernel(pre, final_norm, tiles_per_seq, tm, *refs):
    refs = list(refs)
    x_ref = refs.pop(0)
    if pre == "glu":
        ys_ref, gmix_ref, dskip_ref, wglu_ref = refs[:4]
        refs = refs[4:]
    else:
        o_ref, wo_ref = refs[:2]
        refs = refs[2:]
    (hist_ref, gffn_ref, wup_ref, wgate_ref, convw_ref, convb_ref, wdown_ref) = refs[:7]
    refs = refs[7:]
    if final_norm:
        gfin_ref = refs.pop(0)
    out_ref, cstate_ref, carry = refs

    x = x_ref[...]
    if pre == "glu":
        u = _rms(x, gmix_ref[...])
        g = _gelu(ys_ref[...] + dskip_ref[...] * u).astype(BF16)
        z = _dot(g, wglu_ref[...])
        x = x + z[:, :D_MODEL] * _sigmoid(z[:, D_MODEL:])
    else:
        x = x + _dot(o_ref[...], wo_ref[...])

    h = _rms(x, gffn_ref[...]).astype(BF16)

    @pl.when(pl.program_id(0) % tiles_per_seq == 0)
    def _():
        carry[...] = hist_ref[0]

    for lo, hi in zip(FFN_CHUNK_EDGES[:-1], FFN_CHUNK_EDGES[1:]):
        cs = slice(lo, hi)
        row = lax.broadcasted_iota(jnp.int32, (tm, hi - lo), 0)
        a = _dot(h, wup_ref[:, cs])
        prev = carry[:, cs]
        a1 = jnp.where(row == 0, prev[1:2, :], pltpu.roll(a, 1, 0))
        a2 = jnp.where(row == 0, prev[0:1, :], jnp.where(row == 1, prev[1:2, :], pltpu.roll(a, 2, 0)))
        conv = a2 * convw_ref[0:1, cs] + a1 * convw_ref[1:2, cs] + a * convw_ref[2:3, cs] + convb_ref[:, cs]
        last2 = a[tm - (CONV_WIDTH - 1):, :]
        carry[:, cs] = last2
        cstate_ref[0, :, cs] = last2
        hh = (_gelu(conv) * _dot(h, wgate_ref[:, cs])).astype(BF16)
        x = x + _dot(hh, wdown_ref[cs, :])
    if final_norm:
        x = _rms(x, gfin_ref[...])
    out_ref[...] = x


def _ffn(x, pre, pre_args, hist, g_ffn, w_up, w_gate, conv_w, conv_b, w_down, g_final, seq_len, tm):
    t = x.shape[0]
    n_seq = t // seq_len
    tiles_per_seq = seq_len // tm
    tile = pl.BlockSpec((tm, D_MODEL), lambda i: (i, 0))
    vec = _full((1, D_MODEL))
    const = lambda a: pl.BlockSpec(a.shape, lambda i: (0,) * a.ndim, pipeline_mode=pl.Buffered(1))
    per_seq = pl.BlockSpec((1, CONV_WIDTH - 1, D_FF), lambda i: (i // tiles_per_seq, 0, 0))
    args, in_specs = [x], [tile]
    if pre == "glu":
        ys, g_mix, d_skip, w_glu = pre_args
        args += [ys, g_mix[None, :], d_skip[None, :], w_glu]
        in_specs += [tile, vec, vec, const(w_glu)]
    else:
        o, w_o = pre_args
        args += [o, w_o]
        in_specs += [tile, const(w_o)]
    args += [hist, g_ffn[None, :], w_up, w_gate, conv_w, conv_b[None, :], w_down]
    in_specs += [per_seq, vec, const(w_up), const(w_gate), _full((CONV_WIDTH, D_FF)), _full((1, D_FF)),
                 const(w_down)]
    if g_final is not None:
        args.append(g_final[None, :])
        in_specs.append(vec)
    return pl.pallas_call(
        functools.partial(_ffn_kernel, pre, g_final is not None, tiles_per_seq, tm),
        grid=(t // tm,),
        in_specs=in_specs,
        out_specs=[tile, per_seq],
        out_shape=[jax.ShapeDtypeStruct((t, D_MODEL), F32),
                   jax.ShapeDtypeStruct((n_seq, CONV_WIDTH - 1, D_FF), F32)],
        scratch_shapes=[pltpu.VMEM((CONV_WIDTH - 1, D_FF), F32)],
        compiler_params=_cparams(1), name="ffn_" + pre,
    )(*args)


def _c_pieces(c):
    hi, mid, lo = _split3(c * LOG2E)
    return (hi.astype(F32) + pltpu.roll(mid.astype(F32), FOX_HEADS, 1)
            + pltpu.roll(lo.astype(F32), 2 * FOX_HEADS, 1)).astype(BF16)


def _key_bias_lanes(c3):
    lane = lax.broadcasted_iota(jnp.int32, c3.shape, 1)
    ones = ((lane >= _CQ_LANE) & (lane < _CQ_LANE + 3)).astype(F32)
    return (ones - c3.astype(F32)).astype(BF16)


def _qkv_kernel(tiles_per_seq, tm, transposed_v, x_ref, g_ref, wq_ref, wk_ref, wv_ref, wf_ref, bf_ref,
                k_ref, v_ref, lf_ref, qb_ref, kb_ref, vb_ref, c3_ref, kc_ref, carry):
    u = _rms(x_ref[...], g_ref[...]).astype(BF16)
    q = _dot(u, wq_ref[...]) * (FOX_HEAD_DIM ** -0.5 * LOG2E)
    k = _dot(u, wk_ref[...])
    v = _dot(u, wv_ref[...])
    k_ref[...] = k
    v_ref[...] = v
    qb_ref[...] = q.astype(BF16)
    kb_ref[...] = k.astype(BF16)
    if transposed_v:
        vb_ref[0, 0] = jnp.transpose(v).astype(BF16)
    else:
        vb_ref[...] = v.astype(BF16)
    lane = lax.broadcasted_iota(jnp.int32, (tm, LANES), 1)
    logf = jnp.where(lane < FOX_HEADS, _log_sigmoid(_dot(u, wf_ref[...]) + bf_ref[...]), 0.0)
    lf_ref[...] = logf[:, :FOX_HEADS]

    @pl.when(pl.program_id(0) % tiles_per_seq == 0)
    def _():
        carry[...] = jnp.zeros_like(carry)

    c = _tri_cumsum(logf, tm) + carry[0:1, :]
    carry[...] = jnp.broadcast_to(c[tm - 1:tm, :], carry.shape)
    c3 = _c_pieces(c)
    c3_ref[...] = c3
    kc_ref[...] = _key_bias_lanes(c3)


def _qkv(x, g, wq, wk, wv, wf, bf, seq_len, tm, transposed_v):
    t = x.shape[0]
    tiles_per_seq = seq_len // tm
    tile = pl.BlockSpec((tm, D_MODEL), lambda i: (i, 0))
    narrow = pl.BlockSpec((tm, LANES), lambda i: (i, 0))
    const = lambda a: pl.BlockSpec(a.shape, lambda i: (0,) * a.ndim, pipeline_mode=pl.Buffered(1))
    if transposed_v:
        vb_spec = pl.BlockSpec((1, 1, D_MODEL, tm), lambda i: (i // tiles_per_seq, i % tiles_per_seq, 0, 0))
        vb_shape = jax.ShapeDtypeStruct((t // seq_len, tiles_per_seq, D_MODEL, tm), BF16)
    else:
        vb_spec, vb_shape = tile, jax.ShapeDtypeStruct((t, D_MODEL), BF16)
    return pl.pallas_call(
        functools.partial(_qkv_kernel, tiles_per_seq, tm, transposed_v),
        grid=(t // tm,),
        in_specs=[tile, _full((1, D_MODEL)), const(wq), const(wk), const(wv), const(wf), _full((1, LANES))],
        out_specs=[tile, tile, pl.BlockSpec((tm, FOX_HEADS), lambda i: (i, 0)), tile, tile, vb_spec,
                   narrow, narrow],
        out_shape=[jax.ShapeDtypeStruct((t, D_MODEL), F32), jax.ShapeDtypeStruct((t, D_MODEL), F32),
                   jax.ShapeDtypeStruct((t, FOX_HEADS), F32),
                   jax.ShapeDtypeStruct((t, D_MODEL), BF16), jax.ShapeDtypeStruct((t, D_MODEL), BF16),
                   vb_shape, jax.ShapeDtypeStruct((t, LANES), BF16), jax.ShapeDtypeStruct((t, LANES), BF16)],
        scratch_shapes=[pltpu.VMEM((8, LANES), F32)],
        compiler_params=_cparams(1), name="fox_qkv",
    )(x, g[None, :], wq, wk, wv, wf, bf)


def _query_ext(q, c3q, head, lo_half):
    rows = q.shape[0]
    lane = lax.broadcasted_iota(jnp.int32, (rows, LANES), 1)
    own = (lane < FOX_HEAD_DIM) if lo_half else (lane >= FOX_HEAD_DIM)
    qm = jnp.where(own, q, jnp.zeros_like(q))
    pick = lambda j: jnp.sum(jnp.where(lane == head + j * FOX_HEADS, c3q, 0.0), axis=1, keepdims=True)
    onehot = (lane == head) | (lane == head + FOX_HEADS) | (lane == head + 2 * FOX_HEADS)
    qc = jnp.where(lane == _CQ_LANE, pick(0),
                   jnp.where(lane == _CQ_LANE + 1, pick(1),
                             jnp.where(lane == _CQ_LANE + 2, pick(2), jnp.where(onehot, 1.0, 0.0))))
    return jnp.concatenate([qm, qc.astype(BF16)], axis=1)


def _sublane_allreduce(x, op):
    for sh in (4, 2, 1):
        x = op(x, pltpu.roll(x, sh, 0))
    return x


def _fold_rows(x, op):
    parts = [x[i * SUBLANES:(i + 1) * SUBLANES] for i in range(x.shape[0] // SUBLANES)]
    while len(parts) > 1:
        parts = [op(parts[i], parts[i + 1]) for i in range(0, len(parts), 2)]
    return parts[0]


KEY_CHUNK = 256
ATTN_PAIRS_PER_STEP = 2
ONES_ROWS = 16


def _attn_prompt_kernel(t, n_items, qi_tab, j_tab, q_ref, c3_ref, k_ref, kc_ref, vt_ref, o_ref,
                        qt, e_buf, cm_buf, acc):
    n_heads = acc.shape[0]
    first_head = pl.program_id(1) * n_heads
    pair_lanes = lambda a: slice((a // 2) * PAIR_LANES, (a // 2 + 1) * PAIR_LANES)
    big = -NEG_INF
    ck = KEY_CHUNK
    n_ck = t // ck
    for a in range(n_heads):
        acc[a] = jnp.zeros(acc.shape[1:], F32)
    ones_rows = jnp.ones((ONES_ROWS, ck), BF16)

    def load_q(qi):
        start = pl.multiple_of(qi * t, t)
        c3q = c3_ref[pl.ds(start, t), :].astype(F32)
        for a in range(n_heads):
            qe = _query_ext(q_ref[pl.ds(start, t), pair_lanes(a)], c3q, first_head + a, a % 2 == 0)
            qt[a] = jnp.transpose(qe.astype(F32)).astype(BF16)

    def stage_a(w, slot, diagonal):
        start = pl.multiple_of(j_tab[w] * t, t)
        width = 2 * LANES
        hk = ck
        for a in range(n_heads):
            for i in range(t // hk):
                rows = pl.ds(start + i * hk, hk)
                kext = jnp.concatenate([k_ref[rows, pair_lanes(a)], kc_ref[rows, :]], axis=1)
                for q0 in range(0, t, width):
                    if diagonal and i * hk > q0 + width - 1:
                        e_buf[slot, a, i * hk:(i + 1) * hk, q0:q0 + width] = jnp.zeros((hk, width), BF16)
                        cm_buf[slot, a, i, :, q0:q0 + width] = jnp.full((SUBLANES, width), NEG_INF, F32)
                        continue
                    s = _dot(kext, qt[a, :, q0:q0 + width])
                    if diagonal and (i + 1) * hk - 1 > q0:
                        r = lax.broadcasted_iota(jnp.int32, (hk, width), 0) + i * hk
                        c = lax.broadcasted_iota(jnp.int32, (hk, width), 1) + q0
                        s = jnp.where(r <= c, s, NEG_INF)
                    cm = _sublane_allreduce(_fold_rows(s, jnp.maximum), jnp.maximum)
                    e_buf[slot, a, i * hk:(i + 1) * hk, q0:q0 + width] = jnp.exp2(
                        s - jnp.tile(cm, (hk // SUBLANES, 1))).astype(BF16)
                    cm_buf[slot, a, i, :, q0:q0 + width] = cm

    def stage_c(w, ms, slot):
        m_floor = jnp.where(j_tab[w] == 0, NEG_INF, big)
        new_ms = []
        for a in range(n_heads):
            m_old = jnp.minimum(ms[a], m_floor)
            cms = [cm_buf[slot, a, i] for i in range(n_ck)]
            m_new = functools.reduce(jnp.maximum, cms, m_old)
            n_rep = acc.shape[1] // SUBLANES
            total = acc[a] * jnp.tile(jnp.exp2(m_old - m_new), (n_rep, 1))
            for i in range(n_ck):
                vt = jnp.concatenate([vt_ref[0, j_tab[w], pair_lanes(a), i * ck:(i + 1) * ck], ones_rows], axis=0)
                pv = _dot(vt, e_buf[slot, a, i * ck:(i + 1) * ck, :])
                total = total + pv * jnp.tile(jnp.exp2(cms[i] - m_new), (n_rep, 1))
            acc[a] = total
            new_ms.append(m_new)
        return tuple(new_ms)

    def finalize(w):
        outs = []
        for a in range(n_heads):
            full = acc[a]
            denom = jnp.tile(full[PAIR_LANES:PAIR_LANES + SUBLANES], (PAIR_LANES // SUBLANES, 1))
            outs.append(full[:PAIR_LANES] / denom)
        row = lax.broadcasted_iota(jnp.int32, (PAIR_LANES, t), 0)
        rows = pl.ds(pl.multiple_of(qi_tab[w] * t, t), t)
        for a in range(0, n_heads, 2):
            o = jnp.transpose(jnp.where(row < FOX_HEAD_DIM, outs[a], outs[a + 1]))
            o_ref[rows, pair_lanes(a)] = o.astype(o_ref.dtype)

    load_q(0)
    stage_a(0, 0, True)

    def item(w, ms, slot):
        nxt = jnp.minimum(w + 1, n_items - 1)

        @pl.when(j_tab[nxt] == 0)
        def _():
            load_q(qi_tab[nxt])

        def stages(diagonal, ms):
            stage_a(nxt, 1 - slot, diagonal)
            ms = stage_c(w, ms, slot)
            return ms

        ms = lax.cond(j_tab[nxt] == qi_tab[nxt], functools.partial(stages, True),
                      functools.partial(stages, False), ms)

        @pl.when(j_tab[w] == qi_tab[w])
        def _():
            finalize(w)

        return ms

    def body(i, ms):
        return item(2 * i + 1, item(2 * i, ms, 0), 1)

    ms = lax.fori_loop(0, n_items // 2, body, tuple(jnp.full((SUBLANES, t), NEG_INF, F32) for _ in range(n_heads)))
    if n_items % 2:
        item(n_items - 1, ms, 0)


def _attn_prompt(qb, kb, kc, vt, c3, n_seq, seq_len, t):
    total = qb.shape[0]
    nt = seq_len // t
    items = [(qi, j) for qi in range(nt) for j in range(qi + 1)]
    qi_tab = jnp.asarray([qi for qi, _ in items], jnp.int32)
    j_tab = jnp.asarray([j for _, j in items], jnp.int32)
    nh = 2 * ATTN_PAIRS_PER_STEP
    width = ATTN_PAIRS_PER_STEP * PAIR_LANES
    once = pl.Buffered(1)
    seq_in = pl.BlockSpec((seq_len, width), lambda b, p, *_: (b, p), pipeline_mode=once)
    seq_bias = pl.BlockSpec((seq_len, LANES), lambda b, p, *_: (b, 0), pipeline_mode=once)
    vt_in = pl.BlockSpec((1, nt, width, t), lambda b, p, *_: (b, 0, p, 0), pipeline_mode=once)
    return pl.pallas_call(
        functools.partial(_attn_prompt_kernel, t, len(items)),
        grid_spec=pltpu.PrefetchScalarGridSpec(
            num_scalar_prefetch=2,
            grid=(n_seq, HEAD_PAIRS // ATTN_PAIRS_PER_STEP),
            in_specs=[seq_in, seq_bias, seq_in, seq_bias, vt_in],
            out_specs=pl.BlockSpec((seq_len, width), lambda b, p, *_: (b, p)),
            scratch_shapes=[pltpu.VMEM((nh, 2 * LANES, t), BF16), pltpu.VMEM((2, nh, t, t), BF16),
                            pltpu.VMEM((2, nh, t // KEY_CHUNK, SUBLANES, t), F32),
                            pltpu.VMEM((nh, PAIR_LANES + ONES_ROWS, t), F32)]),
        out_shape=jax.ShapeDtypeStruct((total, D_MODEL), BF16),
        compiler_params=_cparams(2), name="fox_attn_prompt",
    )(qi_tab, j_tab, qb, c3, kb, kc, vt)


def _attn_sample_kernel(n_cache_tiles, tk, t_new, q_ref, c3q_ref, kc_ref, vc_ref, lfc_ref,
                        kn_ref, vn_ref, lfn_ref, o_ref, qt, m_scr, acc_scr, carry, s_buf, p_buf, al_buf):
    j = pl.program_id(1)
    rows = 2 * t_new

    @pl.when(j == 0)
    def _():
        c3q = c3q_ref[...].astype(F32)
        for p in range(HEAD_PAIRS):
            q = q_ref[:, p * PAIR_LANES:(p + 1) * PAIR_LANES]
            qe = jnp.concatenate([_query_ext(q, c3q, 2 * p, True), _query_ext(q, c3q, 2 * p + 1, False)], axis=0)
            qt[p] = jnp.transpose(qe.astype(F32)).astype(BF16)
        m_scr[...] = jnp.full(m_scr.shape, NEG_INF, F32)
        acc_scr[...] = jnp.zeros(acc_scr.shape, F32)
        carry[...] = jnp.zeros_like(carry)

    def attend(k_pair, v_pair, n_keys, c3k, masked):
        kc = _key_bias_lanes(c3k)
        for p in range(HEAD_PAIRS):
            s = _dot(jnp.concatenate([k_pair(p), kc], axis=1), qt[p])
            if masked:
                r = lax.broadcasted_iota(jnp.int32, (n_keys, rows), 0)
                c = lax.broadcasted_iota(jnp.int32, (n_keys, rows), 1) & (t_new - 1)
                s = jnp.where(r <= c, s, NEG_INF)
            s_buf[p, :n_keys] = s
        for p in range(HEAD_PAIRS):
            s = s_buf[p, :n_keys]
            m_old = m_scr[p]
            m_new = jnp.maximum(m_old, _sublane_allreduce(_fold_rows(s, jnp.maximum), jnp.maximum))
            p_buf[p, :n_keys] = jnp.exp2(s - jnp.tile(m_new, (n_keys // SUBLANES, 1))).astype(BF16)
            alpha = jnp.exp2(m_old - m_new)
            al_buf[p] = jnp.transpose(jnp.tile(alpha, (rows // SUBLANES, 1)))
            m_scr[p] = m_new
        ones = jnp.ones((n_keys, LANES), BF16)
        for p in range(HEAD_PAIRS):
            pv = lax.dot_general(p_buf[p, :n_keys], jnp.concatenate([v_pair(p), ones], axis=1),
                                 (((0,), (0,)), ((), ())), preferred_element_type=F32)
            acc_scr[p] = acc_scr[p] * jnp.tile(al_buf[p], (1, 2)) + pv

    def cache_pair(ref):
        return lambda p: ref[0, :, p * PAIR_LANES:(p + 1) * PAIR_LANES].astype(BF16)

    def new_pair(ref):
        return lambda p: ref[:, p * PAIR_LANES:(p + 1) * PAIR_LANES]

    @pl.when(j < n_cache_tiles)
    def _():
        c = _tri_cumsum(lfc_ref[0], tk) + carry[0:1, :]
        carry[...] = jnp.broadcast_to(c[tk - 1:tk, :], carry.shape)
        attend(cache_pair(kc_ref), cache_pair(vc_ref), tk, _c_pieces(c), False)

    @pl.when(j == n_cache_tiles)
    def _():
        c = _tri_cumsum(lfn_ref[...], t_new) + carry[0:1, :]
        attend(new_pair(kn_ref), new_pair(vn_ref), t_new, _c_pieces(c), True)
        lane = lax.broadcasted_iota(jnp.int32, (t_new, LANES), 1)
        for p in range(HEAD_PAIRS):
            acc = acc_scr[p]
            oa = acc[:t_new, :LANES] / acc[:t_new, LANES:]
            ob = acc[t_new:, :LANES] / acc[t_new:, LANES:]
            o_ref[:, p * PAIR_LANES:(p + 1) * PAIR_LANES] = jnp.where(lane < FOX_HEAD_DIM, oa, ob).astype(o_ref.dtype)


def _attn_sample(qb, kb, vb, c3, logf, cache_k, cache_v, cache_logf, tk):
    n_seq, past, _ = cache_k.shape
    t_new = qb.shape[0] // n_seq
    nct = past // tk
    pad = lambda a: jnp.pad(a, ((0, 0),) * (a.ndim - 1) + ((0, LANES - FOX_HEADS),))
    new_tile = pl.BlockSpec((t_new, D_MODEL), lambda b, j: (b, 0))
    new_c3 = pl.BlockSpec((t_new, LANES), lambda b, j: (b, 0))
    cache_tile = pl.BlockSpec((1, tk, D_MODEL), lambda b, j: (b, jnp.minimum(j, nct - 1), 0))
    lf_tile = pl.BlockSpec((1, tk, LANES), lambda b, j: (b, jnp.minimum(j, nct - 1), 0))
    return pl.pallas_call(
        functools.partial(_attn_sample_kernel, nct, tk, t_new),
        grid=(n_seq, nct + 1),
        in_specs=[new_tile, new_c3, cache_tile, cache_tile, lf_tile, new_tile, new_tile, new_c3],
        out_specs=new_tile,
        out_shape=jax.ShapeDtypeStruct((n_seq * t_new, D_MODEL), BF16),
        scratch_shapes=[pltpu.VMEM((HEAD_PAIRS, 2 * LANES, 2 * t_new), BF16),
                        pltpu.VMEM((HEAD_PAIRS, SUBLANES, 2 * t_new), F32),
                        pltpu.VMEM((HEAD_PAIRS, 2 * t_new, 2 * LANES), F32),
                        pltpu.VMEM((8, LANES), F32),
                        pltpu.VMEM((HEAD_PAIRS, tk, 2 * t_new), F32),
                        pltpu.VMEM((HEAD_PAIRS, tk, 2 * t_new), BF16),
                        pltpu.VMEM((HEAD_PAIRS, 2 * t_new, 2 * t_new), F32)],
        compiler_params=_cparams(2), name="fox_attn_sample",
    )(qb, c3, cache_k, cache_v, pad(cache_logf), kb, vb, pad(logf))


def _trunk(x, n_seq, seq_len, ssm_h0, fox_cache, conv_hist, w, tm, tq):
    t = n_seq * seq_len
    nb = seq_len // SSM_BLOCK
    nc = SSM_GROUPS // COLUMN_GROUPS
    half = COLUMN_GROUPS * SSM_STATE
    x = x.reshape(t, D_MODEL)

    u0 = _norm(x, w["norm_mix"][0], tm)
    if ssm_h0 is None:
        h0 = None
    else:
        per_c = lambda h: jnp.swapaxes(h.reshape(n_seq, nc, half), 0, 1)
        h0c = jnp.concatenate([per_c(ssm_h0[0]), per_c(ssm_h0[1])], axis=2)
        h0 = jnp.zeros((nc, n_seq, nb, 2 * half), F32).at[:, :, 0, :].set(h0c).reshape(nc, n_seq * nb, 2 * half)
    ys, fin = _s5_core(u0, w["s5"], n_seq, seq_len, h0, fold_sequences=ssm_h0 is not None)
    per_g = lambda f: jnp.swapaxes(f.reshape(nc, n_seq, COLUMN_GROUPS, SSM_STATE), 0, 1).reshape(
        n_seq, SSM_GROUPS, SSM_STATE)
    fre, fim = per_g(fin[:, :, :half]), per_g(fin[:, :, half:])
    x, conv0 = _ffn(x, "glu", (ys, w["norm_mix"][0], w["ssm_d"], w["w_glu"]), conv_hist[0],
                    w["norm_ffn"][0], w["w_up"][0], w["w_gate"][0], w["conv_w"][0], w["conv_b"][0],
                    w["w_down"][0], None, seq_len, tm)

    k, v, logf, qb, kb, vb, c3, kc = _qkv(x, w["norm_mix"][1], w["wq"], w["wk"], w["wv"], w["wf"], w["bf"],
                                          seq_len, tq, transposed_v=fox_cache is None)
    if fox_cache is None:
        o = _attn_prompt(qb, kb, kc, vb, c3, n_seq, seq_len, tq)
    else:
        o = _attn_sample(qb, kb, vb, c3, logf, *fox_cache, tk=min(512, fox_cache[0].shape[1]))
    y, conv1 = _ffn(x, "oproj", (o, w["w_o"]), conv_hist[1],
                    w["norm_ffn"][1], w["w_up"][1], w["w_gate"][1], w["conv_w"][1], w["conv_b"][1],
                    w["w_down"][1], w["norm_final"], seq_len, tm)

    shape4 = (1, n_seq, seq_len, FOX_HEADS, FOX_HEAD_DIM)
    return (y.reshape(n_seq, seq_len, D_MODEL),
            fre[None], fim[None],
            k.reshape(shape4), v.reshape(shape4), logf.reshape(1, n_seq, seq_len, FOX_HEADS),
            jnp.stack([conv0, conv1]))


def kernel(x_prompt, x_sample, state_ssm_re, state_ssm_im, cache_fox_k, cache_fox_v, cache_fox_logf, state_ffn_conv, norm_mix, norm_ffn, norm_final, ssm_a_re, ssm_a_im, ssm_log_step, ssm_b_re, ssm_b_im, ssm_c_re, ssm_c_im, ssm_d, ssm_w_glu, fox_w_qkvf, fox_b_f, fox_w_o, ffn_w_up, ffn_w_gate, ffn_conv_w, ffn_conv_b, ffn_w_down):
    n_p, l_p, _ = x_prompt.shape
    n_s, l_s, _ = x_sample.shape
    past = cache_fox_k.shape[2]
    wqkvf = fox_w_qkvf[0]
    w = {
        "norm_mix": norm_mix, "norm_ffn": norm_ffn, "norm_final": norm_final,
        "s5": _s5_block_diagonal(_s5_params(ssm_a_re[0], ssm_a_im[0], ssm_log_step[0], ssm_b_re[0],
                                            ssm_b_im[0], ssm_c_re[0], ssm_c_im[0])),
        "ssm_d": ssm_d[0], "w_glu": ssm_w_glu[0].astype(BF16),
        "wq": wqkvf[:, :D_MODEL].astype(BF16), "wk": wqkvf[:, D_MODEL:2 * D_MODEL].astype(BF16),
        "wv": wqkvf[:, 2 * D_MODEL:3 * D_MODEL].astype(BF16),
        "wf": jnp.pad(wqkvf[:, 3 * D_MODEL:], ((0, 0), (0, LANES - FOX_HEADS))).astype(BF16),
        "bf": jnp.pad(fox_b_f[0], (0, LANES - FOX_HEADS))[None, :],
        "w_o": fox_w_o[0].astype(BF16),
        "w_up": ffn_w_up.astype(BF16), "w_gate": ffn_w_gate.astype(BF16), "w_down": ffn_w_down.astype(BF16),
        "conv_w": ffn_conv_w, "conv_b": ffn_conv_b,
    }
    zero_hist = jnp.zeros((2, n_p, CONV_WIDTH - 1, D_FF), F32)
    outs_p = _trunk(x_prompt, n_p, l_p, None, None, zero_hist, w, tm=512, tq=512)
    cache = (cache_fox_k[0].reshape(n_s, past, D_MODEL), cache_fox_v[0].reshape(n_s, past, D_MODEL),
             cache_fox_logf[0])
    outs_s = _trunk(x_sample, n_s, l_s, (state_ssm_re[0], state_ssm_im[0]), cache, state_ffn_conv, w,
                    tm=l_s, tq=l_s)
    return (outs_p[0], outs_s[0]) + outs_p[1:] + outs_s[1:]
```

```python
import functools
import math

import jax
import jax.numpy as jnp
from jax import lax
from jax.experimental import pallas as pl
from jax.experimental.pallas import tpu as pltpu

D_MODEL = 1024
SSM_GROUP = 16
SSM_GROUPS = D_MODEL // SSM_GROUP
SSM_STATE = 64
SSM_BLOCK = 16
SSM_BLOCK_LOG2 = 4
BLOCK_LANES = SSM_BLOCK * SSM_GROUP
COLUMN_LANES = 128
COLUMN_GROUPS = COLUMN_LANES // SSM_GROUP
STEP_PAIRS = SSM_BLOCK // 2
FOX_HEADS = 16
FOX_HEAD_DIM = 64
HEAD_PAIRS = FOX_HEADS // 2
PAIR_LANES = 2 * FOX_HEAD_DIM
D_FF = 2816
CONV_WIDTH = 3
FFN_CHUNK_EDGES = (0, 1536, D_FF)
NORM_EPS = 1e-6
NEG_INF = -1e30
LOG2E = 1.4426950408889634
LANES = 128
SUBLANES = 8
_CQ_LANE = 3 * FOX_HEADS
VMEM_LIMIT = 56 * 1024 * 1024

BF16 = jnp.bfloat16
F32 = jnp.float32


def _cparams(n_axes, vmem=VMEM_LIMIT):
    return pltpu.CompilerParams(dimension_semantics=("arbitrary",) * n_axes, vmem_limit_bytes=vmem)


def _full(shape):
    nd = len(shape)
    return pl.BlockSpec(shape, lambda *_: (0,) * nd)


def _rms(x, g):
    ms = jnp.mean(x * x, axis=-1, keepdims=True)
    return x * lax.rsqrt(ms + NORM_EPS) * g


def _gelu(x):
    c = math.sqrt(2.0 / math.pi)
    return 0.5 * x * (1.0 + jnp.tanh(c * (x + 0.044715 * (x * x * x))))


def _sigmoid(x):
    return 1.0 / (1.0 + jnp.exp(-x))


def _log_sigmoid(x):
    return jnp.minimum(x, 0.0) - jnp.log1p(jnp.exp(-jnp.abs(x)))


def _split3(x):
    hi = x.astype(BF16)
    r1 = x - hi.astype(F32)
    mid = r1.astype(BF16)
    lo = (r1 - mid.astype(F32)).astype(BF16)
    return hi, mid, lo


def _dot(a, b):
    return jnp.dot(a, b, preferred_element_type=F32)


def _dot_nt(a, b):
    return lax.dot_general(a, b, (((1,), (1,)), ((), ())), preferred_element_type=F32)


def _dot_nt_f32(a, b):
    a0, a1, a2 = _split3(a)
    b0, b1, b2 = _split3(b)
    return (_dot_nt(a0, b0) + (_dot_nt(a0, b1) + _dot_nt(a1, b0))
            + (_dot_nt(a0, b2) + _dot_nt(a1, b1) + _dot_nt(a2, b0)))


def _tri_cumsum(x, n):
    r = lax.broadcasted_iota(jnp.int32, (n, n), 0)
    c = lax.broadcasted_iota(jnp.int32, (n, n), 1)
    tri = jnp.where(c <= r, 1.0, 0.0).astype(BF16)
    hi, mid, lo = _split3(x)
    return _dot(tri, hi) + _dot(tri, mid) + _dot(tri, lo)


def _s5_param_kernel(are_r, aim_r, ls_ref, btre, btim, ctre, ctim,
                     k_ref, wre_ref, wim_ref, zre_ref, zim_ref, mu_ref):
    dt = jnp.exp(ls_ref[0])
    lre, lim = are_r[0], aim_r[0]
    mag = jnp.exp(lre * dt)
    abr = mag * jnp.cos(lim * dt)
    abi = mag * jnp.sin(lim * dt)
    den = lre * lre + lim * lim
    nr = abr - 1.0
    zre = (nr * lre + abi * lim) / den
    zim = (abi * lre - nr * lim) / den

    def cpow(e):
        m = jnp.exp(lre * dt * e)
        a = lim * dt * e
        return m * jnp.cos(a), m * jnp.sin(a)

    def per_step(e):
        step = lax.broadcasted_iota(jnp.int32, (SSM_BLOCK, SSM_STATE), 0).astype(F32)
        pr, pi = cpow(e(step))
        rep = lambda t: jnp.broadcast_to(t[:, None, :], (SSM_BLOCK, SSM_GROUP, SSM_STATE)).reshape(
            BLOCK_LANES, SSM_STATE)
        return rep(pr), rep(pi)

    bre, bim = btre[0], btim[0]
    bbre = zre * bre - zim * bim
    bbim = zre * bim + zim * bre
    pwr, pwi = per_step(lambda s: (SSM_BLOCK - 1.0) - s)
    wre_ref[0] = (pwr * bbre - pwi * bbim).astype(BF16)
    wim_ref[0] = (pwr * bbim + pwi * bbre).astype(BF16)
    pbr, pbi = per_step(lambda s: -s)
    bxre = pbr * bbre - pbi * bbim
    bxim = pbr * bbim + pbi * bbre

    cre, cim = ctre[0], ctim[0]
    pcr, pci = per_step(lambda t: t)
    cxre = pcr * cre - pci * cim
    cxim = pcr * cim + pci * cre
    pzr, pzi = per_step(lambda t: t + 1.0)
    zre_ref[0] = (pzr * cre - pzi * cim).astype(BF16)
    zim_ref[0] = (-(pzr * cim + pzi * cre)).astype(BF16)

    kfull = _dot_nt_f32(bxre, cxre) - _dot_nt_f32(bxim, cxim)
    rs = lax.broadcasted_iota(jnp.int32, (BLOCK_LANES, BLOCK_LANES), 0) >> SSM_BLOCK_LOG2
    ct = lax.broadcasted_iota(jnp.int32, (BLOCK_LANES, BLOCK_LANES), 1) >> SSM_BLOCK_LOG2
    k_ref[0] = jnp.where(ct >= rs, kfull, 0.0).astype(BF16)

    mur, mui = cpow(float(SSM_BLOCK))
    mu_ref[0, 0:1, :] = mur
    mu_ref[0, 1:2, :] = mui


def _s5_params(a_re, a_im, log_step, b_re, b_im, c_re, c_im):
    g, p = a_re.shape
    bt = jnp.tile(jnp.swapaxes(b_re, 1, 2), (1, SSM_BLOCK, 1)), jnp.tile(jnp.swapaxes(b_im, 1, 2), (1, SSM_BLOCK, 1))
    ct = jnp.tile(c_re, (1, SSM_BLOCK, 1)), jnp.tile(c_im, (1, SSM_BLOCK, 1))
    row = pl.BlockSpec((1, 1, p), lambda i: (i, 0, 0))
    one = pl.BlockSpec((1, 1, 1), lambda i: (i, 0, 0))
    tall = pl.BlockSpec((1, BLOCK_LANES, p), lambda i: (i, 0, 0))
    return pl.pallas_call(
        _s5_param_kernel,
        grid=(g,),
        in_specs=[row, row, one, tall, tall, tall, tall],
        out_specs=[pl.BlockSpec((1, BLOCK_LANES, BLOCK_LANES), lambda i: (i, 0, 0)),
                   tall, tall, tall, tall, pl.BlockSpec((1, 2, p), lambda i: (i, 0, 0))],
        out_shape=[jax.ShapeDtypeStruct((g, BLOCK_LANES, BLOCK_LANES), BF16)]
        + [jax.ShapeDtypeStruct((g, BLOCK_LANES, p), BF16)] * 4
        + [jax.ShapeDtypeStruct((g, 2, p), F32)],
        compiler_params=_cparams(1),
        name="s5_params",
    )(a_re[:, None, :], a_im[:, None, :], log_step[:, None, None], bt[0], bt[1], ct[0], ct[1])


def _norm_kernel(x_ref, g_ref, u_ref):
    u_ref[...] = _rms(x_ref[...], g_ref[...]).astype(u_ref.dtype)


def _norm(x, g, tm):
    t = x.shape[0]
    tile = pl.BlockSpec((tm, D_MODEL), lambda i: (i, 0))
    return pl.pallas_call(
        _norm_kernel, grid=(t // tm,),
        in_specs=[tile, _full((1, D_MODEL))], out_specs=tile,
        out_shape=jax.ShapeDtypeStruct((t, D_MODEL), F32),
        compiler_params=_cparams(1), name="s5_prenorm",
    )(x, g[None, :])


def _s5_core_kernel(nb, n_seq, has_h0, *refs):
    if has_h0:
        u_ref, kk_ref, ww_ref, zz_ref, mu_ref, h0_ref, y_ref, f_ref, s_scr = refs
    else:
        u_ref, kk_ref, ww_ref, zz_ref, mu_ref, y_ref, f_ref, s_scr = refs
    rows = nb * n_seq
    half = COLUMN_GROUPS * SSM_STATE
    xp = [jnp.concatenate([u_ref[pl.ds(2 * p, rows, stride=SSM_BLOCK), :],
                           u_ref[pl.ds(2 * p + 1, rows, stride=SSM_BLOCK), :]], axis=1).astype(BF16)
          for p in range(STEP_PAIRS)]
    s = _dot(xp[0], ww_ref[0, 0])
    for p in range(1, STEP_PAIRS):
        s = s + _dot(xp[p], ww_ref[0, p])
    sre, sim = s[:, :half], s[:, half:]
    mu = mu_ref[0]
    mre, mim = mu[0:1, :], mu[1:2, :]
    blk = lax.broadcasted_iota(jnp.int32, (rows, half), 0) & (nb - 1)
    if has_h0:
        h0 = h0_ref[0]
        h0re, h0im = h0[:, :half], h0[:, half:]
        sre = sre + (mre * h0re - mim * h0im)
        sim = sim + (mre * h0im + mim * h0re)
    d = 1
    while d < nb:
        keep = blk >= d
        shr = jnp.where(keep, pltpu.roll(sre, d, 0), 0.0)
        shi = jnp.where(keep, pltpu.roll(sim, d, 0), 0.0)
        sre, sim = sre + (mre * shr - mim * shi), sim + (mre * shi + mim * shr)
        mre, mim = mre * mre - mim * mim, 2.0 * mre * mim
        d *= 2
    n_chunks = half // LANES
    for i in range(n_chunks):
        s_scr[i] = sre[:, i * LANES:(i + 1) * LANES]
        s_scr[n_chunks + i] = sim[:, i * LANES:(i + 1) * LANES]
    f_ref[0, 0] = jnp.concatenate([s_scr[i, pl.ds(nb - 1, n_seq, stride=nb), :] for i in range(2 * n_chunks)],
                                  axis=1)
    first = blk == 0
    inre = jnp.where(first, 0.0, pltpu.roll(sre, 1, 0))
    inim = jnp.where(first, 0.0, pltpu.roll(sim, 1, 0))
    if has_h0:
        inre = inre + h0re
        inim = inim + h0im
    s_in = jnp.concatenate([inre, inim], axis=1).astype(BF16)
    for p in range(STEP_PAIRS):
        y = _dot_nt(s_in, zz_ref[0, p])
        for q in range(p + 1):
            y = y + _dot(xp[q], kk_ref[0, p - q])
        y_ref[pl.ds(2 * p, rows, stride=SSM_BLOCK), :] = y[:, :COLUMN_LANES]
        y_ref[pl.ds(2 * p + 1, rows, stride=SSM_BLOCK), :] = y[:, COLUMN_LANES:]


def _s5_block_diagonal(params):
    kmat, wre, wim, zre, zim, mu = params
    cg, nc = COLUMN_GROUPS, SSM_GROUPS // COLUMN_GROUPS
    rows = 2 * COLUMN_LANES
    grp = lambda n, width: (jnp.arange(n) // width) % cg

    def stack_rows(m, pairs_on_rows):
        minor = m.shape[-1]
        if pairs_on_rows:
            m6 = m.reshape(nc, cg, STEP_PAIRS, 2, SSM_GROUP, minor)
            return jnp.transpose(m6, (0, 2, 3, 1, 4, 5)).reshape(nc, STEP_PAIRS, rows, minor)
        m6 = m.reshape(nc, cg, 2, SSM_GROUP, STEP_PAIRS, minor // STEP_PAIRS)
        return jnp.transpose(m6, (0, 4, 2, 1, 3, 5)).reshape(nc, STEP_PAIRS, rows, minor // STEP_PAIRS)

    def spread(stacked, width):
        n_in = stacked.shape[-1]
        n_out = n_in * cg
        src = (jnp.arange(n_out) // (width * cg)) * width + jnp.arange(n_out) % width
        place = (jnp.arange(n_in)[:, None] == src[None, :]).astype(BF16)
        own = (grp(rows, SSM_GROUP)[:, None] == grp(n_out, width)[None, :]).astype(BF16)
        return jnp.einsum('cprk,kn->cprn', stacked, place, preferred_element_type=BF16) * own

    kk = spread(stack_rows(kmat[:, :2 * SSM_GROUP, :], False), SSM_GROUP)
    ww = spread(stack_rows(jnp.concatenate([wre, wim], axis=2), True), SSM_STATE)
    zz_t = spread(stack_rows(jnp.concatenate([zre, zim], axis=2), True), SSM_STATE)
    mu_c = jnp.transpose(mu.reshape(nc, cg, 2, SSM_STATE), (0, 2, 1, 3)).reshape(nc, 2, cg * SSM_STATE)
    return kk, ww, zz_t, mu_c


def _s5_core(u, bd, n_seq, seq_len, h0, fold_sequences):
    kk, ww, zz, mu_c = bd
    nc = kk.shape[0]
    nb = seq_len // SSM_BLOCK
    seqs = n_seq if fold_sequences else 1
    n_steps = n_seq // seqs
    rows_tok = seqs * seq_len
    rows = seqs * nb
    state = 2 * COLUMN_GROUPS * SSM_STATE
    col = pl.BlockSpec((rows_tok, COLUMN_LANES), lambda c, i: (i, c))
    per_c = lambda a: pl.BlockSpec((1,) + a.shape[1:], lambda c, i: (c,) + (0,) * (a.ndim - 1))
    in_specs = [col, per_c(kk), per_c(ww), per_c(zz), per_c(mu_c)]
    args = [u, kk, ww, zz, mu_c]
    if h0 is not None:
        in_specs.append(pl.BlockSpec((1, rows, state), lambda c, i: (c, i, 0)))
        args.append(h0)
    y, fin = pl.pallas_call(
        functools.partial(_s5_core_kernel, nb, seqs, h0 is not None),
        grid=(nc, n_steps),
        in_specs=in_specs,
        out_specs=[col, pl.BlockSpec((1, 1, seqs, state), lambda c, i: (c, i, 0, 0))],
        out_shape=[jax.ShapeDtypeStruct(u.shape, F32), jax.ShapeDtypeStruct((nc, n_steps, seqs, state), F32)],
        scratch_shapes=[pltpu.VMEM((state // LANES, rows, LANES), F32)],
        compiler_params=_cparams(2), name="s5_core",
    )(*args)
    return y, fin.reshape(nc, n_seq, state)


def _ffn_kernel(pre, final_norm, tiles_per_seq, seqs_per_tile, tm, *refs):
    refs = list(refs)
    x_ref = refs.pop(0)
    if pre == "glu":
        ys_ref, gmix_ref, dskip_ref, wglu_ref = refs[:4]
        refs = refs[4:]
    else:
        o_ref, wo_ref = refs[:2]
        refs = refs[2:]
    (hist_ref, gffn_ref, wup_ref, wgate_ref, convw_ref, convb_ref, wdown_ref) = refs[:7]
    refs = refs[7:]
    if final_norm:
        gfin_ref = refs.pop(0)
    out_ref, cstate_ref, carry = refs

    x = x_ref[...]
    if pre == "glu":
        u = _rms(x, gmix_ref[...])
        g = _gelu(ys_ref[...] + dskip_ref[...] * u).astype(BF16)
        z = _dot(g, wglu_ref[...])
        x = x + z[:, :D_MODEL] * _sigmoid(z[:, D_MODEL:])
    else:
        x = x + _dot(o_ref[...], wo_ref[...])

    h = _rms(x, gffn_ref[...]).astype(BF16)
    seqs = max(1, seqs_per_tile)
    span = tm // seqs

    if seqs_per_tile == 0:
        @pl.when(pl.program_id(0) % tiles_per_seq == 0)
        def _():
            carry[...] = hist_ref[0]

    for lo, hi in zip(FFN_CHUNK_EDGES[:-1], FFN_CHUNK_EDGES[1:]):
        cs = slice(lo, hi)
        row = lax.broadcasted_iota(jnp.int32, (tm, hi - lo), 0)
        a = _dot(h, wup_ref[:, cs])
        a1, a2 = pltpu.roll(a, 1, 0), pltpu.roll(a, 2, 0)
        for s in range(seqs):
            prev = carry[:, cs] if seqs_per_tile == 0 else hist_ref[s, :, cs]
            a1 = jnp.where(row == s * span, prev[1:2, :], a1)
            a2 = jnp.where(row == s * span, prev[0:1, :], jnp.where(row == s * span + 1, prev[1:2, :], a2))
            cstate_ref[s, :, cs] = a[(s + 1) * span - (CONV_WIDTH - 1):(s + 1) * span, :]
        conv = a2 * convw_ref[0:1, cs] + a1 * convw_ref[1:2, cs] + a * convw_ref[2:3, cs] + convb_ref[:, cs]
        if seqs_per_tile == 0:
            carry[:, cs] = a[tm - (CONV_WIDTH - 1):, :]
        hh = (_gelu(conv) * _dot(h, wgate_ref[:, cs])).astype(BF16)
        x = x + _dot(hh, wdown_ref[cs, :])
    if final_norm:
        x = _rms(x, gfin_ref[...])
    out_ref[...] = x


def _ffn(x, pre, pre_args, hist, g_ffn, w_up, w_gate, conv_w, conv_b, w_down, g_final, seq_len, tm):
    t = x.shape[0]
    n_seq = t // seq_len
    tiles_per_seq = max(1, seq_len // tm)
    seqs_per_tile = 0 if tm < seq_len else tm // seq_len
    tile = pl.BlockSpec((tm, D_MODEL), lambda i: (i, 0))
    vec = _full((1, D_MODEL))
    const = lambda a: pl.BlockSpec(a.shape, lambda i: (0,) * a.ndim, pipeline_mode=pl.Buffered(1))
    per_seq = pl.BlockSpec((max(1, seqs_per_tile), CONV_WIDTH - 1, D_FF), lambda i: (i // tiles_per_seq, 0, 0))
    args, in_specs = [x], [tile]
    if pre == "glu":
        ys, g_mix, d_skip, w_glu = pre_args
        args += [ys, g_mix[None, :], d_skip[None, :], w_glu]
        in_specs += [tile, vec, vec, const(w_glu)]
    else:
        o, w_o = pre_args
        args += [o, w_o]
        in_specs += [tile, const(w_o)]
    args += [hist, g_ffn[None, :], w_up, w_gate, conv_w, conv_b[None, :], w_down]
    in_specs += [per_seq, vec, const(w_up), const(w_gate), _full((CONV_WIDTH, D_FF)), _full((1, D_FF)),
                 const(w_down)]
    if g_final is not None:
        args.append(g_final[None, :])
        in_specs.append(vec)
    return pl.pallas_call(
        functools.partial(_ffn_kernel, pre, g_final is not None, tiles_per_seq, seqs_per_tile, tm),
        grid=(t // tm,),
        in_specs=in_specs,
        out_specs=[tile, per_seq],
        out_shape=[jax.ShapeDtypeStruct((t, D_MODEL), F32),
                   jax.ShapeDtypeStruct((n_seq, CONV_WIDTH - 1, D_FF), F32)],
        scratch_shapes=[pltpu.VMEM((CONV_WIDTH - 1, D_FF), F32)],
        compiler_params=_cparams(1), name="ffn_" + pre,
    )(*args)


def _c_pieces(c):
    hi, mid, lo = _split3(c * LOG2E)
    return (hi.astype(F32) + pltpu.roll(mid.astype(F32), FOX_HEADS, 1)
            + pltpu.roll(lo.astype(F32), 2 * FOX_HEADS, 1)).astype(BF16)


def _key_bias_lanes(c3):
    lane = lax.broadcasted_iota(jnp.int32, c3.shape, 1)
    ones = ((lane >= _CQ_LANE) & (lane < _CQ_LANE + 3)).astype(F32)
    return (ones - c3.astype(F32)).astype(BF16)


def _qkv_kernel(tiles_per_seq, tm, transposed_v, x_ref, g_ref, wq_ref, wk_ref, wv_ref, wf_ref, bf_ref,
                k_ref, v_ref, lf_ref, qb_ref, kb_ref, vb_ref, c3_ref, kc_ref, carry):
    u = _rms(x_ref[...], g_ref[...]).astype(BF16)
    q = _dot(u, wq_ref[...]) * (FOX_HEAD_DIM ** -0.5 * LOG2E)
    k = _dot(u, wk_ref[...])
    v = _dot(u, wv_ref[...])
    k_ref[...] = k
    v_ref[...] = v
    kb_ref[...] = k.astype(BF16)
    if transposed_v:
        qb_ref[0, 0] = jnp.transpose(q).astype(BF16)
        vb_ref[0, 0] = jnp.transpose(v).astype(BF16)
    else:
        qb_ref[...] = q.astype(BF16)
        vb_ref[...] = v.astype(BF16)
    lane = lax.broadcasted_iota(jnp.int32, (tm, LANES), 1)
    logf = jnp.where(lane < FOX_HEADS, _log_sigmoid(_dot(u, wf_ref[...]) + bf_ref[...]), 0.0)
    lf_ref[...] = logf[:, :FOX_HEADS]

    @pl.when(pl.program_id(0) % tiles_per_seq == 0)
    def _():
        carry[...] = jnp.zeros_like(carry)

    c = _tri_cumsum(logf, tm) + carry[0:1, :]
    carry[...] = jnp.broadcast_to(c[tm - 1:tm, :], carry.shape)
    c3 = _c_pieces(c)
    if transposed_v:
        c3_ref[0, 0] = jnp.transpose(c3.astype(F32))
    else:
        c3_ref[...] = c3
    kc_ref[...] = _key_bias_lanes(c3)


def _qkv(x, g, wq, wk, wv, wf, bf, seq_len, tm, transposed_v):
    t = x.shape[0]
    tiles_per_seq = seq_len // tm
    tile = pl.BlockSpec((tm, D_MODEL), lambda i: (i, 0))
    narrow = pl.BlockSpec((tm, LANES), lambda i: (i, 0))
    const = lambda a: pl.BlockSpec(a.shape, lambda i: (0,) * a.ndim, pipeline_mode=pl.Buffered(1))
    wide_bf = jax.ShapeDtypeStruct((t, D_MODEL), BF16)
    if transposed_v:
        per_tile = lambda i: (i // tiles_per_seq, i % tiles_per_seq, 0, 0)
        qv_spec = pl.BlockSpec((1, 1, D_MODEL, tm), per_tile)
        qv_shape = jax.ShapeDtypeStruct((t // seq_len, tiles_per_seq, D_MODEL, tm), BF16)
        c3_spec = pl.BlockSpec((1, 1, LANES, tm), per_tile)
        c3_shape = jax.ShapeDtypeStruct((t // seq_len, tiles_per_seq, LANES, tm), F32)
    else:
        qv_spec, qv_shape = tile, wide_bf
        c3_spec, c3_shape = narrow, jax.ShapeDtypeStruct((t, LANES), BF16)
    return pl.pallas_call(
        functools.partial(_qkv_kernel, tiles_per_seq, tm, transposed_v),
        grid=(t // tm,),
        in_specs=[tile, _full((1, D_MODEL)), const(wq), const(wk), const(wv), const(wf), _full((1, LANES))],
        out_specs=[tile, tile, pl.BlockSpec((tm, FOX_HEADS), lambda i: (i, 0)), qv_spec, tile, qv_spec,
                   c3_spec, narrow],
        out_shape=[jax.ShapeDtypeStruct((t, D_MODEL), F32), jax.ShapeDtypeStruct((t, D_MODEL), F32),
                   jax.ShapeDtypeStruct((t, FOX_HEADS), F32),
                   qv_shape, wide_bf, qv_shape, c3_shape, jax.ShapeDtypeStruct((t, LANES), BF16)],
        scratch_shapes=[pltpu.VMEM((8, LANES), F32)],
        compiler_params=_cparams(1), name="fox_qkv",
    )(x, g[None, :], wq, wk, wv, wf, bf)


def _query_ext(q, c3q, head, lo_half):
    rows = q.shape[0]
    lane = lax.broadcasted_iota(jnp.int32, (rows, LANES), 1)
    own = (lane < FOX_HEAD_DIM) if lo_half else (lane >= FOX_HEAD_DIM)
    qm = jnp.where(own, q, jnp.zeros_like(q))
    pick = lambda j: jnp.sum(jnp.where(lane == head + j * FOX_HEADS, c3q, 0.0), axis=1, keepdims=True)
    onehot = (lane == head) | (lane == head + FOX_HEADS) | (lane == head + 2 * FOX_HEADS)
    qc = jnp.where(lane == _CQ_LANE, pick(0),
                   jnp.where(lane == _CQ_LANE + 1, pick(1),
                             jnp.where(lane == _CQ_LANE + 2, pick(2), jnp.where(onehot, 1.0, 0.0))))
    return jnp.concatenate([qm, qc.astype(BF16)], axis=1)


def _sublane_allreduce(x, op):
    for sh in (4, 2, 1):
        x = op(x, pltpu.roll(x, sh, 0))
    return x


def _fold_rows(x, op):
    parts = [x[i * SUBLANES:(i + 1) * SUBLANES] for i in range(x.shape[0] // SUBLANES)]
    while len(parts) > 1:
        parts = [op(parts[i], parts[i + 1]) for i in range(0, len(parts), 2)]
    return parts[0]


KEY_CHUNK = 256
ATTN_PAIRS_PER_STEP = 2
ONES_ROWS = 16


def _attn_prompt_kernel(t, n_items, qi_tab, j_tab, q_ref, c3_ref, k_ref, kc_ref, vt_ref, o_ref,
                        qt, e_buf, cm_buf, acc):
    n_heads = acc.shape[0]
    first_head = pl.program_id(1) * n_heads
    pair_lanes = lambda a: slice((a // 2) * PAIR_LANES, (a // 2 + 1) * PAIR_LANES)
    big = -NEG_INF
    ck = KEY_CHUNK
    n_ck = t // ck
    for a in range(n_heads):
        acc[a] = jnp.zeros(acc.shape[1:], F32)
    ones_rows = jnp.ones((ONES_ROWS, ck), BF16)

    def load_q(qi):
        row = lax.broadcasted_iota(jnp.int32, (PAIR_LANES, t), 0)
        for a in range(n_heads):
            head = first_head + a
            own = (row < FOX_HEAD_DIM) if a % 2 == 0 else (row >= FOX_HEAD_DIM)
            q_pair = q_ref[0, qi, pair_lanes(a), :]
            piece = lambda j: c3_ref[0, qi, pl.ds(head + j * FOX_HEADS, 1), :]
            onehot = (row == head) | (row == head + FOX_HEADS) | (row == head + 2 * FOX_HEADS)
            bias = jnp.where(row == _CQ_LANE, piece(0),
                             jnp.where(row == _CQ_LANE + 1, piece(1),
                                       jnp.where(row == _CQ_LANE + 2, piece(2), jnp.where(onehot, 1.0, 0.0))))
            qt[a] = jnp.concatenate([jnp.where(own, q_pair, jnp.zeros_like(q_pair)), bias.astype(BF16)], axis=0)

    def stage_a(w, slot, diagonal):
        start = pl.multiple_of(j_tab[w] * t, t)
        width = 2 * LANES
        hk = ck
        for a in range(n_heads):
            for i in range(t // hk):
                rows = pl.ds(start + i * hk, hk)
                kext = jnp.concatenate([k_ref[rows, pair_lanes(a)], kc_ref[rows, :]], axis=1)
                for q0 in range(0, t, width):
                    if diagonal and i * hk > q0 + width - 1:
                        e_buf[slot, a, i * hk:(i + 1) * hk, q0:q0 + width] = jnp.zeros((hk, width), BF16)
                        cm_buf[slot, a, i, :, q0:q0 + width] = jnp.full((SUBLANES, width), NEG_INF, F32)
                        continue
                    s = _dot(kext, qt[a, :, q0:q0 + width])
                    if diagonal and (i + 1) * hk - 1 > q0:
                        r = lax.broadcasted_iota(jnp.int32, (hk, width), 0) + i * hk
                        c = lax.broadcasted_iota(jnp.int32, (hk, width), 1) + q0
                        s = jnp.where(r <= c, s, NEG_INF)
                    cm = _sublane_allreduce(_fold_rows(s, jnp.maximum), jnp.maximum)
                    e_buf[slot, a, i * hk:(i + 1) * hk, q0:q0 + width] = jnp.exp2(
                        s - jnp.tile(cm, (hk // SUBLANES, 1))).astype(BF16)
                    cm_buf[slot, a, i, :, q0:q0 + width] = cm

    def stage_c(w, ms, slot):
        m_floor = jnp.where(j_tab[w] == 0, NEG_INF, big)
        new_ms = []
        for a in range(n_heads):
            m_old = jnp.minimum(ms[a], m_floor)
            cms = [cm_buf[slot, a, i] for i in range(n_ck)]
            m_new = functools.reduce(jnp.maximum, cms, m_old)
            n_rep = acc.shape[1] // SUBLANES
            total = acc[a] * jnp.tile(jnp.exp2(m_old - m_new), (n_rep, 1))
            for i in range(n_ck):
                vt = jnp.concatenate([vt_ref[0, j_tab[w], pair_lanes(a), i * ck:(i + 1) * ck], ones_rows], axis=0)
                pv = _dot(vt, e_buf[slot, a, i * ck:(i + 1) * ck, :])
                total = total + pv * jnp.tile(jnp.exp2(cms[i] - m_new), (n_rep, 1))
            acc[a] = total
            new_ms.append(m_new)
        return tuple(new_ms)

    def finalize(w):
        outs = []
        for a in range(n_heads):
            full = acc[a]
            denom = jnp.tile(full[PAIR_LANES:PAIR_LANES + SUBLANES], (PAIR_LANES // SUBLANES, 1))
            outs.append(full[:PAIR_LANES] / denom)
        row = lax.broadcasted_iota(jnp.int32, (PAIR_LANES, t), 0)
        rows = pl.ds(pl.multiple_of(qi_tab[w] * t, t), t)
        for a in range(0, n_heads, 2):
            o = jnp.transpose(jnp.where(row < FOX_HEAD_DIM, outs[a], outs[a + 1]))
            o_ref[rows, pair_lanes(a)] = o.astype(o_ref.dtype)

    load_q(0)
    stage_a(0, 0, True)

    def item(w, ms, slot):
        nxt = jnp.minimum(w + 1, n_items - 1)

        @pl.when(j_tab[nxt] == 0)
        def _():
            load_q(qi_tab[nxt])

        def stages(diagonal, ms):
            stage_a(nxt, 1 - slot, diagonal)
            ms = stage_c(w, ms, slot)
            return ms

        ms = lax.cond(j_tab[nxt] == qi_tab[nxt], functools.partial(stages, True),
                      functools.partial(stages, False), ms)

        @pl.when(j_tab[w] == qi_tab[w])
        def _():
            finalize(w)

        return ms

    def body(i, ms):
        return item(2 * i + 1, item(2 * i, ms, 0), 1)

    ms = lax.fori_loop(0, n_items // 2, body, tuple(jnp.full((SUBLANES, t), NEG_INF, F32) for _ in range(n_heads)))
    if n_items % 2:
        item(n_items - 1, ms, 0)


def _attn_prompt(qt, kb, kc, vt, c3t, n_seq, seq_len, t):
    total = kb.shape[0]
    nt = seq_len // t
    items = [(qi, j) for qi in range(nt) for j in range(qi + 1)]
    qi_tab = jnp.asarray([qi for qi, _ in items], jnp.int32)
    j_tab = jnp.asarray([j for _, j in items], jnp.int32)
    nh = 2 * ATTN_PAIRS_PER_STEP
    width = ATTN_PAIRS_PER_STEP * PAIR_LANES
    once = pl.Buffered(1)
    seq_in = pl.BlockSpec((seq_len, width), lambda b, p, *_: (b, p), pipeline_mode=once)
    seq_bias = pl.BlockSpec((seq_len, LANES), lambda b, p, *_: (b, 0), pipeline_mode=once)
    vt_in = pl.BlockSpec((1, nt, width, t), lambda b, p, *_: (b, 0, p, 0), pipeline_mode=once)
    c3t_in = pl.BlockSpec((1, nt, LANES, t), lambda b, p, *_: (b, 0, 0, 0), pipeline_mode=once)
    return pl.pallas_call(
        functools.partial(_attn_prompt_kernel, t, len(items)),
        grid_spec=pltpu.PrefetchScalarGridSpec(
            num_scalar_prefetch=2,
            grid=(n_seq, HEAD_PAIRS // ATTN_PAIRS_PER_STEP),
            in_specs=[vt_in, c3t_in, seq_in, seq_bias, vt_in],
            out_specs=pl.BlockSpec((seq_len, width), lambda b, p, *_: (b, p)),
            scratch_shapes=[pltpu.VMEM((nh, 2 * LANES, t), BF16), pltpu.VMEM((2, nh, t, t), BF16),
                            pltpu.VMEM((2, nh, t // KEY_CHUNK, SUBLANES, t), F32),
                            pltpu.VMEM((nh, PAIR_LANES + ONES_ROWS, t), F32)]),
        out_shape=jax.ShapeDtypeStruct((total, D_MODEL), BF16),
        compiler_params=_cparams(2), name="fox_attn_prompt",
    )(qi_tab, j_tab, qt, c3t, kb, kc, vt)


def _attn_sample_kernel(n_cache_tiles, tk, t_new, q_ref, c3q_ref, kc_ref, vc_ref, lfc_ref,
                        kn_ref, vn_ref, lfn_ref, o_ref, qt, m_scr, acc_scr, carry, s_buf, p_buf, al_buf):
    j = pl.program_id(1)
    rows = 2 * t_new

    @pl.when(j == 0)
    def _():
        c3q = c3q_ref[...].astype(F32)
        for p in range(HEAD_PAIRS):
            q = q_ref[:, p * PAIR_LANES:(p + 1) * PAIR_LANES]
            qe = jnp.concatenate([_query_ext(q, c3q, 2 * p, True), _query_ext(q, c3q, 2 * p + 1, False)], axis=0)
            qt[p] = jnp.transpose(qe.astype(F32)).astype(BF16)
        m_scr[...] = jnp.full(m_scr.shape, NEG_INF, F32)
        acc_scr[...] = jnp.zeros(acc_scr.shape, F32)
        carry[...] = jnp.zeros_like(carry)

    def attend(k_pair, v_pair, n_keys, c3k, masked):
        kc = _key_bias_lanes(c3k)
        for p in range(HEAD_PAIRS):
            s = _dot(jnp.concatenate([k_pair(p), kc], axis=1), qt[p])
            if masked:
                r = lax.broadcasted_iota(jnp.int32, (n_keys, rows), 0)
                c = lax.broadcasted_iota(jnp.int32, (n_keys, rows), 1) & (t_new - 1)
                s = jnp.where(r <= c, s, NEG_INF)
            s_buf[p, :n_keys] = s
        for p in range(HEAD_PAIRS):
            s = s_buf[p, :n_keys]
            m_old = m_scr[p]
            m_new = jnp.maximum(m_old, _sublane_allreduce(_fold_rows(s, jnp.maximum), jnp.maximum))
            p_buf[p, :n_keys] = jnp.exp2(s - jnp.tile(m_new, (n_keys // SUBLANES, 1))).astype(BF16)
            alpha = jnp.exp2(m_old - m_new)
            al_buf[p] = jnp.transpose(jnp.tile(alpha, (rows // SUBLANES, 1)))
            m_scr[p] = m_new
        ones = jnp.ones((n_keys, LANES), BF16)
        for p in range(HEAD_PAIRS):
            pv = lax.dot_general(p_buf[p, :n_keys], jnp.concatenate([v_pair(p), ones], axis=1),
                                 (((0,), (0,)), ((), ())), preferred_element_type=F32)
            acc_scr[p] = acc_scr[p] * jnp.tile(al_buf[p], (1, 2)) + pv

    def cache_pair(ref):
        return lambda p: ref[0, :, p * PAIR_LANES:(p + 1) * PAIR_LANES].astype(BF16)

    def new_pair(ref):
        return lambda p: ref[:, p * PAIR_LANES:(p + 1) * PAIR_LANES]

    @pl.when(j < n_cache_tiles)
    def _():
        c = _tri_cumsum(lfc_ref[0], tk) + carry[0:1, :]
        carry[...] = jnp.broadcast_to(c[tk - 1:tk, :], carry.shape)
        attend(cache_pair(kc_ref), cache_pair(vc_ref), tk, _c_pieces(c), False)

    @pl.when(j == n_cache_tiles)
    def _():
        c = _tri_cumsum(lfn_ref[...], t_new) + carry[0:1, :]
        attend(new_pair(kn_ref), new_pair(vn_ref), t_new, _c_pieces(c), True)
        lane = lax.broadcasted_iota(jnp.int32, (t_new, LANES), 1)
        for p in range(HEAD_PAIRS):
            acc = acc_scr[p]
            oa = acc[:t_new, :LANES] / acc[:t_new, LANES:]
            ob = acc[t_new:, :LANES] / acc[t_new:, LANES:]
            o_ref[:, p * PAIR_LANES:(p + 1) * PAIR_LANES] = jnp.where(lane < FOX_HEAD_DIM, oa, ob).astype(o_ref.dtype)


def _attn_sample(qb, kb, vb, c3, logf, cache_k, cache_v, cache_logf, tk):
    n_seq, past, _ = cache_k.shape
    t_new = qb.shape[0] // n_seq
    nct = past // tk
    pad = lambda a: jnp.pad(a, ((0, 0),) * (a.ndim - 1) + ((0, LANES - FOX_HEADS),))
    new_tile = pl.BlockSpec((t_new, D_MODEL), lambda b, j: (b, 0))
    new_c3 = pl.BlockSpec((t_new, LANES), lambda b, j: (b, 0))
    cache_tile = pl.BlockSpec((1, tk, D_MODEL), lambda b, j: (b, jnp.minimum(j, nct - 1), 0))
    lf_tile = pl.BlockSpec((1, tk, LANES), lambda b, j: (b, jnp.minimum(j, nct - 1), 0))
    return pl.pallas_call(
        functools.partial(_attn_sample_kernel, nct, tk, t_new),
        grid=(n_seq, nct + 1),
        in_specs=[new_tile, new_c3, cache_tile, cache_tile, lf_tile, new_tile, new_tile, new_c3],
        out_specs=new_tile,
        out_shape=jax.ShapeDtypeStruct((n_seq * t_new, D_MODEL), BF16),
        scratch_shapes=[pltpu.VMEM((HEAD_PAIRS, 2 * LANES, 2 * t_new), BF16),
                        pltpu.VMEM((HEAD_PAIRS, SUBLANES, 2 * t_new), F32),
                        pltpu.VMEM((HEAD_PAIRS, 2 * t_new, 2 * LANES), F32),
                        pltpu.VMEM((8, LANES), F32),
                        pltpu.VMEM((HEAD_PAIRS, tk, 2 * t_new), F32),
                        pltpu.VMEM((HEAD_PAIRS, tk, 2 * t_new), BF16),
                        pltpu.VMEM((HEAD_PAIRS, 2 * t_new, 2 * t_new), F32)],
        compiler_params=_cparams(2), name="fox_attn_sample",
    )(qb, c3, cache_k, cache_v, pad(cache_logf), kb, vb, pad(logf))


def _trunk(x, n_seq, seq_len, ssm_h0, fox_cache, conv_hist, w, tm, tq):
    t = n_seq * seq_len
    nb = seq_len // SSM_BLOCK
    nc = SSM_GROUPS // COLUMN_GROUPS
    half = COLUMN_GROUPS * SSM_STATE
    x = x.reshape(t, D_MODEL)

    u0 = _norm(x, w["norm_mix"][0], tm)
    if ssm_h0 is None:
        h0 = None
    else:
        per_c = lambda h: jnp.swapaxes(h.reshape(n_seq, nc, half), 0, 1)
        h0c = jnp.concatenate([per_c(ssm_h0[0]), per_c(ssm_h0[1])], axis=2)
        h0 = jnp.zeros((nc, n_seq, nb, 2 * half), F32).at[:, :, 0, :].set(h0c).reshape(nc, n_seq * nb, 2 * half)
    ys, fin = _s5_core(u0, w["s5"], n_seq, seq_len, h0, fold_sequences=ssm_h0 is not None)
    per_g = lambda f: jnp.swapaxes(f.reshape(nc, n_seq, COLUMN_GROUPS, SSM_STATE), 0, 1).reshape(
        n_seq, SSM_GROUPS, SSM_STATE)
    fre, fim = per_g(fin[:, :, :half]), per_g(fin[:, :, half:])
    x, conv0 = _ffn(x, "glu", (ys, w["norm_mix"][0], w["ssm_d"], w["w_glu"]), conv_hist[0],
                    w["norm_ffn"][0], w["w_up"][0], w["w_gate"][0], w["conv_w"][0], w["conv_b"][0],
                    w["w_down"][0], None, seq_len, tm)

    k, v, logf, qb, kb, vb, c3, kc = _qkv(x, w["norm_mix"][1], w["wq"], w["wk"], w["wv"], w["wf"], w["bf"],
                                          seq_len, tq, transposed_v=fox_cache is None)
    if fox_cache is None:
        o = _attn_prompt(qb, kb, kc, vb, c3, n_seq, seq_len, tq)
    else:
        o = _attn_sample(qb, kb, vb, c3, logf, *fox_cache, tk=min(512, fox_cache[0].shape[1]))
    y, conv1 = _ffn(x, "oproj", (o, w["w_o"]), conv_hist[1],
                    w["norm_ffn"][1], w["w_up"][1], w["w_gate"][1], w["conv_w"][1], w["conv_b"][1],
                    w["w_down"][1], w["norm_final"], seq_len, tm)

    shape4 = (1, n_seq, seq_len, FOX_HEADS, FOX_HEAD_DIM)
    return (y.reshape(n_seq, seq_len, D_MODEL),
            fre[None], fim[None],
            k.reshape(shape4), v.reshape(shape4), logf.reshape(1, n_seq, seq_len, FOX_HEADS),
            jnp.stack([conv0, conv1]))


def kernel(x_prompt, x_sample, state_ssm_re, state_ssm_im, cache_fox_k, cache_fox_v, cache_fox_logf, state_ffn_conv, norm_mix, norm_ffn, norm_final, ssm_a_re, ssm_a_im, ssm_log_step, ssm_b_re, ssm_b_im, ssm_c_re, ssm_c_im, ssm_d, ssm_w_glu, fox_w_qkvf, fox_b_f, fox_w_o, ffn_w_up, ffn_w_gate, ffn_conv_w, ffn_conv_b, ffn_w_down):
    n_p, l_p, _ = x_prompt.shape
    n_s, l_s, _ = x_sample.shape
    past = cache_fox_k.shape[2]
    wqkvf = fox_w_qkvf[0]
    w = {
        "norm_mix": norm_mix, "norm_ffn": norm_ffn, "norm_final": norm_final,
        "s5": _s5_block_diagonal(_s5_params(ssm_a_re[0], ssm_a_im[0], ssm_log_step[0], ssm_b_re[0],
                                            ssm_b_im[0], ssm_c_re[0], ssm_c_im[0])),
        "ssm_d": ssm_d[0], "w_glu": ssm_w_glu[0].astype(BF16),
        "wq": wqkvf[:, :D_MODEL].astype(BF16), "wk": wqkvf[:, D_MODEL:2 * D_MODEL].astype(BF16),
        "wv": wqkvf[:, 2 * D_MODEL:3 * D_MODEL].astype(BF16),
        "wf": jnp.pad(wqkvf[:, 3 * D_MODEL:], ((0, 0), (0, LANES - FOX_HEADS))).astype(BF16),
        "bf": jnp.pad(fox_b_f[0], (0, LANES - FOX_HEADS))[None, :],
        "w_o": fox_w_o[0].astype(BF16),
        "w_up": ffn_w_up.astype(BF16), "w_gate": ffn_w_gate.astype(BF16), "w_down": ffn_w_down.astype(BF16),
        "conv_w": ffn_conv_w, "conv_b": ffn_conv_b,
    }
    zero_hist = jnp.zeros((2, n_p, CONV_WIDTH - 1, D_FF), F32)
    outs_p = _trunk(x_prompt, n_p, l_p, None, None, zero_hist, w, tm=512, tq=512)
    cache = (cache_fox_k[0].reshape(n_s, past, D_MODEL), cache_fox_v[0].reshape(n_s, past, D_MODEL),
             cache_fox_logf[0])
    outs_s = _trunk(x_sample, n_s, l_s, (state_ssm_re[0], state_ssm_im[0]), cache, state_ffn_conv, w,
                    tm=4 * l_s, tq=l_s)
    return (outs_p[0], outs_s[0]) + outs_p[1:] + outs_s[1:]
```

```python
import functools
import math

import jax
import jax.numpy as jnp
from jax import lax
from jax.experimental import pallas as pl
from jax.experimental.pallas import tpu as pltpu

D_MODEL = 1024
SSM_GROUP = 16
SSM_GROUPS = D_MODEL // SSM_GROUP
SSM_STATE = 64
SSM_BLOCK = 16
SSM_BLOCK_LOG2 = 4
BLOCK_LANES = SSM_BLOCK * SSM_GROUP
COLUMN_LANES = 128
COLUMN_GROUPS = COLUMN_LANES // SSM_GROUP
STEP_PAIRS = SSM_BLOCK // 2
FOX_HEADS = 16
FOX_HEAD_DIM = 64
HEAD_PAIRS = FOX_HEADS // 2
PAIR_LANES = 2 * FOX_HEAD_DIM
D_FF = 2816
CONV_WIDTH = 3
FFN_SUBTILES = 1
FFN_CHUNK_EDGES = (0, 1536, D_FF)
NORM_EPS = 1e-6
NEG_INF = -1e30
LOG2E = 1.4426950408889634
LANES = 128
SUBLANES = 8
_CQ_LANE = 3 * FOX_HEADS
VMEM_LIMIT = 56 * 1024 * 1024

BF16 = jnp.bfloat16
F32 = jnp.float32


def _cparams(n_axes, vmem=VMEM_LIMIT):
    return pltpu.CompilerParams(dimension_semantics=("arbitrary",) * n_axes, vmem_limit_bytes=vmem)


def _full(shape):
    nd = len(shape)
    return pl.BlockSpec(shape, lambda *_: (0,) * nd)


def _rms(x, g):
    ms = jnp.mean(x * x, axis=-1, keepdims=True)
    return x * lax.rsqrt(ms + NORM_EPS) * g


def _gelu(x):
    c = math.sqrt(2.0 / math.pi)
    return 0.5 * x * (1.0 + jnp.tanh(c * (x + 0.044715 * (x * x * x))))


def _sigmoid(x):
    return 1.0 / (1.0 + jnp.exp(-x))


def _log_sigmoid(x):
    return jnp.minimum(x, 0.0) - jnp.log1p(jnp.exp(-jnp.abs(x)))


def _split3(x):
    hi = x.astype(BF16)
    r1 = x - hi.astype(F32)
    mid = r1.astype(BF16)
    lo = (r1 - mid.astype(F32)).astype(BF16)
    return hi, mid, lo


def _dot(a, b):
    return jnp.dot(a, b, preferred_element_type=F32)


def _dot_nt(a, b):
    return lax.dot_general(a, b, (((1,), (1,)), ((), ())), preferred_element_type=F32)


def _dot_nt_f32(a, b):
    a0, a1, a2 = _split3(a)
    b0, b1, b2 = _split3(b)
    return (_dot_nt(a0, b0) + (_dot_nt(a0, b1) + _dot_nt(a1, b0))
            + (_dot_nt(a0, b2) + _dot_nt(a1, b1) + _dot_nt(a2, b0)))


def _tri_cumsum(x, n):
    r = lax.broadcasted_iota(jnp.int32, (n, n), 0)
    c = lax.broadcasted_iota(jnp.int32, (n, n), 1)
    tri = jnp.where(c <= r, 1.0, 0.0).astype(BF16)
    hi, mid, lo = _split3(x)
    return _dot(tri, hi) + _dot(tri, mid) + _dot(tri, lo)


def _s5_param_kernel(are_r, aim_r, ls_ref, btre, btim, ctre, ctim,
                     k_ref, wre_ref, wim_ref, zre_ref, zim_ref, mu_ref):
    dt = jnp.exp(ls_ref[0])
    lre, lim = are_r[0], aim_r[0]
    mag = jnp.exp(lre * dt)
    abr = mag * jnp.cos(lim * dt)
    abi = mag * jnp.sin(lim * dt)
    den = lre * lre + lim * lim
    nr = abr - 1.0
    zre = (nr * lre + abi * lim) / den
    zim = (abi * lre - nr * lim) / den

    def cpow(e):
        m = jnp.exp(lre * dt * e)
        a = lim * dt * e
        return m * jnp.cos(a), m * jnp.sin(a)

    def per_step(e):
        step = lax.broadcasted_iota(jnp.int32, (SSM_BLOCK, SSM_STATE), 0).astype(F32)
        pr, pi = cpow(e(step))
        rep = lambda t: jnp.broadcast_to(t[:, None, :], (SSM_BLOCK, SSM_GROUP, SSM_STATE)).reshape(
            BLOCK_LANES, SSM_STATE)
        return rep(pr), rep(pi)

    bre, bim = btre[0], btim[0]
    bbre = zre * bre - zim * bim
    bbim = zre * bim + zim * bre
    pwr, pwi = per_step(lambda s: (SSM_BLOCK - 1.0) - s)
    wre_ref[0] = (pwr * bbre - pwi * bbim).astype(BF16)
    wim_ref[0] = (pwr * bbim + pwi * bbre).astype(BF16)
    pbr, pbi = per_step(lambda s: -s)
    bxre = pbr * bbre - pbi * bbim
    bxim = pbr * bbim + pbi * bbre

    cre, cim = ctre[0], ctim[0]
    pcr, pci = per_step(lambda t: t)
    cxre = pcr * cre - pci * cim
    cxim = pcr * cim + pci * cre
    pzr, pzi = per_step(lambda t: t + 1.0)
    zre_ref[0] = (pzr * cre - pzi * cim).astype(BF16)
    zim_ref[0] = (-(pzr * cim + pzi * cre)).astype(BF16)

    kfull = _dot_nt_f32(bxre, cxre) - _dot_nt_f32(bxim, cxim)
    rs = lax.broadcasted_iota(jnp.int32, (BLOCK_LANES, BLOCK_LANES), 0) >> SSM_BLOCK_LOG2
    ct = lax.broadcasted_iota(jnp.int32, (BLOCK_LANES, BLOCK_LANES), 1) >> SSM_BLOCK_LOG2
    k_ref[0] = jnp.where(ct >= rs, kfull, 0.0).astype(BF16)

    mur, mui = cpow(float(SSM_BLOCK))
    mu_ref[0, 0:1, :] = mur
    mu_ref[0, 1:2, :] = mui


def _s5_params(a_re, a_im, log_step, b_re, b_im, c_re, c_im):
    g, p = a_re.shape
    bt = jnp.tile(jnp.swapaxes(b_re, 1, 2), (1, SSM_BLOCK, 1)), jnp.tile(jnp.swapaxes(b_im, 1, 2), (1, SSM_BLOCK, 1))
    ct = jnp.tile(c_re, (1, SSM_BLOCK, 1)), jnp.tile(c_im, (1, SSM_BLOCK, 1))
    row = pl.BlockSpec((1, 1, p), lambda i: (i, 0, 0))
    one = pl.BlockSpec((1, 1, 1), lambda i: (i, 0, 0))
    tall = pl.BlockSpec((1, BLOCK_LANES, p), lambda i: (i, 0, 0))
    return pl.pallas_call(
        _s5_param_kernel,
        grid=(g,),
        in_specs=[row, row, one, tall, tall, tall, tall],
        out_specs=[pl.BlockSpec((1, BLOCK_LANES, BLOCK_LANES), lambda i: (i, 0, 0)),
                   tall, tall, tall, tall, pl.BlockSpec((1, 2, p), lambda i: (i, 0, 0))],
        out_shape=[jax.ShapeDtypeStruct((g, BLOCK_LANES, BLOCK_LANES), BF16)]
        + [jax.ShapeDtypeStruct((g, BLOCK_LANES, p), BF16)] * 4
        + [jax.ShapeDtypeStruct((g, 2, p), F32)],
        compiler_params=_cparams(1),
        name="s5_params",
    )(a_re[:, None, :], a_im[:, None, :], log_step[:, None, None], bt[0], bt[1], ct[0], ct[1])


def _norm_kernel(x_ref, g_ref, u_ref):
    u_ref[...] = _rms(x_ref[...], g_ref[...]).astype(u_ref.dtype)


def _norm(x, g, tm):
    t = x.shape[0]
    tile = pl.BlockSpec((tm, D_MODEL), lambda i: (i, 0))
    return pl.pallas_call(
        _norm_kernel, grid=(t // tm,),
        in_specs=[tile, _full((1, D_MODEL))], out_specs=tile,
        out_shape=jax.ShapeDtypeStruct((t, D_MODEL), F32),
        compiler_params=_cparams(1), name="s5_prenorm",
    )(x, g[None, :])


def _s5_core_kernel(nb, n_seq, has_h0, *refs):
    if has_h0:
        u_ref, kk_ref, ww_ref, zz_ref, mu_ref, h0_ref, y_ref, f_ref, s_scr = refs
    else:
        u_ref, kk_ref, ww_ref, zz_ref, mu_ref, y_ref, f_ref, s_scr = refs
    rows = nb * n_seq
    half = COLUMN_GROUPS * SSM_STATE
    xp = [jnp.concatenate([u_ref[pl.ds(2 * p, rows, stride=SSM_BLOCK), :],
                           u_ref[pl.ds(2 * p + 1, rows, stride=SSM_BLOCK), :]], axis=1).astype(BF16)
          for p in range(STEP_PAIRS)]
    s = _dot(xp[0], ww_ref[0, 0])
    for p in range(1, STEP_PAIRS):
        s = s + _dot(xp[p], ww_ref[0, p])
    sre, sim = s[:, :half], s[:, half:]
    mu = mu_ref[0]
    mre, mim = mu[0:1, :], mu[1:2, :]
    blk = lax.broadcasted_iota(jnp.int32, (rows, half), 0) & (nb - 1)
    if has_h0:
        h0 = h0_ref[0]
        h0re, h0im = h0[:, :half], h0[:, half:]
        sre = sre + (mre * h0re - mim * h0im)
        sim = sim + (mre * h0im + mim * h0re)
    d = 1
    while d < nb:
        keep = blk >= d
        shr = jnp.where(keep, pltpu.roll(sre, d, 0), 0.0)
        shi = jnp.where(keep, pltpu.roll(sim, d, 0), 0.0)
        sre, sim = sre + (mre * shr - mim * shi), sim + (mre * shi + mim * shr)
        mre, mim = mre * mre - mim * mim, 2.0 * mre * mim
        d *= 2
    n_chunks = half // LANES
    for i in range(n_chunks):
        s_scr[i] = sre[:, i * LANES:(i + 1) * LANES]
        s_scr[n_chunks + i] = sim[:, i * LANES:(i + 1) * LANES]
    f_ref[0, 0] = jnp.concatenate([s_scr[i, pl.ds(nb - 1, n_seq, stride=nb), :] for i in range(2 * n_chunks)],
                                  axis=1)
    first = blk == 0
    inre = jnp.where(first, 0.0, pltpu.roll(sre, 1, 0))
    inim = jnp.where(first, 0.0, pltpu.roll(sim, 1, 0))
    if has_h0:
        inre = inre + h0re
        inim = inim + h0im
    s_in = jnp.concatenate([inre, inim], axis=1).astype(BF16)
    for p in range(STEP_PAIRS):
        y = _dot_nt(s_in, zz_ref[0, p])
        for q in range(p + 1):
            y = y + _dot(xp[q], kk_ref[0, p - q])
        y_ref[pl.ds(2 * p, rows, stride=SSM_BLOCK), :] = y[:, :COLUMN_LANES]
        y_ref[pl.ds(2 * p + 1, rows, stride=SSM_BLOCK), :] = y[:, COLUMN_LANES:]


def _s5_block_diagonal(params):
    kmat, wre, wim, zre, zim, mu = params
    cg, nc = COLUMN_GROUPS, SSM_GROUPS // COLUMN_GROUPS
    rows = 2 * COLUMN_LANES
    grp = lambda n, width: (jnp.arange(n) // width) % cg

    def stack_rows(m, pairs_on_rows):
        minor = m.shape[-1]
        if pairs_on_rows:
            m6 = m.reshape(nc, cg, STEP_PAIRS, 2, SSM_GROUP, minor)
            return jnp.transpose(m6, (0, 2, 3, 1, 4, 5)).reshape(nc, STEP_PAIRS, rows, minor)
        m6 = m.reshape(nc, cg, 2, SSM_GROUP, STEP_PAIRS, minor // STEP_PAIRS)
        return jnp.transpose(m6, (0, 4, 2, 1, 3, 5)).reshape(nc, STEP_PAIRS, rows, minor // STEP_PAIRS)

    def spread(stacked, width):
        n_in = stacked.shape[-1]
        n_out = n_in * cg
        src = (jnp.arange(n_out) // (width * cg)) * width + jnp.arange(n_out) % width
        place = (jnp.arange(n_in)[:, None] == src[None, :]).astype(BF16)
        own = (grp(rows, SSM_GROUP)[:, None] == grp(n_out, width)[None, :]).astype(BF16)
        return jnp.einsum('cprk,kn->cprn', stacked, place, preferred_element_type=BF16) * own

    kk = spread(stack_rows(kmat[:, :2 * SSM_GROUP, :], False), SSM_GROUP)
    ww = spread(stack_rows(jnp.concatenate([wre, wim], axis=2), True), SSM_STATE)
    zz_t = spread(stack_rows(jnp.concatenate([zre, zim], axis=2), True), SSM_STATE)
    mu_c = jnp.transpose(mu.reshape(nc, cg, 2, SSM_STATE), (0, 2, 1, 3)).reshape(nc, 2, cg * SSM_STATE)
    return kk, ww, zz_t, mu_c


def _s5_core(u, bd, n_seq, seq_len, h0, fold_sequences):
    kk, ww, zz, mu_c = bd
    nc = kk.shape[0]
    nb = seq_len // SSM_BLOCK
    seqs = n_seq if fold_sequences else 1
    n_steps = n_seq // seqs
    rows_tok = seqs * seq_len
    rows = seqs * nb
    state = 2 * COLUMN_GROUPS * SSM_STATE
    col = pl.BlockSpec((rows_tok, COLUMN_LANES), lambda c, i: (i, c))
    per_c = lambda a: pl.BlockSpec((1,) + a.shape[1:], lambda c, i: (c,) + (0,) * (a.ndim - 1))
    in_specs = [col, per_c(kk), per_c(ww), per_c(zz), per_c(mu_c)]
    args = [u, kk, ww, zz, mu_c]
    if h0 is not None:
        in_specs.append(pl.BlockSpec((1, rows, state), lambda c, i: (c, i, 0)))
        args.append(h0)
    y, fin = pl.pallas_call(
        functools.partial(_s5_core_kernel, nb, seqs, h0 is not None),
        grid=(nc, n_steps),
        in_specs=in_specs,
        out_specs=[col, pl.BlockSpec((1, 1, seqs, state), lambda c, i: (c, i, 0, 0))],
        out_shape=[jax.ShapeDtypeStruct(u.shape, F32), jax.ShapeDtypeStruct((nc, n_steps, seqs, state), F32)],
        scratch_shapes=[pltpu.VMEM((state // LANES, rows, LANES), F32)],
        compiler_params=_cparams(2), name="s5_core",
    )(*args)
    return y, fin.reshape(nc, n_seq, state)


def _ffn_kernel(pre, final_norm, tiles_per_seq, seqs_per_tile, tm, *refs):
    refs = list(refs)
    x_ref = refs.pop(0)
    if pre == "glu":
        ys_ref, gmix_ref, dskip_ref, wglu_ref = refs[:4]
        refs = refs[4:]
    else:
        o_ref, wo_ref = refs[:2]
        refs = refs[2:]
    (hist_ref, gffn_ref, wup_ref, wgate_ref, convw_ref, convb_ref, wdown_ref) = refs[:7]
    refs = refs[7:]
    if final_norm:
        gfin_ref = refs.pop(0)
    out_ref, cstate_ref, carry = refs

    subs = FFN_SUBTILES if seqs_per_tile == 0 else 1
    sub = tm // subs
    seqs = max(1, seqs_per_tile)
    span = sub // seqs

    def mixer_epilogue(rows):
        x = x_ref[rows, :]
        if pre == "glu":
            u = _rms(x, gmix_ref[...])
            g = _gelu(ys_ref[rows, :] + dskip_ref[...] * u).astype(BF16)
            z = _dot(g, wglu_ref[...])
            x = x + z[:, :D_MODEL] * _sigmoid(z[:, D_MODEL:])
        else:
            x = x + _dot(o_ref[rows, :], wo_ref[...])
        return x, _rms(x, gffn_ref[...]).astype(BF16)

    staged = [mixer_epilogue(slice(i * sub, (i + 1) * sub)) for i in range(subs)]

    if seqs_per_tile == 0:
        @pl.when(pl.program_id(0) % tiles_per_seq == 0)
        def _():
            carry[...] = hist_ref[0]

    last = {}
    for i, (x, h) in enumerate(staged):
        for lo, hi in zip(FFN_CHUNK_EDGES[:-1], FFN_CHUNK_EDGES[1:]):
            cs = slice(lo, hi)
            row = lax.broadcasted_iota(jnp.int32, (sub, hi - lo), 0)
            a = _dot(h, wup_ref[:, cs])
            a1, a2 = pltpu.roll(a, 1, 0), pltpu.roll(a, 2, 0)
            for s in range(seqs):
                if seqs_per_tile == 0:
                    prev = carry[:, cs] if i == 0 else last[lo]
                else:
                    prev = hist_ref[s, :, cs]
                a1 = jnp.where(row == s * span, prev[1:2, :], a1)
                a2 = jnp.where(row == s * span, prev[0:1, :], jnp.where(row == s * span + 1, prev[1:2, :], a2))
                if seqs_per_tile:
                    cstate_ref[s, :, cs] = a[(s + 1) * span - (CONV_WIDTH - 1):(s + 1) * span, :]
            conv = a2 * convw_ref[0:1, cs] + a1 * convw_ref[1:2, cs] + a * convw_ref[2:3, cs] + convb_ref[:, cs]
            last[lo] = a[sub - (CONV_WIDTH - 1):, :]
            hh = (_gelu(conv) * _dot(h, wgate_ref[:, cs])).astype(BF16)
            x = x + _dot(hh, wdown_ref[cs, :])
        if final_norm:
            x = _rms(x, gfin_ref[...])
        out_ref[i * sub:(i + 1) * sub, :] = x
    if seqs_per_tile == 0:
        for lo, hi in zip(FFN_CHUNK_EDGES[:-1], FFN_CHUNK_EDGES[1:]):
            carry[:, lo:hi] = last[lo]
            cstate_ref[0, :, lo:hi] = last[lo]


def _ffn(x, pre, pre_args, hist, g_ffn, w_up, w_gate, conv_w, conv_b, w_down, g_final, seq_len, tm):
    t = x.shape[0]
    n_seq = t // seq_len
    tiles_per_seq = max(1, seq_len // tm)
    seqs_per_tile = 0 if tm < seq_len else tm // seq_len
    tile = pl.BlockSpec((tm, D_MODEL), lambda i: (i, 0))
    vec = _full((1, D_MODEL))
    const = lambda a: pl.BlockSpec(a.shape, lambda i: (0,) * a.ndim, pipeline_mode=pl.Buffered(1))
    per_seq = pl.BlockSpec((max(1, seqs_per_tile), CONV_WIDTH - 1, D_FF), lambda i: (i // tiles_per_seq, 0, 0))
    args, in_specs = [x], [tile]
    if pre == "glu":
        ys, g_mix, d_skip, w_glu = pre_args
        args += [ys, g_mix[None, :], d_skip[None, :], w_glu]
        in_specs += [tile, vec, vec, const(w_glu)]
    else:
        o, w_o = pre_args
        args += [o, w_o]
        in_specs += [tile, const(w_o)]
    args += [hist, g_ffn[None, :], w_up, w_gate, conv_w, conv_b[None, :], w_down]
    in_specs += [per_seq, vec, const(w_up), const(w_gate), _full((CONV_WIDTH, D_FF)), _full((1, D_FF)),
                 const(w_down)]
    if g_final is not None:
        args.append(g_final[None, :])
        in_specs.append(vec)
    return pl.pallas_call(
        functools.partial(_ffn_kernel, pre, g_final is not None, tiles_per_seq, seqs_per_tile, tm),
        grid=(t // tm,),
        in_specs=in_specs,
        out_specs=[tile, per_seq],
        out_shape=[jax.ShapeDtypeStruct((t, D_MODEL), F32),
                   jax.ShapeDtypeStruct((n_seq, CONV_WIDTH - 1, D_FF), F32)],
        scratch_shapes=[pltpu.VMEM((CONV_WIDTH - 1, D_FF), F32)],
        compiler_params=_cparams(1), name="ffn_" + pre,
    )(*args)


def _c_pieces(c):
    hi, mid, lo = _split3(c * LOG2E)
    return (hi.astype(F32) + pltpu.roll(mid.astype(F32), FOX_HEADS, 1)
            + pltpu.roll(lo.astype(F32), 2 * FOX_HEADS, 1)).astype(BF16)


def _key_bias_lanes(c3):
    lane = lax.broadcasted_iota(jnp.int32, c3.shape, 1)
    ones = ((lane >= _CQ_LANE) & (lane < _CQ_LANE + 3)).astype(F32)
    return (ones - c3.astype(F32)).astype(BF16)


def _qkv_kernel(tiles_per_seq, tm, transposed_v, x_ref, g_ref, wq_ref, wk_ref, wv_ref, wf_ref, bf_ref,
                k_ref, v_ref, lf_ref, qb_ref, kb_ref, vb_ref, c3_ref, kc_ref, carry):
    u = _rms(x_ref[...], g_ref[...]).astype(BF16)
    q = _dot(u, wq_ref[...]) * (FOX_HEAD_DIM ** -0.5 * LOG2E)
    k = _dot(u, wk_ref[...])
    v = _dot(u, wv_ref[...])
    k_ref[...] = k
    v_ref[...] = v
    kb_ref[...] = k.astype(BF16)
    if transposed_v:
        qb_ref[0, 0] = jnp.transpose(q).astype(BF16)
        vb_ref[0, 0] = jnp.transpose(v).astype(BF16)
    else:
        qb_ref[...] = q.astype(BF16)
        vb_ref[...] = v.astype(BF16)
    lane = lax.broadcasted_iota(jnp.int32, (tm, LANES), 1)
    logf = jnp.where(lane < FOX_HEADS, _log_sigmoid(_dot(u, wf_ref[...]) + bf_ref[...]), 0.0)
    lf_ref[...] = logf[:, :FOX_HEADS]

    @pl.when(pl.program_id(0) % tiles_per_seq == 0)
    def _():
        carry[...] = jnp.zeros_like(carry)

    c = _tri_cumsum(logf, tm) + carry[0:1, :]
    carry[...] = jnp.broadcast_to(c[tm - 1:tm, :], carry.shape)
    c3 = _c_pieces(c)
    if transposed_v:
        c3_ref[0, 0] = jnp.transpose(c3.astype(F32))
    else:
        c3_ref[...] = c3
    kc_ref[...] = _key_bias_lanes(c3)


def _qkv(x, g, wq, wk, wv, wf, bf, seq_len, tm, transposed_v):
    t = x.shape[0]
    tiles_per_seq = seq_len // tm
    tile = pl.BlockSpec((tm, D_MODEL), lambda i: (i, 0))
    narrow = pl.BlockSpec((tm, LANES), lambda i: (i, 0))
    const = lambda a: pl.BlockSpec(a.shape, lambda i: (0,) * a.ndim, pipeline_mode=pl.Buffered(1))
    wide_bf = jax.ShapeDtypeStruct((t, D_MODEL), BF16)
    if transposed_v:
        per_tile = lambda i: (i // tiles_per_seq, i % tiles_per_seq, 0, 0)
        qv_spec = pl.BlockSpec((1, 1, D_MODEL, tm), per_tile)
        qv_shape = jax.ShapeDtypeStruct((t // seq_len, tiles_per_seq, D_MODEL, tm), BF16)
        c3_spec = pl.BlockSpec((1, 1, LANES, tm), per_tile)
        c3_shape = jax.ShapeDtypeStruct((t // seq_len, tiles_per_seq, LANES, tm), F32)
    else:
        qv_spec, qv_shape = tile, wide_bf
        c3_spec, c3_shape = narrow, jax.ShapeDtypeStruct((t, LANES), BF16)
    return pl.pallas_call(
        functools.partial(_qkv_kernel, tiles_per_seq, tm, transposed_v),
        grid=(t // tm,),
        in_specs=[tile, _full((1, D_MODEL)), const(wq), const(wk), const(wv), const(wf), _full((1, LANES))],
        out_specs=[tile, tile, pl.BlockSpec((tm, FOX_HEADS), lambda i: (i, 0)), qv_spec, tile, qv_spec,
                   c3_spec, narrow],
        out_shape=[jax.ShapeDtypeStruct((t, D_MODEL), F32), jax.ShapeDtypeStruct((t, D_MODEL), F32),
                   jax.ShapeDtypeStruct((t, FOX_HEADS), F32),
                   qv_shape, wide_bf, qv_shape, c3_shape, jax.ShapeDtypeStruct((t, LANES), BF16)],
        scratch_shapes=[pltpu.VMEM((8, LANES), F32)],
        compiler_params=_cparams(1), name="fox_qkv",
    )(x, g[None, :], wq, wk, wv, wf, bf)


def _query_ext(q, c3q, head, lo_half):
    rows = q.shape[0]
    lane = lax.broadcasted_iota(jnp.int32, (rows, LANES), 1)
    own = (lane < FOX_HEAD_DIM) if lo_half else (lane >= FOX_HEAD_DIM)
    qm = jnp.where(own, q, jnp.zeros_like(q))
    pick = lambda j: jnp.sum(jnp.where(lane == head + j * FOX_HEADS, c3q, 0.0), axis=1, keepdims=True)
    onehot = (lane == head) | (lane == head + FOX_HEADS) | (lane == head + 2 * FOX_HEADS)
    qc = jnp.where(lane == _CQ_LANE, pick(0),
                   jnp.where(lane == _CQ_LANE + 1, pick(1),
                             jnp.where(lane == _CQ_LANE + 2, pick(2), jnp.where(onehot, 1.0, 0.0))))
    return jnp.concatenate([qm, qc.astype(BF16)], axis=1)


def _sublane_allreduce(x, op):
    for sh in (4, 2, 1):
        x = op(x, pltpu.roll(x, sh, 0))
    return x


def _fold_rows(x, op):
    parts = [x[i * SUBLANES:(i + 1) * SUBLANES] for i in range(x.shape[0] // SUBLANES)]
    while len(parts) > 1:
        parts = [op(parts[i], parts[i + 1]) for i in range(0, len(parts), 2)]
    return parts[0]


KEY_CHUNK = 256
ATTN_PAIRS_PER_STEP = 4
ONES_ROWS = 16


def _attn_prompt_kernel(t, n_items, qi_tab, j_tab, q_ref, c3_ref, k_ref, kc_ref, vt_ref, o_ref,
                        qt, e_buf, cm_buf, acc):
    n_heads = acc.shape[0]
    first_head = pl.program_id(1) * n_heads
    pair_lanes = lambda a: slice((a // 2) * PAIR_LANES, (a // 2 + 1) * PAIR_LANES)
    big = -NEG_INF
    ck = KEY_CHUNK
    n_ck = t // ck
    for a in range(n_heads):
        acc[a] = jnp.zeros(acc.shape[1:], F32)
    ones_rows = jnp.ones((ONES_ROWS, ck), BF16)

    def load_q(qi):
        row = lax.broadcasted_iota(jnp.int32, (PAIR_LANES, t), 0)
        for a in range(n_heads):
            head = first_head + a
            own = (row < FOX_HEAD_DIM) if a % 2 == 0 else (row >= FOX_HEAD_DIM)
            q_pair = q_ref[0, qi, pair_lanes(a), :]
            piece = lambda j: c3_ref[0, qi, pl.ds(head + j * FOX_HEADS, 1), :]
            onehot = (row == head) | (row == head + FOX_HEADS) | (row == head + 2 * FOX_HEADS)
            bias = jnp.where(row == _CQ_LANE, piece(0),
                             jnp.where(row == _CQ_LANE + 1, piece(1),
                                       jnp.where(row == _CQ_LANE + 2, piece(2), jnp.where(onehot, 1.0, 0.0))))
            qt[a] = jnp.concatenate([jnp.where(own, q_pair, jnp.zeros_like(q_pair)), bias.astype(BF16)], axis=0)

    def stage_a(w, slot, diagonal):
        start = pl.multiple_of(j_tab[w] * t, t)
        width = 2 * LANES
        hk = ck
        for a in range(n_heads):
            for i in range(t // hk):
                rows = pl.ds(start + i * hk, hk)
                kext = jnp.concatenate([k_ref[rows, pair_lanes(a)], kc_ref[rows, :]], axis=1)
                for q0 in range(0, t, width):
                    if diagonal and i * hk > q0 + width - 1:
                        e_buf[slot, a, i * hk:(i + 1) * hk, q0:q0 + width] = jnp.zeros((hk, width), BF16)
                        cm_buf[slot, a, i, :, q0:q0 + width] = jnp.full((SUBLANES, width), NEG_INF, F32)
                        continue
                    s = _dot(kext, qt[a, :, q0:q0 + width])
                    if diagonal and (i + 1) * hk - 1 > q0:
                        r = lax.broadcasted_iota(jnp.int32, (hk, width), 0) + i * hk
                        c = lax.broadcasted_iota(jnp.int32, (hk, width), 1) + q0
                        s = jnp.where(r <= c, s, NEG_INF)
                    cm = _sublane_allreduce(_fold_rows(s, jnp.maximum), jnp.maximum)
                    e_buf[slot, a, i * hk:(i + 1) * hk, q0:q0 + width] = jnp.exp2(
                        s - jnp.tile(cm, (hk // SUBLANES, 1))).astype(BF16)
                    cm_buf[slot, a, i, :, q0:q0 + width] = cm

    def stage_c(w, ms, slot):
        m_floor = jnp.where(j_tab[w] == 0, NEG_INF, big)
        new_ms = []
        for a in range(n_heads):
            m_old = jnp.minimum(ms[a], m_floor)
            cms = [cm_buf[slot, a, i] for i in range(n_ck)]
            m_new = functools.reduce(jnp.maximum, cms, m_old)
            n_rep = acc.shape[1] // SUBLANES
            total = acc[a] * jnp.tile(jnp.exp2(m_old - m_new), (n_rep, 1))
            for i in range(n_ck):
                vt = jnp.concatenate([vt_ref[0, j_tab[w], pair_lanes(a), i * ck:(i + 1) * ck], ones_rows], axis=0)
                pv = _dot(vt, e_buf[slot, a, i * ck:(i + 1) * ck, :])
                total = total + pv * jnp.tile(jnp.exp2(cms[i] - m_new), (n_rep, 1))
            acc[a] = total
            new_ms.append(m_new)
        return tuple(new_ms)

    def finalize(w):
        outs = []
        for a in range(n_heads):
            full = acc[a]
            denom = jnp.tile(full[PAIR_LANES:PAIR_LANES + SUBLANES], (PAIR_LANES // SUBLANES, 1))
            outs.append(full[:PAIR_LANES] / denom)
        row = lax.broadcasted_iota(jnp.int32, (PAIR_LANES, t), 0)
        rows = pl.ds(pl.multiple_of(qi_tab[w] * t, t), t)
        for a in range(0, n_heads, 2):
            o = jnp.transpose(jnp.where(row < FOX_HEAD_DIM, outs[a], outs[a + 1]))
            o_ref[rows, pair_lanes(a)] = o.astype(o_ref.dtype)

    load_q(0)
    stage_a(0, 0, True)

    def item(w, ms, slot):
        nxt = jnp.minimum(w + 1, n_items - 1)

        @pl.when(j_tab[nxt] == 0)
        def _():
            load_q(qi_tab[nxt])

        def stages(diagonal, ms):
            stage_a(nxt, 1 - slot, diagonal)
            ms = stage_c(w, ms, slot)
            return ms

        ms = lax.cond(j_tab[nxt] == qi_tab[nxt], functools.partial(stages, True),
                      functools.partial(stages, False), ms)

        @pl.when(j_tab[w] == qi_tab[w])
        def _():
            finalize(w)

        return ms

    def body(i, ms):
        return item(2 * i + 1, item(2 * i, ms, 0), 1)

    ms = lax.fori_loop(0, n_items // 2, body, tuple(jnp.full((SUBLANES, t), NEG_INF, F32) for _ in range(n_heads)))
    if n_items % 2:
        item(n_items - 1, ms, 0)


def _attn_prompt(qt, kb, kc, vt, c3t, n_seq, seq_len, t):
    total = kb.shape[0]
    nt = seq_len // t
    items = [(qi, j) for qi in range(nt) for j in range(qi + 1)]
    qi_tab = jnp.asarray([qi for qi, _ in items], jnp.int32)
    j_tab = jnp.asarray([j for _, j in items], jnp.int32)
    nh = 2 * ATTN_PAIRS_PER_STEP
    width = ATTN_PAIRS_PER_STEP * PAIR_LANES
    once = pl.Buffered(1)
    seq_in = pl.BlockSpec((seq_len, width), lambda b, p, *_: (b, p), pipeline_mode=once)
    seq_bias = pl.BlockSpec((seq_len, LANES), lambda b, p, *_: (b, 0), pipeline_mode=once)
    vt_in = pl.BlockSpec((1, nt, width, t), lambda b, p, *_: (b, 0, p, 0), pipeline_mode=once)
    c3t_in = pl.BlockSpec((1, nt, LANES, t), lambda b, p, *_: (b, 0, 0, 0), pipeline_mode=once)
    return pl.pallas_call(
        functools.partial(_attn_prompt_kernel, t, len(items)),
        grid_spec=pltpu.PrefetchScalarGridSpec(
            num_scalar_prefetch=2,
            grid=(n_seq, HEAD_PAIRS // ATTN_PAIRS_PER_STEP),
            in_specs=[vt_in, c3t_in, seq_in, seq_bias, vt_in],
            out_specs=pl.BlockSpec((seq_len, width), lambda b, p, *_: (b, p), pipeline_mode=once),
            scratch_shapes=[pltpu.VMEM((nh, 2 * LANES, t), BF16), pltpu.VMEM((2, nh, t, t), BF16),
                            pltpu.VMEM((2, nh, t // KEY_CHUNK, SUBLANES, t), F32),
                            pltpu.VMEM((nh, PAIR_LANES + ONES_ROWS, t), F32)]),
        out_shape=jax.ShapeDtypeStruct((total, D_MODEL), BF16),
        compiler_params=_cparams(2), name="fox_attn_prompt",
    )(qi_tab, j_tab, qt, c3t, kb, kc, vt)


def _attn_sample_kernel(n_cache_tiles, tk, t_new, q_ref, c3q_ref, kc_ref, vc_ref, lfc_ref,
                        kn_ref, vn_ref, lfn_ref, o_ref, qt, m_scr, acc_scr, carry, s_buf, p_buf, al_buf):
    j = pl.program_id(1)
    rows = 2 * t_new

    @pl.when(j == 0)
    def _():
        c3q = c3q_ref[...].astype(F32)
        for p in range(HEAD_PAIRS):
            q = q_ref[:, p * PAIR_LANES:(p + 1) * PAIR_LANES]
            qe = jnp.concatenate([_query_ext(q, c3q, 2 * p, True), _query_ext(q, c3q, 2 * p + 1, False)], axis=0)
            qt[p] = jnp.transpose(qe.astype(F32)).astype(BF16)
        m_scr[...] = jnp.full(m_scr.shape, NEG_INF, F32)
        acc_scr[...] = jnp.zeros(acc_scr.shape, F32)
        carry[...] = jnp.zeros_like(carry)

    def attend(k_pair, v_pair, n_keys, c3k, masked):
        kc = _key_bias_lanes(c3k)
        for p in range(HEAD_PAIRS):
            s = _dot(jnp.concatenate([k_pair(p), kc], axis=1), qt[p])
            if masked:
                r = lax.broadcasted_iota(jnp.int32, (n_keys, rows), 0)
                c = lax.broadcasted_iota(jnp.int32, (n_keys, rows), 1) & (t_new - 1)
                s = jnp.where(r <= c, s, NEG_INF)
            s_buf[p, :n_keys] = s
        for p in range(HEAD_PAIRS):
            s = s_buf[p, :n_keys]
            m_old = m_scr[p]
            m_new = jnp.maximum(m_old, _sublane_allreduce(_fold_rows(s, jnp.maximum), jnp.maximum))
            p_buf[p, :n_keys] = jnp.exp2(s - jnp.tile(m_new, (n_keys // SUBLANES, 1))).astype(BF16)
            alpha = jnp.exp2(m_old - m_new)
            al_buf[p] = jnp.transpose(jnp.tile(alpha, (rows // SUBLANES, 1)))
            m_scr[p] = m_new
        ones = jnp.ones((n_keys, LANES), BF16)
        for p in range(HEAD_PAIRS):
            pv = lax.dot_general(p_buf[p, :n_keys], jnp.concatenate([v_pair(p), ones], axis=1),
                                 (((0,), (0,)), ((), ())), preferred_element_type=F32)
            acc_scr[p] = acc_scr[p] * jnp.tile(al_buf[p], (1, 2)) + pv

    def cache_pair(ref):
        return lambda p: ref[0, :, p * PAIR_LANES:(p + 1) * PAIR_LANES].astype(BF16)

    def new_pair(ref):
        return lambda p: ref[:, p * PAIR_LANES:(p + 1) * PAIR_LANES]

    @pl.when(j < n_cache_tiles)
    def _():
        c = _tri_cumsum(lfc_ref[0], tk) + carry[0:1, :]
        carry[...] = jnp.broadcast_to(c[tk - 1:tk, :], carry.shape)
        attend(cache_pair(kc_ref), cache_pair(vc_ref), tk, _c_pieces(c), False)

    @pl.when(j == n_cache_tiles)
    def _():
        c = _tri_cumsum(lfn_ref[...], t_new) + carry[0:1, :]
        attend(new_pair(kn_ref), new_pair(vn_ref), t_new, _c_pieces(c), True)
        lane = lax.broadcasted_iota(jnp.int32, (t_new, LANES), 1)
        for p in range(HEAD_PAIRS):
            acc = acc_scr[p]
            oa = acc[:t_new, :LANES] / acc[:t_new, LANES:]
            ob = acc[t_new:, :LANES] / acc[t_new:, LANES:]
            o_ref[:, p * PAIR_LANES:(p + 1) * PAIR_LANES] = jnp.where(lane < FOX_HEAD_DIM, oa, ob).astype(o_ref.dtype)


def _attn_sample(qb, kb, vb, c3, logf, cache_k, cache_v, cache_logf, tk):
    n_seq, past, _ = cache_k.shape
    t_new = qb.shape[0] // n_seq
    nct = past // tk
    pad = lambda a: jnp.pad(a, ((0, 0),) * (a.ndim - 1) + ((0, LANES - FOX_HEADS),))
    new_tile = pl.BlockSpec((t_new, D_MODEL), lambda b, j: (b, 0))
    new_c3 = pl.BlockSpec((t_new, LANES), lambda b, j: (b, 0))
    cache_tile = pl.BlockSpec((1, tk, D_MODEL), lambda b, j: (b, jnp.minimum(j, nct - 1), 0))
    lf_tile = pl.BlockSpec((1, tk, LANES), lambda b, j: (b, jnp.minimum(j, nct - 1), 0))
    return pl.pallas_call(
        functools.partial(_attn_sample_kernel, nct, tk, t_new),
        grid=(n_seq, nct + 1),
        in_specs=[new_tile, new_c3, cache_tile, cache_tile, lf_tile, new_tile, new_tile, new_c3],
        out_specs=new_tile,
        out_shape=jax.ShapeDtypeStruct((n_seq * t_new, D_MODEL), BF16),
        scratch_shapes=[pltpu.VMEM((HEAD_PAIRS, 2 * LANES, 2 * t_new), BF16),
                        pltpu.VMEM((HEAD_PAIRS, SUBLANES, 2 * t_new), F32),
                        pltpu.VMEM((HEAD_PAIRS, 2 * t_new, 2 * LANES), F32),
                        pltpu.VMEM((8, LANES), F32),
                        pltpu.VMEM((HEAD_PAIRS, tk, 2 * t_new), F32),
                        pltpu.VMEM((HEAD_PAIRS, tk, 2 * t_new), BF16),
                        pltpu.VMEM((HEAD_PAIRS, 2 * t_new, 2 * t_new), F32)],
        compiler_params=_cparams(2), name="fox_attn_sample",
    )(qb, c3, cache_k, cache_v, pad(cache_logf), kb, vb, pad(logf))


def _trunk(x, n_seq, seq_len, ssm_h0, fox_cache, conv_hist, w, tm, tq):
    t = n_seq * seq_len
    nb = seq_len // SSM_BLOCK
    nc = SSM_GROUPS // COLUMN_GROUPS
    half = COLUMN_GROUPS * SSM_STATE
    x = x.reshape(t, D_MODEL)

    u0 = _norm(x, w["norm_mix"][0], tm)
    if ssm_h0 is None:
        h0 = None
    else:
        per_c = lambda h: jnp.swapaxes(h.reshape(n_seq, nc, half), 0, 1)
        h0c = jnp.concatenate([per_c(ssm_h0[0]), per_c(ssm_h0[1])], axis=2)
        h0 = jnp.zeros((nc, n_seq, nb, 2 * half), F32).at[:, :, 0, :].set(h0c).reshape(nc, n_seq * nb, 2 * half)
    ys, fin = _s5_core(u0, w["s5"], n_seq, seq_len, h0, fold_sequences=ssm_h0 is not None)
    per_g = lambda f: jnp.swapaxes(f.reshape(nc, n_seq, COLUMN_GROUPS, SSM_STATE), 0, 1).reshape(
        n_seq, SSM_GROUPS, SSM_STATE)
    fre, fim = per_g(fin[:, :, :half]), per_g(fin[:, :, half:])
    x, conv0 = _ffn(x, "glu", (ys, w["norm_mix"][0], w["ssm_d"], w["w_glu"]), conv_hist[0],
                    w["norm_ffn"][0], w["w_up"][0], w["w_gate"][0], w["conv_w"][0], w["conv_b"][0],
                    w["w_down"][0], None, seq_len, tm)

    k, v, logf, qb, kb, vb, c3, kc = _qkv(x, w["norm_mix"][1], w["wq"], w["wk"], w["wv"], w["wf"], w["bf"],
                                          seq_len, tq, transposed_v=fox_cache is None)
    if fox_cache is None:
        o = _attn_prompt(qb, kb, kc, vb, c3, n_seq, seq_len, tq)
    else:
        o = _attn_sample(qb, kb, vb, c3, logf, *fox_cache, tk=min(512, fox_cache[0].shape[1]))
    y, conv1 = _ffn(x, "oproj", (o, w["w_o"]), conv_hist[1],
                    w["norm_ffn"][1], w["w_up"][1], w["w_gate"][1], w["conv_w"][1], w["conv_b"][1],
                    w["w_down"][1], w["norm_final"], seq_len, tm)

    shape4 = (1, n_seq, seq_len, FOX_HEADS, FOX_HEAD_DIM)
    return (y.reshape(n_seq, seq_len, D_MODEL),
            fre[None], fim[None],
            k.reshape(shape4), v.reshape(shape4), logf.reshape(1, n_seq, seq_len, FOX_HEADS),
            jnp.stack([conv0, conv1]))


def kernel(x_prompt, x_sample, state_ssm_re, state_ssm_im, cache_fox_k, cache_fox_v, cache_fox_logf, state_ffn_conv, norm_mix, norm_ffn, norm_final, ssm_a_re, ssm_a_im, ssm_log_step, ssm_b_re, ssm_b_im, ssm_c_re, ssm_c_im, ssm_d, ssm_w_glu, fox_w_qkvf, fox_b_f, fox_w_o, ffn_w_up, ffn_w_gate, ffn_conv_w, ffn_conv_b, ffn_w_down):
    n_p, l_p, _ = x_prompt.shape
    n_s, l_s, _ = x_sample.shape
    past = cache_fox_k.shape[2]
    wqkvf = fox_w_qkvf[0]
    w = {
        "norm_mix": norm_mix, "norm_ffn": norm_ffn, "norm_final": norm_final,
        "s5": _s5_block_diagonal(_s5_params(ssm_a_re[0], ssm_a_im[0], ssm_log_step[0], ssm_b_re[0],
                                            ssm_b_im[0], ssm_c_re[0], ssm_c_im[0])),
        "ssm_d": ssm_d[0], "w_glu": ssm_w_glu[0].astype(BF16),
        "wq": wqkvf[:, :D_MODEL].astype(BF16), "wk": wqkvf[:, D_MODEL:2 * D_MODEL].astype(BF16),
        "wv": wqkvf[:, 2 * D_MODEL:3 * D_MODEL].astype(BF16),
        "wf": jnp.pad(wqkvf[:, 3 * D_MODEL:], ((0, 0), (0, LANES - FOX_HEADS))).astype(BF16),
        "bf": jnp.pad(fox_b_f[0], (0, LANES - FOX_HEADS))[None, :],
        "w_o": fox_w_o[0].astype(BF16),
        "w_up": ffn_w_up.astype(BF16), "w_gate": ffn_w_gate.astype(BF16), "w_down": ffn_w_down.astype(BF16),
        "conv_w": ffn_conv_w, "conv_b": ffn_conv_b,
    }
    zero_hist = jnp.zeros((2, n_p, CONV_WIDTH - 1, D_FF), F32)
    outs_p = _trunk(x_prompt, n_p, l_p, None, None, zero_hist, w, tm=512, tq=512)
    cache = (cache_fox_k[0].reshape(n_s, past, D_MODEL), cache_fox_v[0].reshape(n_s, past, D_MODEL),
             cache_fox_logf[0])
    outs_s = _trunk(x_sample, n_s, l_s, (state_ssm_re[0], state_ssm_im[0]), cache, state_ffn_conv, w,
                    tm=4 * l_s, tq=l_s)
    return (outs_p[0], outs_s[0]) + outs_p[1:] + outs_s[1:]
```

```python
import functools
import math

import jax
import jax.numpy as jnp
from jax import lax
from jax.experimental import pallas as pl
from jax.experimental.pallas import tpu as pltpu

D_MODEL = 1024
SSM_GROUP = 16
SSM_GROUPS = D_MODEL // SSM_GROUP
SSM_STATE = 64
SSM_BLOCK = 16
SSM_BLOCK_LOG2 = 4
BLOCK_LANES = SSM_BLOCK * SSM_GROUP
COLUMN_LANES = 128
COLUMN_GROUPS = COLUMN_LANES // SSM_GROUP
STEP_PAIRS = SSM_BLOCK // 2
FOX_HEADS = 16
FOX_HEAD_DIM = 64
HEAD_PAIRS = FOX_HEADS // 2
PAIR_LANES = 2 * FOX_HEAD_DIM
D_FF = 2816
CONV_WIDTH = 3
FFN_SUBTILES = 1
FFN_CHUNK_EDGES = (0, 1536, D_FF)
NORM_EPS = 1e-6
NEG_INF = -1e30
LOG2E = 1.4426950408889634
LANES = 128
SUBLANES = 8
_CQ_LANE = 3 * FOX_HEADS
VMEM_LIMIT = 56 * 1024 * 1024

BF16 = jnp.bfloat16
F32 = jnp.float32


def _cparams(n_axes, vmem=VMEM_LIMIT):
    return pltpu.CompilerParams(dimension_semantics=("arbitrary",) * n_axes, vmem_limit_bytes=vmem)


def _full(shape):
    nd = len(shape)
    return pl.BlockSpec(shape, lambda *_: (0,) * nd)


def _rms(x, g):
    ms = jnp.mean(x * x, axis=-1, keepdims=True)
    return x * lax.rsqrt(ms + NORM_EPS) * g


def _gelu(x):
    c = math.sqrt(2.0 / math.pi)
    return 0.5 * x * (1.0 + jnp.tanh(c * (x + 0.044715 * (x * x * x))))


def _sigmoid(x):
    return 1.0 / (1.0 + jnp.exp(-x))


def _log_sigmoid(x):
    return jnp.minimum(x, 0.0) - jnp.log1p(jnp.exp(-jnp.abs(x)))


def _split3(x):
    hi = x.astype(BF16)
    r1 = x - hi.astype(F32)
    mid = r1.astype(BF16)
    lo = (r1 - mid.astype(F32)).astype(BF16)
    return hi, mid, lo


def _dot(a, b):
    return jnp.dot(a, b, preferred_element_type=F32)


def _dot_nt(a, b):
    return lax.dot_general(a, b, (((1,), (1,)), ((), ())), preferred_element_type=F32)


def _dot_nt_f32(a, b):
    a0, a1, a2 = _split3(a)
    b0, b1, b2 = _split3(b)
    return (_dot_nt(a0, b0) + (_dot_nt(a0, b1) + _dot_nt(a1, b0))
            + (_dot_nt(a0, b2) + _dot_nt(a1, b1) + _dot_nt(a2, b0)))


def _tri_cumsum(x, n):
    r = lax.broadcasted_iota(jnp.int32, (n, n), 0)
    c = lax.broadcasted_iota(jnp.int32, (n, n), 1)
    tri = jnp.where(c <= r, 1.0, 0.0).astype(BF16)
    hi, mid, lo = _split3(x)
    return _dot(tri, hi) + _dot(tri, mid) + _dot(tri, lo)


def _s5_param_kernel(are_r, aim_r, ls_ref, btre, btim, ctre, ctim,
                     k_ref, wre_ref, wim_ref, zre_ref, zim_ref, mu_ref):
    dt = jnp.exp(ls_ref[0])
    lre, lim = are_r[0], aim_r[0]
    mag = jnp.exp(lre * dt)
    abr = mag * jnp.cos(lim * dt)
    abi = mag * jnp.sin(lim * dt)
    den = lre * lre + lim * lim
    nr = abr - 1.0
    zre = (nr * lre + abi * lim) / den
    zim = (abi * lre - nr * lim) / den

    def cpow(e):
        m = jnp.exp(lre * dt * e)
        a = lim * dt * e
        return m * jnp.cos(a), m * jnp.sin(a)

    def per_step(e):
        step = lax.broadcasted_iota(jnp.int32, (SSM_BLOCK, SSM_STATE), 0).astype(F32)
        pr, pi = cpow(e(step))
        rep = lambda t: jnp.broadcast_to(t[:, None, :], (SSM_BLOCK, SSM_GROUP, SSM_STATE)).reshape(
            BLOCK_LANES, SSM_STATE)
        return rep(pr), rep(pi)

    bre, bim = btre[0], btim[0]
    bbre = zre * bre - zim * bim
    bbim = zre * bim + zim * bre
    pwr, pwi = per_step(lambda s: (SSM_BLOCK - 1.0) - s)
    wre_ref[0] = (pwr * bbre - pwi * bbim).astype(BF16)
    wim_ref[0] = (pwr * bbim + pwi * bbre).astype(BF16)
    pbr, pbi = per_step(lambda s: -s)
    bxre = pbr * bbre - pbi * bbim
    bxim = pbr * bbim + pbi * bbre

    cre, cim = ctre[0], ctim[0]
    pcr, pci = per_step(lambda t: t)
    cxre = pcr * cre - pci * cim
    cxim = pcr * cim + pci * cre
    pzr, pzi = per_step(lambda t: t + 1.0)
    zre_ref[0] = (pzr * cre - pzi * cim).astype(BF16)
    zim_ref[0] = (-(pzr * cim + pzi * cre)).astype(BF16)

    kfull = _dot_nt_f32(bxre, cxre) - _dot_nt_f32(bxim, cxim)
    rs = lax.broadcasted_iota(jnp.int32, (BLOCK_LANES, BLOCK_LANES), 0) >> SSM_BLOCK_LOG2
    ct = lax.broadcasted_iota(jnp.int32, (BLOCK_LANES, BLOCK_LANES), 1) >> SSM_BLOCK_LOG2
    k_ref[0] = jnp.where(ct >= rs, kfull, 0.0).astype(BF16)

    mur, mui = cpow(float(SSM_BLOCK))
    mu_ref[0, 0:1, :] = mur
    mu_ref[0, 1:2, :] = mui


def _s5_params(a_re, a_im, log_step, b_re, b_im, c_re, c_im):
    g, p = a_re.shape
    bt = jnp.tile(jnp.swapaxes(b_re, 1, 2), (1, SSM_BLOCK, 1)), jnp.tile(jnp.swapaxes(b_im, 1, 2), (1, SSM_BLOCK, 1))
    ct = jnp.tile(c_re, (1, SSM_BLOCK, 1)), jnp.tile(c_im, (1, SSM_BLOCK, 1))
    row = pl.BlockSpec((1, 1, p), lambda i: (i, 0, 0))
    one = pl.BlockSpec((1, 1, 1), lambda i: (i, 0, 0))
    tall = pl.BlockSpec((1, BLOCK_LANES, p), lambda i: (i, 0, 0))
    return pl.pallas_call(
        _s5_param_kernel,
        grid=(g,),
        in_specs=[row, row, one, tall, tall, tall, tall],
        out_specs=[pl.BlockSpec((1, BLOCK_LANES, BLOCK_LANES), lambda i: (i, 0, 0)),
                   tall, tall, tall, tall, pl.BlockSpec((1, 2, p), lambda i: (i, 0, 0))],
        out_shape=[jax.ShapeDtypeStruct((g, BLOCK_LANES, BLOCK_LANES), BF16)]
        + [jax.ShapeDtypeStruct((g, BLOCK_LANES, p), BF16)] * 4
        + [jax.ShapeDtypeStruct((g, 2, p), F32)],
        compiler_params=_cparams(1),
        name="s5_params",
    )(a_re[:, None, :], a_im[:, None, :], log_step[:, None, None], bt[0], bt[1], ct[0], ct[1])


def _norm_kernel(x_ref, g_ref, u_ref):
    u_ref[...] = _rms(x_ref[...], g_ref[...]).astype(u_ref.dtype)


def _norm(x, g, tm):
    t = x.shape[0]
    tile = pl.BlockSpec((tm, D_MODEL), lambda i: (i, 0))
    return pl.pallas_call(
        _norm_kernel, grid=(t // tm,),
        in_specs=[tile, _full((1, D_MODEL))], out_specs=tile,
        out_shape=jax.ShapeDtypeStruct((t, D_MODEL), F32),
        compiler_params=_cparams(1), name="s5_prenorm",
    )(x, g[None, :])


def _s5_core_kernel(nb, n_seq, has_h0, *refs):
    if has_h0:
        u_ref, kk_ref, ww_ref, zz_ref, mu_ref, h0_ref, y_ref, f_ref, s_scr = refs
    else:
        u_ref, kk_ref, ww_ref, zz_ref, mu_ref, y_ref, f_ref, s_scr = refs
    rows = nb * n_seq
    half = COLUMN_GROUPS * SSM_STATE
    xp = [jnp.concatenate([u_ref[pl.ds(2 * p, rows, stride=SSM_BLOCK), :],
                           u_ref[pl.ds(2 * p + 1, rows, stride=SSM_BLOCK), :]], axis=1).astype(BF16)
          for p in range(STEP_PAIRS)]
    s = _dot(xp[0], ww_ref[0, 0])
    for p in range(1, STEP_PAIRS):
        s = s + _dot(xp[p], ww_ref[0, p])
    sre, sim = s[:, :half], s[:, half:]
    mu = mu_ref[0]
    mre, mim = mu[0:1, :], mu[1:2, :]
    blk = lax.broadcasted_iota(jnp.int32, (rows, half), 0) & (nb - 1)
    if has_h0:
        h0 = h0_ref[0]
        h0re, h0im = h0[:, :half], h0[:, half:]
        sre = sre + (mre * h0re - mim * h0im)
        sim = sim + (mre * h0im + mim * h0re)
    d = 1
    while d < nb:
        keep = blk >= d
        shr = jnp.where(keep, pltpu.roll(sre, d, 0), 0.0)
        shi = jnp.where(keep, pltpu.roll(sim, d, 0), 0.0)
        sre, sim = sre + (mre * shr - mim * shi), sim + (mre * shi + mim * shr)
        mre, mim = mre * mre - mim * mim, 2.0 * mre * mim
        d *= 2
    n_chunks = half // LANES
    for i in range(n_chunks):
        s_scr[i] = sre[:, i * LANES:(i + 1) * LANES]
        s_scr[n_chunks + i] = sim[:, i * LANES:(i + 1) * LANES]
    f_ref[0, 0] = jnp.concatenate([s_scr[i, pl.ds(nb - 1, n_seq, stride=nb), :] for i in range(2 * n_chunks)],
                                  axis=1)
    first = blk == 0
    inre = jnp.where(first, 0.0, pltpu.roll(sre, 1, 0))
    inim = jnp.where(first, 0.0, pltpu.roll(sim, 1, 0))
    if has_h0:
        inre = inre + h0re
        inim = inim + h0im
    s_in = jnp.concatenate([inre, inim], axis=1).astype(BF16)
    for p in range(STEP_PAIRS):
        y = _dot_nt(s_in, zz_ref[0, p])
        for q in range(p + 1):
            y = y + _dot(xp[q], kk_ref[0, p - q])
        y_ref[pl.ds(2 * p, rows, stride=SSM_BLOCK), :] = y[:, :COLUMN_LANES]
        y_ref[pl.ds(2 * p + 1, rows, stride=SSM_BLOCK), :] = y[:, COLUMN_LANES:]


def _s5_block_diagonal(params):
    kmat, wre, wim, zre, zim, mu = params
    cg, nc = COLUMN_GROUPS, SSM_GROUPS // COLUMN_GROUPS
    rows = 2 * COLUMN_LANES
    grp = lambda n, width: (jnp.arange(n) // width) % cg

    def stack_rows(m, pairs_on_rows):
        minor = m.shape[-1]
        if pairs_on_rows:
            m6 = m.reshape(nc, cg, STEP_PAIRS, 2, SSM_GROUP, minor)
            return jnp.transpose(m6, (0, 2, 3, 1, 4, 5)).reshape(nc, STEP_PAIRS, rows, minor)
        m6 = m.reshape(nc, cg, 2, SSM_GROUP, STEP_PAIRS, minor // STEP_PAIRS)
        return jnp.transpose(m6, (0, 4, 2, 1, 3, 5)).reshape(nc, STEP_PAIRS, rows, minor // STEP_PAIRS)

    def spread(stacked, width):
        n_in = stacked.shape[-1]
        n_out = n_in * cg
        src = (jnp.arange(n_out) // (width * cg)) * width + jnp.arange(n_out) % width
        place = (jnp.arange(n_in)[:, None] == src[None, :]).astype(BF16)
        own = (grp(rows, SSM_GROUP)[:, None] == grp(n_out, width)[None, :]).astype(BF16)
        return jnp.einsum('cprk,kn->cprn', stacked, place, preferred_element_type=BF16) * own

    kk = spread(stack_rows(kmat[:, :2 * SSM_GROUP, :], False), SSM_GROUP)
    ww = spread(stack_rows(jnp.concatenate([wre, wim], axis=2), True), SSM_STATE)
    zz_t = spread(stack_rows(jnp.concatenate([zre, zim], axis=2), True), SSM_STATE)
    mu_c = jnp.transpose(mu.reshape(nc, cg, 2, SSM_STATE), (0, 2, 1, 3)).reshape(nc, 2, cg * SSM_STATE)
    return kk, ww, zz_t, mu_c


def _s5_core(u, bd, n_seq, seq_len, h0, fold_sequences):
    kk, ww, zz, mu_c = bd
    nc = kk.shape[0]
    nb = seq_len // SSM_BLOCK
    seqs = n_seq if fold_sequences else 1
    n_steps = n_seq // seqs
    rows_tok = seqs * seq_len
    rows = seqs * nb
    state = 2 * COLUMN_GROUPS * SSM_STATE
    col = pl.BlockSpec((rows_tok, COLUMN_LANES), lambda c, i: (i, c))
    per_c = lambda a: pl.BlockSpec((1,) + a.shape[1:], lambda c, i: (c,) + (0,) * (a.ndim - 1))
    in_specs = [col, per_c(kk), per_c(ww), per_c(zz), per_c(mu_c)]
    args = [u, kk, ww, zz, mu_c]
    if h0 is not None:
        in_specs.append(pl.BlockSpec((1, rows, state), lambda c, i: (c, i, 0)))
        args.append(h0)
    y, fin = pl.pallas_call(
        functools.partial(_s5_core_kernel, nb, seqs, h0 is not None),
        grid=(nc, n_steps),
        in_specs=in_specs,
        out_specs=[col, pl.BlockSpec((1, 1, seqs, state), lambda c, i: (c, i, 0, 0))],
        out_shape=[jax.ShapeDtypeStruct(u.shape, F32), jax.ShapeDtypeStruct((nc, n_steps, seqs, state), F32)],
        scratch_shapes=[pltpu.VMEM((state // LANES, rows, LANES), F32)],
        compiler_params=_cparams(2), name="s5_core",
    )(*args)
    return y, fin.reshape(nc, n_seq, state)


def _ffn_kernel(pre, final_norm, tiles_per_seq, seqs_per_tile, tm, *refs):
    refs = list(refs)
    x_ref = refs.pop(0)
    if pre == "glu":
        ys_ref, gmix_ref, dskip_ref, wglu_ref = refs[:4]
        refs = refs[4:]
    else:
        o_ref, wo_ref = refs[:2]
        refs = refs[2:]
    (hist_ref, gffn_ref, wup_ref, wgate_ref, convw_ref, convb_ref, wdown_ref) = refs[:7]
    refs = refs[7:]
    if final_norm:
        gfin_ref = refs.pop(0)
    out_ref, cstate_ref, carry = refs

    if seqs_per_tile == 0:
        @pl.when(pl.program_id(0) % tiles_per_seq == 0)
        def _():
            carry[...] = hist_ref[0]

    subs = FFN_SUBTILES if seqs_per_tile == 0 else 1
    sub = tm // subs
    seqs = max(1, seqs_per_tile)
    span = sub // seqs

    def mixer_epilogue(rows):
        x = x_ref[rows, :]
        if pre == "glu":
            u = _rms(x, gmix_ref[...])
            g = _gelu(ys_ref[rows, :] + dskip_ref[...] * u).astype(BF16)
            z = _dot(g, wglu_ref[...])
            x = x + z[:, :D_MODEL] * _sigmoid(z[:, D_MODEL:])
        else:
            x = x + _dot(o_ref[rows, :], wo_ref[...])
        return x, _rms(x, gffn_ref[...]).astype(BF16)

    staged = [mixer_epilogue(slice(i * sub, (i + 1) * sub)) for i in range(subs)]
    last = {}
    for i, (x, h) in enumerate(staged):
        for lo, hi in zip(FFN_CHUNK_EDGES[:-1], FFN_CHUNK_EDGES[1:]):
            cs = slice(lo, hi)
            row = lax.broadcasted_iota(jnp.int32, (sub, hi - lo), 0)
            a = _dot(h, wup_ref[:, cs])
            a1, a2 = pltpu.roll(a, 1, 0), pltpu.roll(a, 2, 0)
            for s in range(seqs):
                if seqs_per_tile == 0:
                    prev = carry[:, cs] if i == 0 else last[lo]
                else:
                    prev = hist_ref[s, :, cs]
                a1 = jnp.where(row == s * span, prev[1:2, :], a1)
                a2 = jnp.where(row == s * span, prev[0:1, :], jnp.where(row == s * span + 1, prev[1:2, :], a2))
                if seqs_per_tile:
                    cstate_ref[s, :, cs] = a[(s + 1) * span - (CONV_WIDTH - 1):(s + 1) * span, :]
            conv = a2 * convw_ref[0:1, cs] + a1 * convw_ref[1:2, cs] + a * convw_ref[2:3, cs] + convb_ref[:, cs]
            last[lo] = a[sub - (CONV_WIDTH - 1):, :]
            hh = (_gelu(conv) * _dot(h, wgate_ref[:, cs])).astype(BF16)
            x = x + _dot(hh, wdown_ref[cs, :])
        if final_norm:
            x = _rms(x, gfin_ref[...])
        out_ref[i * sub:(i + 1) * sub, :] = x
    if seqs_per_tile == 0:
        for lo, hi in zip(FFN_CHUNK_EDGES[:-1], FFN_CHUNK_EDGES[1:]):
            carry[:, lo:hi] = last[lo]
            cstate_ref[0, :, lo:hi] = last[lo]


def _ffn(x, pre, pre_args, hist, g_ffn, w_up, w_gate, conv_w, conv_b, w_down, g_final, seq_len, tm):
    t = x.shape[0]
    n_seq = t // seq_len
    tiles_per_seq = max(1, seq_len // tm)
    seqs_per_tile = 0 if tm < seq_len else tm // seq_len
    tile = pl.BlockSpec((tm, D_MODEL), lambda i: (i, 0))
    vec = _full((1, D_MODEL))
    const = lambda a: pl.BlockSpec(a.shape, lambda i: (0,) * a.ndim, pipeline_mode=pl.Buffered(1))
    per_seq = pl.BlockSpec((max(1, seqs_per_tile), CONV_WIDTH - 1, D_FF), lambda i: (i // tiles_per_seq, 0, 0))
    args, in_specs = [x], [tile]
    if pre == "glu":
        ys, g_mix, d_skip, w_glu = pre_args
        args += [ys, g_mix[None, :], d_skip[None, :], w_glu]
        in_specs += [tile, vec, vec, const(w_glu)]
    else:
        o, w_o = pre_args
        args += [o, w_o]
        in_specs += [tile, const(w_o)]
    args += [hist, g_ffn[None, :], w_up, w_gate, conv_w, conv_b[None, :], w_down]
    in_specs += [per_seq, vec, const(w_up), const(w_gate), _full((CONV_WIDTH, D_FF)), _full((1, D_FF)),
                 const(w_down)]
    if g_final is not None:
        args.append(g_final[None, :])
        in_specs.append(vec)
    return pl.pallas_call(
        functools.partial(_ffn_kernel, pre, g_final is not None, tiles_per_seq, seqs_per_tile, tm),
        grid=(t // tm,),
        in_specs=in_specs,
        out_specs=[tile, per_seq],
        out_shape=[jax.ShapeDtypeStruct((t, D_MODEL), F32),
                   jax.ShapeDtypeStruct((n_seq, CONV_WIDTH - 1, D_FF), F32)],
        scratch_shapes=[pltpu.VMEM((CONV_WIDTH - 1, D_FF), F32)],
        compiler_params=_cparams(1), name="ffn_" + pre,
    )(*args)


def _c_pieces(c):
    hi, mid, lo = _split3(c * LOG2E)
    return (hi.astype(F32) + pltpu.roll(mid.astype(F32), FOX_HEADS, 1)
            + pltpu.roll(lo.astype(F32), 2 * FOX_HEADS, 1)).astype(BF16)


def _key_bias_lanes(c3):
    lane = lax.broadcasted_iota(jnp.int32, c3.shape, 1)
    ones = ((lane >= _CQ_LANE) & (lane < _CQ_LANE + 3)).astype(F32)
    return (ones - c3.astype(F32)).astype(BF16)


def _qkv_kernel(tiles_per_seq, tm, transposed_v, x_ref, g_ref, wq_ref, wk_ref, wv_ref, wf_ref, bf_ref,
                k_ref, v_ref, lf_ref, qb_ref, kb_ref, vb_ref, c3_ref, kc_ref, carry):
    @pl.when(pl.program_id(0) % tiles_per_seq == 0)
    def _():
        carry[...] = jnp.zeros_like(carry)

    u = _rms(x_ref[...], g_ref[...]).astype(BF16)
    lane = lax.broadcasted_iota(jnp.int32, (tm, LANES), 1)
    logf = jnp.where(lane < FOX_HEADS, _log_sigmoid(_dot(u, wf_ref[...]) + bf_ref[...]), 0.0)
    lf_ref[...] = logf[:, :FOX_HEADS]
    c = _tri_cumsum(logf, tm) + carry[0:1, :]
    carry[...] = jnp.broadcast_to(c[tm - 1:tm, :], carry.shape)
    c3 = _c_pieces(c)
    if transposed_v:
        c3_ref[0, 0] = jnp.transpose(c3.astype(F32))
    else:
        c3_ref[...] = c3
    kc_ref[...] = _key_bias_lanes(c3)
    q = _dot(u, wq_ref[...]) * (FOX_HEAD_DIM ** -0.5 * LOG2E)
    k = _dot(u, wk_ref[...])
    v = _dot(u, wv_ref[...])
    k_ref[...] = k
    v_ref[...] = v
    kb_ref[...] = k.astype(BF16)
    if transposed_v:
        qb_ref[0, 0] = jnp.transpose(q).astype(BF16)
        vb_ref[0, 0] = jnp.transpose(v).astype(BF16)
    else:
        qb_ref[...] = q.astype(BF16)
        vb_ref[...] = v.astype(BF16)


def _qkv(x, g, wq, wk, wv, wf, bf, seq_len, tm, transposed_v):
    t = x.shape[0]
    tiles_per_seq = seq_len // tm
    tile = pl.BlockSpec((tm, D_MODEL), lambda i: (i, 0))
    narrow = pl.BlockSpec((tm, LANES), lambda i: (i, 0))
    const = lambda a: pl.BlockSpec(a.shape, lambda i: (0,) * a.ndim, pipeline_mode=pl.Buffered(1))
    wide_bf = jax.ShapeDtypeStruct((t, D_MODEL), BF16)
    if transposed_v:
        per_tile = lambda i: (i // tiles_per_seq, i % tiles_per_seq, 0, 0)
        qv_spec = pl.BlockSpec((1, 1, D_MODEL, tm), per_tile)
        qv_shape = jax.ShapeDtypeStruct((t // seq_len, tiles_per_seq, D_MODEL, tm), BF16)
        c3_spec = pl.BlockSpec((1, 1, LANES, tm), per_tile)
        c3_shape = jax.ShapeDtypeStruct((t // seq_len, tiles_per_seq, LANES, tm), F32)
    else:
        qv_spec, qv_shape = tile, wide_bf
        c3_spec, c3_shape = narrow, jax.ShapeDtypeStruct((t, LANES), BF16)
    return pl.pallas_call(
        functools.partial(_qkv_kernel, tiles_per_seq, tm, transposed_v),
        grid=(t // tm,),
        in_specs=[tile, _full((1, D_MODEL)), const(wq), const(wk), const(wv), const(wf), _full((1, LANES))],
        out_specs=[tile, tile, pl.BlockSpec((tm, FOX_HEADS), lambda i: (i, 0)), qv_spec, tile, qv_spec,
                   c3_spec, narrow],
        out_shape=[jax.ShapeDtypeStruct((t, D_MODEL), F32), jax.ShapeDtypeStruct((t, D_MODEL), F32),
                   jax.ShapeDtypeStruct((t, FOX_HEADS), F32),
                   qv_shape, wide_bf, qv_shape, c3_shape, jax.ShapeDtypeStruct((t, LANES), BF16)],
        scratch_shapes=[pltpu.VMEM((8, LANES), F32)],
        compiler_params=_cparams(1), name="fox_qkv",
    )(x, g[None, :], wq, wk, wv, wf, bf)


def _query_ext(q, c3q, head, lo_half):
    rows = q.shape[0]
    lane = lax.broadcasted_iota(jnp.int32, (rows, LANES), 1)
    own = (lane < FOX_HEAD_DIM) if lo_half else (lane >= FOX_HEAD_DIM)
    qm = jnp.where(own, q, jnp.zeros_like(q))
    pick = lambda j: jnp.sum(jnp.where(lane == head + j * FOX_HEADS, c3q, 0.0), axis=1, keepdims=True)
    onehot = (lane == head) | (lane == head + FOX_HEADS) | (lane == head + 2 * FOX_HEADS)
    qc = jnp.where(lane == _CQ_LANE, pick(0),
                   jnp.where(lane == _CQ_LANE + 1, pick(1),
                             jnp.where(lane == _CQ_LANE + 2, pick(2), jnp.where(onehot, 1.0, 0.0))))
    return jnp.concatenate([qm, qc.astype(BF16)], axis=1)


def _sublane_allreduce(x, op):
    for sh in (4, 2, 1):
        x = op(x, pltpu.roll(x, sh, 0))
    return x


def _fold_rows(x, op):
    parts = [x[i * SUBLANES:(i + 1) * SUBLANES] for i in range(x.shape[0] // SUBLANES)]
    while len(parts) > 1:
        parts = [op(parts[i], parts[i + 1]) for i in range(0, len(parts), 2)]
    return parts[0]


KEY_CHUNK = 256
ATTN_PAIRS_PER_STEP = 4
ONES_ROWS = 16


def _attn_prompt_kernel(t, n_items, qi_tab, j_tab, q_ref, c3_ref, k_ref, kc_ref, vt_ref, o_ref,
                        qt, e_buf, cm_buf, acc):
    n_heads = acc.shape[0]
    first_head = pl.program_id(1) * n_heads
    pair_lanes = lambda a: slice((a // 2) * PAIR_LANES, (a // 2 + 1) * PAIR_LANES)
    big = -NEG_INF
    ck = KEY_CHUNK
    n_ck = t // ck
    for a in range(n_heads):
        acc[a] = jnp.zeros(acc.shape[1:], F32)
    ones_rows = jnp.ones((ONES_ROWS, ck), BF16)

    def load_q(qi):
        row = lax.broadcasted_iota(jnp.int32, (PAIR_LANES, t), 0)
        for a in range(n_heads):
            head = first_head + a
            own = (row < FOX_HEAD_DIM) if a % 2 == 0 else (row >= FOX_HEAD_DIM)
            q_pair = q_ref[0, qi, pair_lanes(a), :]
            piece = lambda j: c3_ref[0, qi, pl.ds(head + j * FOX_HEADS, 1), :]
            onehot = (row == head) | (row == head + FOX_HEADS) | (row == head + 2 * FOX_HEADS)
            bias = jnp.where(row == _CQ_LANE, piece(0),
                             jnp.where(row == _CQ_LANE + 1, piece(1),
                                       jnp.where(row == _CQ_LANE + 2, piece(2), jnp.where(onehot, 1.0, 0.0))))
            qt[a] = jnp.concatenate([jnp.where(own, q_pair, jnp.zeros_like(q_pair)), bias.astype(BF16)], axis=0)

    def stage_a(w, slot, diagonal):
        start = pl.multiple_of(j_tab[w] * t, t)
        width = 2 * LANES
        hk = ck
        for a in range(n_heads):
            for i in range(t // hk):
                rows = pl.ds(start + i * hk, hk)
                kext = jnp.concatenate([k_ref[rows, pair_lanes(a)], kc_ref[rows, :]], axis=1)
                for q0 in range(0, t, width):
                    if diagonal and i * hk > q0 + width - 1:
                        e_buf[slot, a, i * hk:(i + 1) * hk, q0:q0 + width] = jnp.zeros((hk, width), BF16)
                        cm_buf[slot, a, i, :, q0:q0 + width] = jnp.full((SUBLANES, width), NEG_INF, F32)
                        continue
                    s = _dot(kext, qt[a, :, q0:q0 + width])
                    if diagonal and (i + 1) * hk - 1 > q0:
                        r = lax.broadcasted_iota(jnp.int32, (hk, width), 0) + i * hk
                        c = lax.broadcasted_iota(jnp.int32, (hk, width), 1) + q0
                        s = jnp.where(r <= c, s, NEG_INF)
                    cm = _sublane_allreduce(_fold_rows(s, jnp.maximum), jnp.maximum)
                    e_buf[slot, a, i * hk:(i + 1) * hk, q0:q0 + width] = jnp.exp2(
                        s - jnp.tile(cm, (hk // SUBLANES, 1))).astype(BF16)
                    cm_buf[slot, a, i, :, q0:q0 + width] = cm

    def stage_c(w, ms, slot):
        m_floor = jnp.where(j_tab[w] == 0, NEG_INF, big)
        new_ms = []
        for a in range(n_heads):
            m_old = jnp.minimum(ms[a], m_floor)
            cms = [cm_buf[slot, a, i] for i in range(n_ck)]
            m_new = functools.reduce(jnp.maximum, cms, m_old)
            n_rep = acc.shape[1] // SUBLANES
            total = acc[a] * jnp.tile(jnp.exp2(m_old - m_new), (n_rep, 1))
            for i in range(n_ck):
                vt = jnp.concatenate([vt_ref[0, j_tab[w], pair_lanes(a), i * ck:(i + 1) * ck], ones_rows], axis=0)
                pv = _dot(vt, e_buf[slot, a, i * ck:(i + 1) * ck, :])
                total = total + pv * jnp.tile(jnp.exp2(cms[i] - m_new), (n_rep, 1))
            acc[a] = total
            new_ms.append(m_new)
        return tuple(new_ms)

    def finalize(w):
        outs = []
        for a in range(n_heads):
            full = acc[a]
            denom = jnp.tile(full[PAIR_LANES:PAIR_LANES + SUBLANES], (PAIR_LANES // SUBLANES, 1))
            outs.append(full[:PAIR_LANES] / denom)
        row = lax.broadcasted_iota(jnp.int32, (PAIR_LANES, t), 0)
        rows = pl.ds(pl.multiple_of(qi_tab[w] * t, t), t)
        for a in range(0, n_heads, 2):
            o = jnp.transpose(jnp.where(row < FOX_HEAD_DIM, outs[a], outs[a + 1]))
            o_ref[rows, pair_lanes(a)] = o.astype(o_ref.dtype)

    load_q(0)
    stage_a(0, 0, True)

    def item(w, ms, slot):
        nxt = jnp.minimum(w + 1, n_items - 1)

        @pl.when(j_tab[nxt] == 0)
        def _():
            load_q(qi_tab[nxt])

        def stages(diagonal, ms):
            stage_a(nxt, 1 - slot, diagonal)
            ms = stage_c(w, ms, slot)
            return ms

        ms = lax.cond(j_tab[nxt] == qi_tab[nxt], functools.partial(stages, True),
                      functools.partial(stages, False), ms)

        @pl.when(j_tab[w] == qi_tab[w])
        def _():
            finalize(w)

        return ms

    def body(i, ms):
        return item(2 * i + 1, item(2 * i, ms, 0), 1)

    ms = lax.fori_loop(0, n_items // 2, body, tuple(jnp.full((SUBLANES, t), NEG_INF, F32) for _ in range(n_heads)))
    if n_items % 2:
        item(n_items - 1, ms, 0)


def _attn_prompt(qt, kb, kc, vt, c3t, n_seq, seq_len, t):
    total = kb.shape[0]
    nt = seq_len // t
    items = [(qi, j) for qi in range(nt) for j in range(qi + 1)]
    qi_tab = jnp.asarray([qi for qi, _ in items], jnp.int32)
    j_tab = jnp.asarray([j for _, j in items], jnp.int32)
    nh = 2 * ATTN_PAIRS_PER_STEP
    width = ATTN_PAIRS_PER_STEP * PAIR_LANES
    once = pl.Buffered(1)
    seq_in = pl.BlockSpec((seq_len, width), lambda b, p, *_: (b, p), pipeline_mode=once)
    seq_bias = pl.BlockSpec((seq_len, LANES), lambda b, p, *_: (b, 0), pipeline_mode=once)
    vt_in = pl.BlockSpec((1, nt, width, t), lambda b, p, *_: (b, 0, p, 0), pipeline_mode=once)
    c3t_in = pl.BlockSpec((1, nt, LANES, t), lambda b, p, *_: (b, 0, 0, 0), pipeline_mode=once)
    return pl.pallas_call(
        functools.partial(_attn_prompt_kernel, t, len(items)),
        grid_spec=pltpu.PrefetchScalarGridSpec(
            num_scalar_prefetch=2,
            grid=(n_seq, HEAD_PAIRS // ATTN_PAIRS_PER_STEP),
            in_specs=[vt_in, c3t_in, seq_in, seq_bias, vt_in],
            out_specs=pl.BlockSpec((seq_len, width), lambda b, p, *_: (b, p), pipeline_mode=once),
            scratch_shapes=[pltpu.VMEM((nh, 2 * LANES, t), BF16), pltpu.VMEM((2, nh, t, t), BF16),
                            pltpu.VMEM((2, nh, t // KEY_CHUNK, SUBLANES, t), F32),
                            pltpu.VMEM((nh, PAIR_LANES + ONES_ROWS, t), F32)]),
        out_shape=jax.ShapeDtypeStruct((total, D_MODEL), BF16),
        compiler_params=_cparams(2), name="fox_attn_prompt",
    )(qi_tab, j_tab, qt, c3t, kb, kc, vt)


def _attn_sample_kernel(n_cache_tiles, tk, t_new, q_ref, c3q_ref, kc_ref, vc_ref, lfc_ref,
                        kn_ref, vn_ref, lfn_ref, o_ref, qt, m_scr, acc_scr, carry, s_buf, p_buf, al_buf):
    j = pl.program_id(1)
    rows = 2 * t_new

    @pl.when(j == 0)
    def _():
        c3q = c3q_ref[...].astype(F32)
        for p in range(HEAD_PAIRS):
            q = q_ref[:, p * PAIR_LANES:(p + 1) * PAIR_LANES]
            qe = jnp.concatenate([_query_ext(q, c3q, 2 * p, True), _query_ext(q, c3q, 2 * p + 1, False)], axis=0)
            qt[p] = jnp.transpose(qe.astype(F32)).astype(BF16)
        m_scr[...] = jnp.full(m_scr.shape, NEG_INF, F32)
        acc_scr[...] = jnp.zeros(acc_scr.shape, F32)
        carry[...] = jnp.zeros_like(carry)

    def attend(k_pair, v_pair, n_keys, c3k, masked):
        kc = _key_bias_lanes(c3k)
        for p in range(HEAD_PAIRS):
            s = _dot(jnp.concatenate([k_pair(p), kc], axis=1), qt[p])
            if masked:
                r = lax.broadcasted_iota(jnp.int32, (n_keys, rows), 0)
                c = lax.broadcasted_iota(jnp.int32, (n_keys, rows), 1) & (t_new - 1)
                s = jnp.where(r <= c, s, NEG_INF)
            s_buf[p, :n_keys] = s
        for p in range(HEAD_PAIRS):
            s = s_buf[p, :n_keys]
            m_old = m_scr[p]
            m_new = jnp.maximum(m_old, _sublane_allreduce(_fold_rows(s, jnp.maximum), jnp.maximum))
            p_buf[p, :n_keys] = jnp.exp2(s - jnp.tile(m_new, (n_keys // SUBLANES, 1))).astype(BF16)
            alpha = jnp.exp2(m_old - m_new)
            al_buf[p] = jnp.transpose(jnp.tile(alpha, (rows // SUBLANES, 1)))
            m_scr[p] = m_new
        ones = jnp.ones((n_keys, LANES), BF16)
        for p in range(HEAD_PAIRS):
            pv = lax.dot_general(p_buf[p, :n_keys], jnp.concatenate([v_pair(p), ones], axis=1),
                                 (((0,), (0,)), ((), ())), preferred_element_type=F32)
            acc_scr[p] = acc_scr[p] * jnp.tile(al_buf[p], (1, 2)) + pv

    def cache_pair(ref):
        return lambda p: ref[0, :, p * PAIR_LANES:(p + 1) * PAIR_LANES].astype(BF16)

    def new_pair(ref):
        return lambda p: ref[:, p * PAIR_LANES:(p + 1) * PAIR_LANES]

    @pl.when(j < n_cache_tiles)
    def _():
        c = _tri_cumsum(lfc_ref[0], tk) + carry[0:1, :]
        carry[...] = jnp.broadcast_to(c[tk - 1:tk, :], carry.shape)
        attend(cache_pair(kc_ref), cache_pair(vc_ref), tk, _c_pieces(c), False)

    @pl.when(j == n_cache_tiles)
    def _():
        c = _tri_cumsum(lfn_ref[...], t_new) + carry[0:1, :]
        attend(new_pair(kn_ref), new_pair(vn_ref), t_new, _c_pieces(c), True)
        lane = lax.broadcasted_iota(jnp.int32, (t_new, LANES), 1)
        for p in range(HEAD_PAIRS):
            acc = acc_scr[p]
            oa = acc[:t_new, :LANES] / acc[:t_new, LANES:]
            ob = acc[t_new:, :LANES] / acc[t_new:, LANES:]
            o_ref[:, p * PAIR_LANES:(p + 1) * PAIR_LANES] = jnp.where(lane < FOX_HEAD_DIM, oa, ob).astype(o_ref.dtype)


def _attn_sample(qb, kb, vb, c3, logf, cache_k, cache_v, cache_logf, tk):
    n_seq, past, _ = cache_k.shape
    t_new = qb.shape[0] // n_seq
    nct = past // tk
    pad = lambda a: jnp.pad(a, ((0, 0),) * (a.ndim - 1) + ((0, LANES - FOX_HEADS),))
    new_tile = pl.BlockSpec((t_new, D_MODEL), lambda b, j: (b, 0))
    new_c3 = pl.BlockSpec((t_new, LANES), lambda b, j: (b, 0))
    cache_tile = pl.BlockSpec((1, tk, D_MODEL), lambda b, j: (b, jnp.minimum(j, nct - 1), 0))
    lf_tile = pl.BlockSpec((1, tk, LANES), lambda b, j: (b, jnp.minimum(j, nct - 1), 0))
    return pl.pallas_call(
        functools.partial(_attn_sample_kernel, nct, tk, t_new),
        grid=(n_seq, nct + 1),
        in_specs=[new_tile, new_c3, cache_tile, cache_tile, lf_tile, new_tile, new_tile, new_c3],
        out_specs=new_tile,
        out_shape=jax.ShapeDtypeStruct((n_seq * t_new, D_MODEL), BF16),
        scratch_shapes=[pltpu.VMEM((HEAD_PAIRS, 2 * LANES, 2 * t_new), BF16),
                        pltpu.VMEM((HEAD_PAIRS, SUBLANES, 2 * t_new), F32),
                        pltpu.VMEM((HEAD_PAIRS, 2 * t_new, 2 * LANES), F32),
                        pltpu.VMEM((8, LANES), F32),
                        pltpu.VMEM((HEAD_PAIRS, tk, 2 * t_new), F32),
                        pltpu.VMEM((HEAD_PAIRS, tk, 2 * t_new), BF16),
                        pltpu.VMEM((HEAD_PAIRS, 2 * t_new, 2 * t_new), F32)],
        compiler_params=_cparams(2), name="fox_attn_sample",
    )(qb, c3, cache_k, cache_v, pad(cache_logf), kb, vb, pad(logf))


def _trunk(x, n_seq, seq_len, ssm_h0, fox_cache, conv_hist, w, tm, tq):
    t = n_seq * seq_len
    nb = seq_len // SSM_BLOCK
    nc = SSM_GROUPS // COLUMN_GROUPS
    half = COLUMN_GROUPS * SSM_STATE
    x = x.reshape(t, D_MODEL)

    u0 = _norm(x, w["norm_mix"][0], tm)
    if ssm_h0 is None:
        h0 = None
    else:
        per_c = lambda h: jnp.swapaxes(h.reshape(n_seq, nc, half), 0, 1)
        h0c = jnp.concatenate([per_c(ssm_h0[0]), per_c(ssm_h0[1])], axis=2)
        h0 = jnp.zeros((nc, n_seq, nb, 2 * half), F32).at[:, :, 0, :].set(h0c).reshape(nc, n_seq * nb, 2 * half)
    ys, fin = _s5_core(u0, w["s5"], n_seq, seq_len, h0, fold_sequences=ssm_h0 is not None)
    per_g = lambda f: jnp.swapaxes(f.reshape(nc, n_seq, COLUMN_GROUPS, SSM_STATE), 0, 1).reshape(
        n_seq, SSM_GROUPS, SSM_STATE)
    fre, fim = per_g(fin[:, :, :half]), per_g(fin[:, :, half:])
    x, conv0 = _ffn(x, "glu", (ys, w["norm_mix"][0], w["ssm_d"], w["w_glu"]), conv_hist[0],
                    w["norm_ffn"][0], w["w_up"][0], w["w_gate"][0], w["conv_w"][0], w["conv_b"][0],
                    w["w_down"][0], None, seq_len, tm)

    k, v, logf, qb, kb, vb, c3, kc = _qkv(x, w["norm_mix"][1], w["wq"], w["wk"], w["wv"], w["wf"], w["bf"],
                                          seq_len, tq, transposed_v=fox_cache is None)
    if fox_cache is None:
        o = _attn_prompt(qb, kb, kc, vb, c3, n_seq, seq_len, tq)
    else:
        o = _attn_sample(qb, kb, vb, c3, logf, *fox_cache, tk=min(512, fox_cache[0].shape[1]))
    y, conv1 = _ffn(x, "oproj", (o, w["w_o"]), conv_hist[1],
                    w["norm_ffn"][1], w["w_up"][1], w["w_gate"][1], w["conv_w"][1], w["conv_b"][1],
                    w["w_down"][1], w["norm_final"], seq_len, tm)

    shape4 = (1, n_seq, seq_len, FOX_HEADS, FOX_HEAD_DIM)
    return (y.reshape(n_seq, seq_len, D_MODEL),
            fre[None], fim[None],
            k.reshape(shape4), v.reshape(shape4), logf.reshape(1, n_seq, seq_len, FOX_HEADS),
            jnp.stack([conv0, conv1]))


def kernel(x_prompt, x_sample, state_ssm_re, state_ssm_im, cache_fox_k, cache_fox_v, cache_fox_logf, state_ffn_conv, norm_mix, norm_ffn, norm_final, ssm_a_re, ssm_a_im, ssm_log_step, ssm_b_re, ssm_b_im, ssm_c_re, ssm_c_im, ssm_d, ssm_w_glu, fox_w_qkvf, fox_b_f, fox_w_o, ffn_w_up, ffn_w_gate, ffn_conv_w, ffn_conv_b, ffn_w_down):
    n_p, l_p, _ = x_prompt.shape
    n_s, l_s, _ = x_sample.shape
    past = cache_fox_k.shape[2]
    wqkvf = fox_w_qkvf[0]
    w = {
        "norm_mix": norm_mix, "norm_ffn": norm_ffn, "norm_final": norm_final,
        "s5": _s5_block_diagonal(_s5_params(ssm_a_re[0], ssm_a_im[0], ssm_log_step[0], ssm_b_re[0],
                                            ssm_b_im[0], ssm_c_re[0], ssm_c_im[0])),
        "ssm_d": ssm_d[0], "w_glu": ssm_w_glu[0].astype(BF16),
        "wq": wqkvf[:, :D_MODEL].astype(BF16), "wk": wqkvf[:, D_MODEL:2 * D_MODEL].astype(BF16),
        "wv": wqkvf[:, 2 * D_MODEL:3 * D_MODEL].astype(BF16),
        "wf": jnp.pad(wqkvf[:, 3 * D_MODEL:], ((0, 0), (0, LANES - FOX_HEADS))).astype(BF16),
        "bf": jnp.pad(fox_b_f[0], (0, LANES - FOX_HEADS))[None, :],
        "w_o": fox_w_o[0].astype(BF16),
        "w_up": ffn_w_up.astype(BF16), "w_gate": ffn_w_gate.astype(BF16), "w_down": ffn_w_down.astype(BF16),
        "conv_w": ffn_conv_w, "conv_b": ffn_conv_b,
    }
    zero_hist = jnp.zeros((2, n_p, CONV_WIDTH - 1, D_FF), F32)
    outs_p = _trunk(x_prompt, n_p, l_p, None, None, zero_hist, w, tm=512, tq=512)
    cache = (cache_fox_k[0].reshape(n_s, past, D_MODEL), cache_fox_v[0].reshape(n_s, past, D_MODEL),
             cache_fox_logf[0])
    outs_s = _trunk(x_sample, n_s, l_s, (state_ssm_re[0], state_ssm_im[0]), cache, state_ffn_conv, w,
                    tm=4 * l_s, tq=l_s)
    return (outs_p[0], outs_s[0]) + outs_p[1:] + outs_s[1:]
```

```python
import functools
import math

import jax
import jax.numpy as jnp
from jax import lax
from jax.experimental import pallas as pl
from jax.experimental.pallas import tpu as pltpu

D_MODEL = 1024
SSM_GROUP = 16
SSM_GROUPS = D_MODEL // SSM_GROUP
SSM_STATE = 64
SSM_BLOCK = 16
SSM_BLOCK_LOG2 = 4
BLOCK_LANES = SSM_BLOCK * SSM_GROUP
COLUMN_LANES = 128
COLUMN_GROUPS = COLUMN_LANES // SSM_GROUP
STEP_PAIRS = SSM_BLOCK // 2
FOX_HEADS = 16
FOX_HEAD_DIM = 64
HEAD_PAIRS = FOX_HEADS // 2
PAIR_LANES = 2 * FOX_HEAD_DIM
D_FF = 2816
CONV_WIDTH = 3
PROMPT_TILE = 512
SAMPLE_SEQS_PER_TILE = 4
CACHE_TILE = 512
FFN_CHUNK_EDGES = (0, 1536, D_FF)
NORM_EPS = 1e-6
NEG_INF = -1e30
LOG2E = 1.4426950408889634
LANES = 128
SUBLANES = 8
_CQ_LANE = 3 * FOX_HEADS
VMEM_LIMIT = 56 * 1024 * 1024

BF16 = jnp.bfloat16
F32 = jnp.float32


def _cparams(n_axes, vmem=VMEM_LIMIT):
    return pltpu.CompilerParams(dimension_semantics=("arbitrary",) * n_axes, vmem_limit_bytes=vmem)


def _full(shape):
    nd = len(shape)
    return pl.BlockSpec(shape, lambda *_: (0,) * nd)


def _rms(x, g):
    ms = jnp.mean(x * x, axis=-1, keepdims=True)
    return x * lax.rsqrt(ms + NORM_EPS) * g


def _gelu(x):
    c = math.sqrt(2.0 / math.pi)
    return 0.5 * x * (1.0 + jnp.tanh(c * (x + 0.044715 * (x * x * x))))


def _sigmoid(x):
    return 1.0 / (1.0 + jnp.exp(-x))


def _log_sigmoid(x):
    return jnp.minimum(x, 0.0) - jnp.log1p(jnp.exp(-jnp.abs(x)))


def _split3(x):
    hi = x.astype(BF16)
    r1 = x - hi.astype(F32)
    mid = r1.astype(BF16)
    lo = (r1 - mid.astype(F32)).astype(BF16)
    return hi, mid, lo


def _dot(a, b):
    return jnp.dot(a, b, preferred_element_type=F32)


def _dot_nt(a, b):
    return lax.dot_general(a, b, (((1,), (1,)), ((), ())), preferred_element_type=F32)


def _dot_nt_f32(a, b):
    a0, a1, a2 = _split3(a)
    b0, b1, b2 = _split3(b)
    return (_dot_nt(a0, b0) + (_dot_nt(a0, b1) + _dot_nt(a1, b0))
            + (_dot_nt(a0, b2) + _dot_nt(a1, b1) + _dot_nt(a2, b0)))


def _tri_cumsum(x, n):
    r = lax.broadcasted_iota(jnp.int32, (n, n), 0)
    c = lax.broadcasted_iota(jnp.int32, (n, n), 1)
    tri = jnp.where(c <= r, 1.0, 0.0).astype(BF16)
    hi, mid, lo = _split3(x)
    return _dot(tri, hi) + _dot(tri, mid) + _dot(tri, lo)


def _s5_param_kernel(are_r, aim_r, ls_ref, btre, btim, ctre, ctim,
                     k_ref, wre_ref, wim_ref, zre_ref, zim_ref, mu_ref):
    dt = jnp.exp(ls_ref[0])
    lre, lim = are_r[0], aim_r[0]
    mag = jnp.exp(lre * dt)
    abr = mag * jnp.cos(lim * dt)
    abi = mag * jnp.sin(lim * dt)
    den = lre * lre + lim * lim
    nr = abr - 1.0
    zre = (nr * lre + abi * lim) / den
    zim = (abi * lre - nr * lim) / den

    def cpow(e):
        m = jnp.exp(lre * dt * e)
        a = lim * dt * e
        return m * jnp.cos(a), m * jnp.sin(a)

    def per_step(e):
        step = lax.broadcasted_iota(jnp.int32, (SSM_BLOCK, SSM_STATE), 0).astype(F32)
        pr, pi = cpow(e(step))
        rep = lambda t: jnp.broadcast_to(t[:, None, :], (SSM_BLOCK, SSM_GROUP, SSM_STATE)).reshape(
            BLOCK_LANES, SSM_STATE)
        return rep(pr), rep(pi)

    bre, bim = btre[0], btim[0]
    bbre = zre * bre - zim * bim
    bbim = zre * bim + zim * bre
    pwr, pwi = per_step(lambda s: (SSM_BLOCK - 1.0) - s)
    wre_ref[0] = (pwr * bbre - pwi * bbim).astype(BF16)
    wim_ref[0] = (pwr * bbim + pwi * bbre).astype(BF16)
    pbr, pbi = per_step(lambda s: -s)
    bxre = pbr * bbre - pbi * bbim
    bxim = pbr * bbim + pbi * bbre

    cre, cim = ctre[0], ctim[0]
    pcr, pci = per_step(lambda t: t)
    cxre = pcr * cre - pci * cim
    cxim = pcr * cim + pci * cre
    pzr, pzi = per_step(lambda t: t + 1.0)
    zre_ref[0] = (pzr * cre - pzi * cim).astype(BF16)
    zim_ref[0] = (-(pzr * cim + pzi * cre)).astype(BF16)

    kfull = _dot_nt_f32(bxre, cxre) - _dot_nt_f32(bxim, cxim)
    rs = lax.broadcasted_iota(jnp.int32, (BLOCK_LANES, BLOCK_LANES), 0) >> SSM_BLOCK_LOG2
    ct = lax.broadcasted_iota(jnp.int32, (BLOCK_LANES, BLOCK_LANES), 1) >> SSM_BLOCK_LOG2
    k_ref[0] = jnp.where(ct >= rs, kfull, 0.0).astype(BF16)

    mur, mui = cpow(float(SSM_BLOCK))
    mu_ref[0, 0:1, :] = mur
    mu_ref[0, 1:2, :] = mui


def _s5_params(a_re, a_im, log_step, b_re, b_im, c_re, c_im):
    g, p = a_re.shape
    bt = jnp.tile(jnp.swapaxes(b_re, 1, 2), (1, SSM_BLOCK, 1)), jnp.tile(jnp.swapaxes(b_im, 1, 2), (1, SSM_BLOCK, 1))
    ct = jnp.tile(c_re, (1, SSM_BLOCK, 1)), jnp.tile(c_im, (1, SSM_BLOCK, 1))
    row = pl.BlockSpec((1, 1, p), lambda i: (i, 0, 0))
    one = pl.BlockSpec((1, 1, 1), lambda i: (i, 0, 0))
    tall = pl.BlockSpec((1, BLOCK_LANES, p), lambda i: (i, 0, 0))
    return pl.pallas_call(
        _s5_param_kernel,
        grid=(g,),
        in_specs=[row, row, one, tall, tall, tall, tall],
        out_specs=[pl.BlockSpec((1, BLOCK_LANES, BLOCK_LANES), lambda i: (i, 0, 0)),
                   tall, tall, tall, tall, pl.BlockSpec((1, 2, p), lambda i: (i, 0, 0))],
        out_shape=[jax.ShapeDtypeStruct((g, BLOCK_LANES, BLOCK_LANES), BF16)]
        + [jax.ShapeDtypeStruct((g, BLOCK_LANES, p), BF16)] * 4
        + [jax.ShapeDtypeStruct((g, 2, p), F32)],
        compiler_params=_cparams(1),
        name="s5_params",
    )(a_re[:, None, :], a_im[:, None, :], log_step[:, None, None], bt[0], bt[1], ct[0], ct[1])


def _norm_kernel(x_ref, g_ref, u_ref):
    u_ref[...] = _rms(x_ref[...], g_ref[...]).astype(u_ref.dtype)


def _norm(x, g, tm):
    t = x.shape[0]
    tile = pl.BlockSpec((tm, D_MODEL), lambda i: (i, 0))
    return pl.pallas_call(
        _norm_kernel, grid=(t // tm,),
        in_specs=[tile, _full((1, D_MODEL))], out_specs=tile,
        out_shape=jax.ShapeDtypeStruct((t, D_MODEL), F32),
        compiler_params=_cparams(1), name="s5_prenorm",
    )(x, g[None, :])


def _s5_core_kernel(nb, n_seq, has_h0, *refs):
    if has_h0:
        u_ref, kk_ref, ww_ref, zz_ref, mu_ref, h0_ref, y_ref, f_ref, s_scr = refs
    else:
        u_ref, kk_ref, ww_ref, zz_ref, mu_ref, y_ref, f_ref, s_scr = refs
    rows = nb * n_seq
    half = COLUMN_GROUPS * SSM_STATE
    xp = [jnp.concatenate([u_ref[pl.ds(2 * p, rows, stride=SSM_BLOCK), :],
                           u_ref[pl.ds(2 * p + 1, rows, stride=SSM_BLOCK), :]], axis=1).astype(BF16)
          for p in range(STEP_PAIRS)]
    s = _dot(xp[0], ww_ref[0, 0])
    for p in range(1, STEP_PAIRS):
        s = s + _dot(xp[p], ww_ref[0, p])
    sre, sim = s[:, :half], s[:, half:]
    mu = mu_ref[0]
    mre, mim = mu[0:1, :], mu[1:2, :]
    blk = lax.broadcasted_iota(jnp.int32, (rows, half), 0) & (nb - 1)
    if has_h0:
        h0 = h0_ref[0]
        h0re, h0im = h0[:, :half], h0[:, half:]
        sre = sre + (mre * h0re - mim * h0im)
        sim = sim + (mre * h0im + mim * h0re)
    d = 1
    while d < nb:
        keep = blk >= d
        shr = jnp.where(keep, pltpu.roll(sre, d, 0), 0.0)
        shi = jnp.where(keep, pltpu.roll(sim, d, 0), 0.0)
        sre, sim = sre + (mre * shr - mim * shi), sim + (mre * shi + mim * shr)
        mre, mim = mre * mre - mim * mim, 2.0 * mre * mim
        d *= 2
    n_chunks = half // LANES
    for i in range(n_chunks):
        s_scr[i] = sre[:, i * LANES:(i + 1) * LANES]
        s_scr[n_chunks + i] = sim[:, i * LANES:(i + 1) * LANES]
    f_ref[0, 0] = jnp.concatenate([s_scr[i, pl.ds(nb - 1, n_seq, stride=nb), :] for i in range(2 * n_chunks)],
                                  axis=1)
    first = blk == 0
    inre = jnp.where(first, 0.0, pltpu.roll(sre, 1, 0))
    inim = jnp.where(first, 0.0, pltpu.roll(sim, 1, 0))
    if has_h0:
        inre = inre + h0re
        inim = inim + h0im
    s_in = jnp.concatenate([inre, inim], axis=1).astype(BF16)
    for p in range(STEP_PAIRS):
        y = _dot_nt(s_in, zz_ref[0, p])
        for q in range(p + 1):
            y = y + _dot(xp[q], kk_ref[0, p - q])
        y_ref[pl.ds(2 * p, rows, stride=SSM_BLOCK), :] = y[:, :COLUMN_LANES]
        y_ref[pl.ds(2 * p + 1, rows, stride=SSM_BLOCK), :] = y[:, COLUMN_LANES:]


def _s5_block_diagonal(params):
    kmat, wre, wim, zre, zim, mu = params
    cg, nc = COLUMN_GROUPS, SSM_GROUPS // COLUMN_GROUPS
    rows = 2 * COLUMN_LANES
    grp = lambda n, width: (jnp.arange(n) // width) % cg

    def stack_rows(m, pairs_on_rows):
        minor = m.shape[-1]
        if pairs_on_rows:
            m6 = m.reshape(nc, cg, STEP_PAIRS, 2, SSM_GROUP, minor)
            return jnp.transpose(m6, (0, 2, 3, 1, 4, 5)).reshape(nc, STEP_PAIRS, rows, minor)
        m6 = m.reshape(nc, cg, 2, SSM_GROUP, STEP_PAIRS, minor // STEP_PAIRS)
        return jnp.transpose(m6, (0, 4, 2, 1, 3, 5)).reshape(nc, STEP_PAIRS, rows, minor // STEP_PAIRS)

    def spread(stacked, width):
        n_in = stacked.shape[-1]
        n_out = n_in * cg
        src = (jnp.arange(n_out) // (width * cg)) * width + jnp.arange(n_out) % width
        place = (jnp.arange(n_in)[:, None] == src[None, :]).astype(BF16)
        own = (grp(rows, SSM_GROUP)[:, None] == grp(n_out, width)[None, :]).astype(BF16)
        return jnp.einsum('cprk,kn->cprn', stacked, place, preferred_element_type=BF16) * own

    kk = spread(stack_rows(kmat[:, :2 * SSM_GROUP, :], False), SSM_GROUP)
    ww = spread(stack_rows(jnp.concatenate([wre, wim], axis=2), True), SSM_STATE)
    zz_t = spread(stack_rows(jnp.concatenate([zre, zim], axis=2), True), SSM_STATE)
    mu_c = jnp.transpose(mu.reshape(nc, cg, 2, SSM_STATE), (0, 2, 1, 3)).reshape(nc, 2, cg * SSM_STATE)
    return kk, ww, zz_t, mu_c


def _s5_core(u, bd, n_seq, seq_len, h0, fold_sequences):
    kk, ww, zz, mu_c = bd
    nc = kk.shape[0]
    nb = seq_len // SSM_BLOCK
    seqs = n_seq if fold_sequences else 1
    n_steps = n_seq // seqs
    rows_tok = seqs * seq_len
    rows = seqs * nb
    state = 2 * COLUMN_GROUPS * SSM_STATE
    col = pl.BlockSpec((rows_tok, COLUMN_LANES), lambda c, i: (i, c))
    per_c = lambda a: pl.BlockSpec((1,) + a.shape[1:], lambda c, i: (c,) + (0,) * (a.ndim - 1))
    in_specs = [col, per_c(kk), per_c(ww), per_c(zz), per_c(mu_c)]
    args = [u, kk, ww, zz, mu_c]
    if h0 is not None:
        in_specs.append(pl.BlockSpec((1, rows, state), lambda c, i: (c, i, 0)))
        args.append(h0)
    y, fin = pl.pallas_call(
        functools.partial(_s5_core_kernel, nb, seqs, h0 is not None),
        grid=(nc, n_steps),
        in_specs=in_specs,
        out_specs=[col, pl.BlockSpec((1, 1, seqs, state), lambda c, i: (c, i, 0, 0))],
        out_shape=[jax.ShapeDtypeStruct(u.shape, F32), jax.ShapeDtypeStruct((nc, n_steps, seqs, state), F32)],
        scratch_shapes=[pltpu.VMEM((state // LANES, rows, LANES), F32)],
        compiler_params=_cparams(2), name="s5_core",
    )(*args)
    return y, fin.reshape(nc, n_seq, state)


def _ffn_kernel(pre, final_norm, tiles_per_seq, seqs_per_tile, tm, *refs):
    refs = list(refs)
    x_ref = refs.pop(0)
    if pre == "glu":
        ys_ref, gmix_ref, dskip_ref, wglu_ref = refs[:4]
        refs = refs[4:]
    else:
        o_ref, wo_ref = refs[:2]
        refs = refs[2:]
    (hist_ref, gffn_ref, wup_ref, wgate_ref, convw_ref, convb_ref, wdown_ref) = refs[:7]
    refs = refs[7:]
    if final_norm:
        gfin_ref = refs.pop(0)
    out_ref, cstate_ref, carry = refs
    chained = seqs_per_tile == 0

    if chained:
        @pl.when(pl.program_id(0) % tiles_per_seq == 0)
        def _():
            carry[...] = hist_ref[0]

    x = x_ref[...]
    if pre == "glu":
        u = _rms(x, gmix_ref[...])
        g = _gelu(ys_ref[...] + dskip_ref[...] * u).astype(BF16)
        z = _dot(g, wglu_ref[...])
        x = x + z[:, :D_MODEL] * _sigmoid(z[:, D_MODEL:])
    else:
        x = x + _dot(o_ref[...], wo_ref[...])
    h = _rms(x, gffn_ref[...]).astype(BF16)

    seqs = max(1, seqs_per_tile)
    span = tm // seqs
    for lo, hi in zip(FFN_CHUNK_EDGES[:-1], FFN_CHUNK_EDGES[1:]):
        cs = slice(lo, hi)
        row = lax.broadcasted_iota(jnp.int32, (tm, hi - lo), 0)
        a = _dot(h, wup_ref[:, cs])
        a1, a2 = pltpu.roll(a, 1, 0), pltpu.roll(a, 2, 0)
        for s in range(seqs):
            prev = carry[:, cs] if chained else hist_ref[s, :, cs]
            a1 = jnp.where(row == s * span, prev[1:2, :], a1)
            a2 = jnp.where(row == s * span, prev[0:1, :], jnp.where(row == s * span + 1, prev[1:2, :], a2))
            cstate_ref[s, :, cs] = a[(s + 1) * span - (CONV_WIDTH - 1):(s + 1) * span, :]
        conv = a2 * convw_ref[0:1, cs] + a1 * convw_ref[1:2, cs] + a * convw_ref[2:3, cs] + convb_ref[:, cs]
        if chained:
            carry[:, cs] = a[tm - (CONV_WIDTH - 1):, :]
        hh = (_gelu(conv) * _dot(h, wgate_ref[:, cs])).astype(BF16)
        x = x + _dot(hh, wdown_ref[cs, :])
    if final_norm:
        x = _rms(x, gfin_ref[...])
    out_ref[...] = x


def _ffn(x, pre, pre_args, hist, g_ffn, w_up, w_gate, conv_w, conv_b, w_down, g_final, seq_len, tm):
    t = x.shape[0]
    n_seq = t // seq_len
    tiles_per_seq = max(1, seq_len // tm)
    seqs_per_tile = 0 if tm < seq_len else tm // seq_len
    tile = pl.BlockSpec((tm, D_MODEL), lambda i: (i, 0))
    vec = _full((1, D_MODEL))
    const = lambda a: pl.BlockSpec(a.shape, lambda i: (0,) * a.ndim, pipeline_mode=pl.Buffered(1))
    per_seq = pl.BlockSpec((max(1, seqs_per_tile), CONV_WIDTH - 1, D_FF), lambda i: (i // tiles_per_seq, 0, 0))
    args, in_specs = [x], [tile]
    if pre == "glu":
        ys, g_mix, d_skip, w_glu = pre_args
        args += [ys, g_mix[None, :], d_skip[None, :], w_glu]
        in_specs += [tile, vec, vec, const(w_glu)]
    else:
        o, w_o = pre_args
        args += [o, w_o]
        in_specs += [tile, const(w_o)]
    args += [hist, g_ffn[None, :], w_up, w_gate, conv_w, conv_b[None, :], w_down]
    in_specs += [per_seq, vec, const(w_up), const(w_gate), _full((CONV_WIDTH, D_FF)), _full((1, D_FF)),
                 const(w_down)]
    if g_final is not None:
        args.append(g_final[None, :])
        in_specs.append(vec)
    return pl.pallas_call(
        functools.partial(_ffn_kernel, pre, g_final is not None, tiles_per_seq, seqs_per_tile, tm),
        grid=(t // tm,),
        in_specs=in_specs,
        out_specs=[tile, per_seq],
        out_shape=[jax.ShapeDtypeStruct((t, D_MODEL), F32),
                   jax.ShapeDtypeStruct((n_seq, CONV_WIDTH - 1, D_FF), F32)],
        scratch_shapes=[pltpu.VMEM((CONV_WIDTH - 1, D_FF), F32)],
        compiler_params=_cparams(1), name="ffn_" + pre,
    )(*args)


def _c_pieces(c):
    hi, mid, lo = _split3(c * LOG2E)
    return (hi.astype(F32) + pltpu.roll(mid.astype(F32), FOX_HEADS, 1)
            + pltpu.roll(lo.astype(F32), 2 * FOX_HEADS, 1)).astype(BF16)


def _key_bias_lanes(c3):
    lane = lax.broadcasted_iota(jnp.int32, c3.shape, 1)
    ones = ((lane >= _CQ_LANE) & (lane < _CQ_LANE + 3)).astype(F32)
    return (ones - c3.astype(F32)).astype(BF16)


def _qkv_kernel(tiles_per_seq, tm, transposed_v, x_ref, g_ref, wq_ref, wk_ref, wv_ref, wf_ref, bf_ref,
                k_ref, v_ref, lf_ref, qb_ref, kb_ref, vb_ref, c3_ref, kc_ref, carry):
    @pl.when(pl.program_id(0) % tiles_per_seq == 0)
    def _():
        carry[...] = jnp.zeros_like(carry)

    u = _rms(x_ref[...], g_ref[...]).astype(BF16)
    lane = lax.broadcasted_iota(jnp.int32, (tm, LANES), 1)
    logf = jnp.where(lane < FOX_HEADS, _log_sigmoid(_dot(u, wf_ref[...]) + bf_ref[...]), 0.0)
    lf_ref[...] = logf[:, :FOX_HEADS]
    c = _tri_cumsum(logf, tm) + carry[0:1, :]
    carry[...] = jnp.broadcast_to(c[tm - 1:tm, :], carry.shape)
    c3 = _c_pieces(c)
    if transposed_v:
        c3_ref[0, 0] = jnp.transpose(c3.astype(F32))
    else:
        c3_ref[...] = c3
    kc_ref[...] = _key_bias_lanes(c3)
    q = _dot(u, wq_ref[...]) * (FOX_HEAD_DIM ** -0.5 * LOG2E)
    k = _dot(u, wk_ref[...])
    v = _dot(u, wv_ref[...])
    k_ref[...] = k
    v_ref[...] = v
    kb_ref[...] = k.astype(BF16)
    if transposed_v:
        qb_ref[0, 0] = jnp.transpose(q).astype(BF16)
        vb_ref[0, 0] = jnp.transpose(v).astype(BF16)
    else:
        qb_ref[...] = q.astype(BF16)
        vb_ref[...] = v.astype(BF16)


def _qkv(x, g, wq, wk, wv, wf, bf, seq_len, tm, transposed_v):
    t = x.shape[0]
    tiles_per_seq = seq_len // tm
    tile = pl.BlockSpec((tm, D_MODEL), lambda i: (i, 0))
    narrow = pl.BlockSpec((tm, LANES), lambda i: (i, 0))
    const = lambda a: pl.BlockSpec(a.shape, lambda i: (0,) * a.ndim, pipeline_mode=pl.Buffered(1))
    wide_bf = jax.ShapeDtypeStruct((t, D_MODEL), BF16)
    if transposed_v:
        per_tile = lambda i: (i // tiles_per_seq, i % tiles_per_seq, 0, 0)
        qv_spec = pl.BlockSpec((1, 1, D_MODEL, tm), per_tile)
        qv_shape = jax.ShapeDtypeStruct((t // seq_len, tiles_per_seq, D_MODEL, tm), BF16)
        c3_spec = pl.BlockSpec((1, 1, LANES, tm), per_tile)
        c3_shape = jax.ShapeDtypeStruct((t // seq_len, tiles_per_seq, LANES, tm), F32)
    else:
        qv_spec, qv_shape = tile, wide_bf
        c3_spec, c3_shape = narrow, jax.ShapeDtypeStruct((t, LANES), BF16)
    return pl.pallas_call(
        functools.partial(_qkv_kernel, tiles_per_seq, tm, transposed_v),
        grid=(t // tm,),
        in_specs=[tile, _full((1, D_MODEL)), const(wq), const(wk), const(wv), const(wf), _full((1, LANES))],
        out_specs=[tile, tile, pl.BlockSpec((tm, FOX_HEADS), lambda i: (i, 0)), qv_spec, tile, qv_spec,
                   c3_spec, narrow],
        out_shape=[jax.ShapeDtypeStruct((t, D_MODEL), F32), jax.ShapeDtypeStruct((t, D_MODEL), F32),
                   jax.ShapeDtypeStruct((t, FOX_HEADS), F32),
                   qv_shape, wide_bf, qv_shape, c3_shape, jax.ShapeDtypeStruct((t, LANES), BF16)],
        scratch_shapes=[pltpu.VMEM((8, LANES), F32)],
        compiler_params=_cparams(1), name="fox_qkv",
    )(x, g[None, :], wq, wk, wv, wf, bf)


def _query_ext(q, c3q, head, lo_half):
    rows = q.shape[0]
    lane = lax.broadcasted_iota(jnp.int32, (rows, LANES), 1)
    own = (lane < FOX_HEAD_DIM) if lo_half else (lane >= FOX_HEAD_DIM)
    qm = jnp.where(own, q, jnp.zeros_like(q))
    pick = lambda j: jnp.sum(jnp.where(lane == head + j * FOX_HEADS, c3q, 0.0), axis=1, keepdims=True)
    onehot = (lane == head) | (lane == head + FOX_HEADS) | (lane == head + 2 * FOX_HEADS)
    qc = jnp.where(lane == _CQ_LANE, pick(0),
                   jnp.where(lane == _CQ_LANE + 1, pick(1),
                             jnp.where(lane == _CQ_LANE + 2, pick(2), jnp.where(onehot, 1.0, 0.0))))
    return jnp.concatenate([qm, qc.astype(BF16)], axis=1)


def _sublane_allreduce(x, op):
    for sh in (4, 2, 1):
        x = op(x, pltpu.roll(x, sh, 0))
    return x


def _fold_rows(x, op):
    parts = [x[i * SUBLANES:(i + 1) * SUBLANES] for i in range(x.shape[0] // SUBLANES)]
    while len(parts) > 1:
        parts = [op(parts[i], parts[i + 1]) for i in range(0, len(parts), 2)]
    return parts[0]


KEY_CHUNK = 256
ATTN_PAIRS_PER_STEP = 4
ONES_ROWS = 16


def _attn_prompt_kernel(t, n_items, qi_tab, j_tab, q_ref, c3_ref, k_ref, kc_ref, vt_ref, o_ref,
                        qt, e_buf, cm_buf, acc):
    n_heads = acc.shape[0]
    first_head = pl.program_id(1) * n_heads
    pair_lanes = lambda a: slice((a // 2) * PAIR_LANES, (a // 2 + 1) * PAIR_LANES)
    big = -NEG_INF
    ck = KEY_CHUNK
    n_ck = t // ck
    for a in range(n_heads):
        acc[a] = jnp.zeros(acc.shape[1:], F32)
    ones_rows = jnp.ones((ONES_ROWS, ck), BF16)

    def load_q(qi):
        row = lax.broadcasted_iota(jnp.int32, (PAIR_LANES, t), 0)
        for a in range(n_heads):
            head = first_head + a
            own = (row < FOX_HEAD_DIM) if a % 2 == 0 else (row >= FOX_HEAD_DIM)
            q_pair = q_ref[0, qi, pair_lanes(a), :]
            piece = lambda j: c3_ref[0, qi, pl.ds(head + j * FOX_HEADS, 1), :]
            onehot = (row == head) | (row == head + FOX_HEADS) | (row == head + 2 * FOX_HEADS)
            bias = jnp.where(row == _CQ_LANE, piece(0),
                             jnp.where(row == _CQ_LANE + 1, piece(1),
                                       jnp.where(row == _CQ_LANE + 2, piece(2), jnp.where(onehot, 1.0, 0.0))))
            qt[a] = jnp.concatenate([jnp.where(own, q_pair, jnp.zeros_like(q_pair)), bias.astype(BF16)], axis=0)

    def stage_a(w, slot, diagonal):
        start = pl.multiple_of(j_tab[w] * t, t)
        width = 2 * LANES
        for a in range(n_heads):
            for i in range(n_ck):
                rows = pl.ds(start + i * ck, ck)
                kext = jnp.concatenate([k_ref[rows, pair_lanes(a)], kc_ref[rows, :]], axis=1)
                for q0 in range(0, t, width):
                    if diagonal and i * ck > q0 + width - 1:
                        e_buf[slot, a, i * ck:(i + 1) * ck, q0:q0 + width] = jnp.zeros((ck, width), BF16)
                        cm_buf[slot, a, i, :, q0:q0 + width] = jnp.full((SUBLANES, width), NEG_INF, F32)
                        continue
                    s = _dot(kext, qt[a, :, q0:q0 + width])
                    if diagonal and (i + 1) * ck - 1 > q0:
                        r = lax.broadcasted_iota(jnp.int32, (ck, width), 0) + i * ck
                        c = lax.broadcasted_iota(jnp.int32, (ck, width), 1) + q0
                        s = jnp.where(r <= c, s, NEG_INF)
                    cm = _sublane_allreduce(_fold_rows(s, jnp.maximum), jnp.maximum)
                    e_buf[slot, a, i * ck:(i + 1) * ck, q0:q0 + width] = jnp.exp2(
                        s - jnp.tile(cm, (ck // SUBLANES, 1))).astype(BF16)
                    cm_buf[slot, a, i, :, q0:q0 + width] = cm

    def stage_c(w, ms, slot):
        m_floor = jnp.where(j_tab[w] == 0, NEG_INF, big)
        new_ms = []
        for a in range(n_heads):
            m_old = jnp.minimum(ms[a], m_floor)
            cms = [cm_buf[slot, a, i] for i in range(n_ck)]
            m_new = functools.reduce(jnp.maximum, cms, m_old)
            n_rep = acc.shape[1] // SUBLANES
            total = acc[a] * jnp.tile(jnp.exp2(m_old - m_new), (n_rep, 1))
            for i in range(n_ck):
                vt = jnp.concatenate([vt_ref[0, j_tab[w], pair_lanes(a), i * ck:(i + 1) * ck], ones_rows], axis=0)
                pv = _dot(vt, e_buf[slot, a, i * ck:(i + 1) * ck, :])
                total = total + pv * jnp.tile(jnp.exp2(cms[i] - m_new), (n_rep, 1))
            acc[a] = total
            new_ms.append(m_new)
        return tuple(new_ms)

    def finalize(w):
        outs = []
        for a in range(n_heads):
            full = acc[a]
            denom = jnp.tile(full[PAIR_LANES:PAIR_LANES + SUBLANES], (PAIR_LANES // SUBLANES, 1))
            outs.append(full[:PAIR_LANES] / denom)
        row = lax.broadcasted_iota(jnp.int32, (PAIR_LANES, t), 0)
        rows = pl.ds(pl.multiple_of(qi_tab[w] * t, t), t)
        for a in range(0, n_heads, 2):
            o = jnp.transpose(jnp.where(row < FOX_HEAD_DIM, outs[a], outs[a + 1]))
            o_ref[rows, pair_lanes(a)] = o.astype(o_ref.dtype)

    load_q(0)
    stage_a(0, 0, True)

    def item(w, ms, slot):
        nxt = jnp.minimum(w + 1, n_items - 1)

        @pl.when(j_tab[nxt] == 0)
        def _():
            load_q(qi_tab[nxt])

        def stages(diagonal, ms):
            stage_a(nxt, 1 - slot, diagonal)
            ms = stage_c(w, ms, slot)
            return ms

        ms = lax.cond(j_tab[nxt] == qi_tab[nxt], functools.partial(stages, True),
                      functools.partial(stages, False), ms)

        @pl.when(j_tab[w] == qi_tab[w])
        def _():
            finalize(w)

        return ms

    def body(i, ms):
        return item(2 * i + 1, item(2 * i, ms, 0), 1)

    ms = lax.fori_loop(0, n_items // 2, body, tuple(jnp.full((SUBLANES, t), NEG_INF, F32) for _ in range(n_heads)))
    if n_items % 2:
        item(n_items - 1, ms, 0)


def _attn_prompt(qt, kb, kc, vt, c3t, n_seq, seq_len, t):
    total = kb.shape[0]
    nt = seq_len // t
    items = [(qi, j) for qi in range(nt) for j in range(qi + 1)]
    qi_tab = jnp.asarray([qi for qi, _ in items], jnp.int32)
    j_tab = jnp.asarray([j for _, j in items], jnp.int32)
    nh = 2 * ATTN_PAIRS_PER_STEP
    width = ATTN_PAIRS_PER_STEP * PAIR_LANES
    once = pl.Buffered(1)
    seq_in = pl.BlockSpec((seq_len, width), lambda b, p, *_: (b, p), pipeline_mode=once)
    seq_bias = pl.BlockSpec((seq_len, LANES), lambda b, p, *_: (b, 0), pipeline_mode=once)
    vt_in = pl.BlockSpec((1, nt, width, t), lambda b, p, *_: (b, 0, p, 0), pipeline_mode=once)
    c3t_in = pl.BlockSpec((1, nt, LANES, t), lambda b, p, *_: (b, 0, 0, 0), pipeline_mode=once)
    return pl.pallas_call(
        functools.partial(_attn_prompt_kernel, t, len(items)),
        grid_spec=pltpu.PrefetchScalarGridSpec(
            num_scalar_prefetch=2,
            grid=(n_seq, HEAD_PAIRS // ATTN_PAIRS_PER_STEP),
            in_specs=[vt_in, c3t_in, seq_in, seq_bias, vt_in],
            out_specs=pl.BlockSpec((seq_len, width), lambda b, p, *_: (b, p), pipeline_mode=once),
            scratch_shapes=[pltpu.VMEM((nh, 2 * LANES, t), BF16), pltpu.VMEM((2, nh, t, t), BF16),
                            pltpu.VMEM((2, nh, t // KEY_CHUNK, SUBLANES, t), F32),
                            pltpu.VMEM((nh, PAIR_LANES + ONES_ROWS, t), F32)]),
        out_shape=jax.ShapeDtypeStruct((total, D_MODEL), BF16),
        compiler_params=_cparams(2), name="fox_attn_prompt",
    )(qi_tab, j_tab, qt, c3t, kb, kc, vt)


def _attn_sample_kernel(n_cache_tiles, tk, t_new, q_ref, c3q_ref, kc_ref, vc_ref, lfc_ref,
                        kn_ref, vn_ref, lfn_ref, o_ref, qt, m_scr, acc_scr, carry, s_buf, p_buf, al_buf):
    j = pl.program_id(1)
    rows = 2 * t_new

    @pl.when(j == 0)
    def _():
        c3q = c3q_ref[...].astype(F32)
        for p in range(HEAD_PAIRS):
            q = q_ref[:, p * PAIR_LANES:(p + 1) * PAIR_LANES]
            qe = jnp.concatenate([_query_ext(q, c3q, 2 * p, True), _query_ext(q, c3q, 2 * p + 1, False)], axis=0)
            qt[p] = jnp.transpose(qe.astype(F32)).astype(BF16)
        m_scr[...] = jnp.full(m_scr.shape, NEG_INF, F32)
        acc_scr[...] = jnp.zeros(acc_scr.shape, F32)
        carry[...] = jnp.zeros_like(carry)

    def attend(k_pair, v_pair, n_keys, c3k, masked):
        kc = _key_bias_lanes(c3k)
        for p in range(HEAD_PAIRS):
            s = _dot(jnp.concatenate([k_pair(p), kc], axis=1), qt[p])
            if masked:
                r = lax.broadcasted_iota(jnp.int32, (n_keys, rows), 0)
                c = lax.broadcasted_iota(jnp.int32, (n_keys, rows), 1) & (t_new - 1)
                s = jnp.where(r <= c, s, NEG_INF)
            s_buf[p, :n_keys] = s
        for p in range(HEAD_PAIRS):
            s = s_buf[p, :n_keys]
            m_old = m_scr[p]
            m_new = jnp.maximum(m_old, _sublane_allreduce(_fold_rows(s, jnp.maximum), jnp.maximum))
            p_buf[p, :n_keys] = jnp.exp2(s - jnp.tile(m_new, (n_keys // SUBLANES, 1))).astype(BF16)
            alpha = jnp.exp2(m_old - m_new)
            al_buf[p] = jnp.transpose(jnp.tile(alpha, (rows // SUBLANES, 1)))
            m_scr[p] = m_new
        ones = jnp.ones((n_keys, LANES), BF16)
        for p in range(HEAD_PAIRS):
            pv = lax.dot_general(p_buf[p, :n_keys], jnp.concatenate([v_pair(p), ones], axis=1),
                                 (((0,), (0,)), ((), ())), preferred_element_type=F32)
            acc_scr[p] = acc_scr[p] * jnp.tile(al_buf[p], (1, 2)) + pv

    def cache_pair(ref):
        return lambda p: ref[0, :, p * PAIR_LANES:(p + 1) * PAIR_LANES].astype(BF16)

    def new_pair(ref):
        return lambda p: ref[:, p * PAIR_LANES:(p + 1) * PAIR_LANES]

    @pl.when(j < n_cache_tiles)
    def _():
        c = _tri_cumsum(lfc_ref[0], tk) + carry[0:1, :]
        carry[...] = jnp.broadcast_to(c[tk - 1:tk, :], carry.shape)
        attend(cache_pair(kc_ref), cache_pair(vc_ref), tk, _c_pieces(c), False)

    @pl.when(j == n_cache_tiles)
    def _():
        c = _tri_cumsum(lfn_ref[...], t_new) + carry[0:1, :]
        attend(new_pair(kn_ref), new_pair(vn_ref), t_new, _c_pieces(c), True)
        lane = lax.broadcasted_iota(jnp.int32, (t_new, LANES), 1)
        for p in range(HEAD_PAIRS):
            acc = acc_scr[p]
            oa = acc[:t_new, :LANES] / acc[:t_new, LANES:]
            ob = acc[t_new:, :LANES] / acc[t_new:, LANES:]
            o_ref[:, p * PAIR_LANES:(p + 1) * PAIR_LANES] = jnp.where(lane < FOX_HEAD_DIM, oa, ob).astype(o_ref.dtype)


def _attn_sample(qb, kb, vb, c3, logf, cache_k, cache_v, cache_logf, tk):
    n_seq, past, _ = cache_k.shape
    t_new = qb.shape[0] // n_seq
    nct = past // tk
    pad = lambda a: jnp.pad(a, ((0, 0),) * (a.ndim - 1) + ((0, LANES - FOX_HEADS),))
    new_tile = pl.BlockSpec((t_new, D_MODEL), lambda b, j: (b, 0))
    new_c3 = pl.BlockSpec((t_new, LANES), lambda b, j: (b, 0))
    cache_tile = pl.BlockSpec((1, tk, D_MODEL), lambda b, j: (b, jnp.minimum(j, nct - 1), 0))
    lf_tile = pl.BlockSpec((1, tk, LANES), lambda b, j: (b, jnp.minimum(j, nct - 1), 0))
    return pl.pallas_call(
        functools.partial(_attn_sample_kernel, nct, tk, t_new),
        grid=(n_seq, nct + 1),
        in_specs=[new_tile, new_c3, cache_tile, cache_tile, lf_tile, new_tile, new_tile, new_c3],
        out_specs=new_tile,
        out_shape=jax.ShapeDtypeStruct((n_seq * t_new, D_MODEL), BF16),
        scratch_shapes=[pltpu.VMEM((HEAD_PAIRS, 2 * LANES, 2 * t_new), BF16),
                        pltpu.VMEM((HEAD_PAIRS, SUBLANES, 2 * t_new), F32),
                        pltpu.VMEM((HEAD_PAIRS, 2 * t_new, 2 * LANES), F32),
                        pltpu.VMEM((8, LANES), F32),
                        pltpu.VMEM((HEAD_PAIRS, tk, 2 * t_new), F32),
                        pltpu.VMEM((HEAD_PAIRS, tk, 2 * t_new), BF16),
                        pltpu.VMEM((HEAD_PAIRS, 2 * t_new, 2 * t_new), F32)],
        compiler_params=_cparams(2), name="fox_attn_sample",
    )(qb, c3, cache_k, cache_v, pad(cache_logf), kb, vb, pad(logf))


def _trunk(x, n_seq, seq_len, ssm_h0, fox_cache, conv_hist, w, tm, tq):
    t = n_seq * seq_len
    nb = seq_len // SSM_BLOCK
    nc = SSM_GROUPS // COLUMN_GROUPS
    half = COLUMN_GROUPS * SSM_STATE
    x = x.reshape(t, D_MODEL)

    u0 = _norm(x, w["norm_mix"][0], tm)
    if ssm_h0 is None:
        h0 = None
    else:
        per_c = lambda h: jnp.swapaxes(h.reshape(n_seq, nc, half), 0, 1)
        h0c = jnp.concatenate([per_c(ssm_h0[0]), per_c(ssm_h0[1])], axis=2)
        h0 = jnp.zeros((nc, n_seq, nb, 2 * half), F32).at[:, :, 0, :].set(h0c).reshape(nc, n_seq * nb, 2 * half)
    ys, fin = _s5_core(u0, w["s5"], n_seq, seq_len, h0, fold_sequences=ssm_h0 is not None)
    per_g = lambda f: jnp.swapaxes(f.reshape(nc, n_seq, COLUMN_GROUPS, SSM_STATE), 0, 1).reshape(
        n_seq, SSM_GROUPS, SSM_STATE)
    fre, fim = per_g(fin[:, :, :half]), per_g(fin[:, :, half:])
    x, conv0 = _ffn(x, "glu", (ys, w["norm_mix"][0], w["ssm_d"], w["w_glu"]), conv_hist[0],
                    w["norm_ffn"][0], w["w_up"][0], w["w_gate"][0], w["conv_w"][0], w["conv_b"][0],
                    w["w_down"][0], None, seq_len, tm)

    k, v, logf, qb, kb, vb, c3, kc = _qkv(x, w["norm_mix"][1], w["wq"], w["wk"], w["wv"], w["wf"], w["bf"],
                                          seq_len, tq, transposed_v=fox_cache is None)
    if fox_cache is None:
        o = _attn_prompt(qb, kb, kc, vb, c3, n_seq, seq_len, tq)
    else:
        o = _attn_sample(qb, kb, vb, c3, logf, *fox_cache, tk=min(CACHE_TILE, fox_cache[0].shape[1]))
    y, conv1 = _ffn(x, "oproj", (o, w["w_o"]), conv_hist[1],
                    w["norm_ffn"][1], w["w_up"][1], w["w_gate"][1], w["conv_w"][1], w["conv_b"][1],
                    w["w_down"][1], w["norm_final"], seq_len, tm)

    shape4 = (1, n_seq, seq_len, FOX_HEADS, FOX_HEAD_DIM)
    return (y.reshape(n_seq, seq_len, D_MODEL),
            fre[None], fim[None],
            k.reshape(shape4), v.reshape(shape4), logf.reshape(1, n_seq, seq_len, FOX_HEADS),
            jnp.stack([conv0, conv1]))


def kernel(x_prompt, x_sample, state_ssm_re, state_ssm_im, cache_fox_k, cache_fox_v, cache_fox_logf, state_ffn_conv, norm_mix, norm_ffn, norm_final, ssm_a_re, ssm_a_im, ssm_log_step, ssm_b_re, ssm_b_im, ssm_c_re, ssm_c_im, ssm_d, ssm_w_glu, fox_w_qkvf, fox_b_f, fox_w_o, ffn_w_up, ffn_w_gate, ffn_conv_w, ffn_conv_b, ffn_w_down):
    n_p, l_p, _ = x_prompt.shape
    n_s, l_s, _ = x_sample.shape
    past = cache_fox_k.shape[2]
    wqkvf = fox_w_qkvf[0]
    w = {
        "norm_mix": norm_mix, "norm_ffn": norm_ffn, "norm_final": norm_final,
        "s5": _s5_block_diagonal(_s5_params(ssm_a_re[0], ssm_a_im[0], ssm_log_step[0], ssm_b_re[0],
                                            ssm_b_im[0], ssm_c_re[0], ssm_c_im[0])),
        "ssm_d": ssm_d[0], "w_glu": ssm_w_glu[0].astype(BF16),
        "wq": wqkvf[:, :D_MODEL].astype(BF16), "wk": wqkvf[:, D_MODEL:2 * D_MODEL].astype(BF16),
        "wv": wqkvf[:, 2 * D_MODEL:3 * D_MODEL].astype(BF16),
        "wf": jnp.pad(wqkvf[:, 3 * D_MODEL:], ((0, 0), (0, LANES - FOX_HEADS))).astype(BF16),
        "bf": jnp.pad(fox_b_f[0], (0, LANES - FOX_HEADS))[None, :],
        "w_o": fox_w_o[0].astype(BF16),
        "w_up": ffn_w_up.astype(BF16), "w_gate": ffn_w_gate.astype(BF16), "w_down": ffn_w_down.astype(BF16),
        "conv_w": ffn_conv_w, "conv_b": ffn_conv_b,
    }
    zero_hist = jnp.zeros((2, n_p, CONV_WIDTH - 1, D_FF), F32)
    outs_p = _trunk(x_prompt, n_p, l_p, None, None, zero_hist, w, tm=PROMPT_TILE, tq=PROMPT_TILE)
    cache = (cache_fox_k[0].reshape(n_s, past, D_MODEL), cache_fox_v[0].reshape(n_s, past, D_MODEL),
             cache_fox_logf[0])
    outs_s = _trunk(x_sample, n_s, l_s, (state_ssm_re[0], state_ssm_im[0]), cache, state_ffn_conv, w,
                    tm=SAMPLE_SEQS_PER_TILE * l_s, tq=l_s)
    return (outs_p[0], outs_s[0]) + outs_p[1:] + outs_s[1:]
```

```python
import functools
import math

import jax
import jax.numpy as jnp
from jax import lax
from jax.experimental import pallas as pl
from jax.experimental.pallas import tpu as pltpu

D_MODEL = 1024
SSM_GROUP = 16
SSM_GROUPS = D_MODEL // SSM_GROUP
SSM_STATE = 64
SSM_BLOCK = 16
SSM_BLOCK_LOG2 = 4
BLOCK_LANES = SSM_BLOCK * SSM_GROUP
COLUMN_LANES = 128
COLUMN_GROUPS = COLUMN_LANES // SSM_GROUP
STEP_PAIRS = SSM_BLOCK // 2
FOX_HEADS = 16
FOX_HEAD_DIM = 64
HEAD_PAIRS = FOX_HEADS // 2
PAIR_LANES = 2 * FOX_HEAD_DIM
D_FF = 2816
CONV_WIDTH = 3
PROMPT_TILE = 512
SAMPLE_SEQS_PER_TILE = 4
CACHE_TILE = 512
FFN_CHUNK_EDGES = (0, 1536, D_FF)
NORM_EPS = 1e-6
NEG_INF = -1e30
LOG2E = 1.4426950408889634
LANES = 128
SUBLANES = 8
_CQ_LANE = 3 * FOX_HEADS
VMEM_LIMIT = 56 * 1024 * 1024

BF16 = jnp.bfloat16
F32 = jnp.float32


def _cparams(n_axes, vmem=VMEM_LIMIT):
    return pltpu.CompilerParams(dimension_semantics=("arbitrary",) * n_axes, vmem_limit_bytes=vmem)


def _full(shape):
    nd = len(shape)
    return pl.BlockSpec(shape, lambda *_: (0,) * nd)


def _rms(x, g):
    ms = jnp.mean(x * x, axis=-1, keepdims=True)
    return x * lax.rsqrt(ms + NORM_EPS) * g


def _gelu(x):
    c = math.sqrt(2.0 / math.pi)
    return 0.5 * x * (1.0 + jnp.tanh(c * (x + 0.044715 * (x * x * x))))


def _sigmoid(x):
    return 1.0 / (1.0 + jnp.exp(-x))


def _log_sigmoid(x):
    return jnp.minimum(x, 0.0) - jnp.log1p(jnp.exp(-jnp.abs(x)))


def _split3(x):
    hi = x.astype(BF16)
    r1 = x - hi.astype(F32)
    mid = r1.astype(BF16)
    lo = (r1 - mid.astype(F32)).astype(BF16)
    return hi, mid, lo


def _dot(a, b):
    return jnp.dot(a, b, preferred_element_type=F32)


def _dot_nt(a, b):
    return lax.dot_general(a, b, (((1,), (1,)), ((), ())), preferred_element_type=F32)


def _dot_nt_f32(a, b):
    a0, a1, a2 = _split3(a)
    b0, b1, b2 = _split3(b)
    return (_dot_nt(a0, b0) + (_dot_nt(a0, b1) + _dot_nt(a1, b0))
            + (_dot_nt(a0, b2) + _dot_nt(a1, b1) + _dot_nt(a2, b0)))


def _tri_cumsum(x, n):
    r = lax.broadcasted_iota(jnp.int32, (n, n), 0)
    c = lax.broadcasted_iota(jnp.int32, (n, n), 1)
    tri = jnp.where(c <= r, 1.0, 0.0).astype(BF16)
    hi, mid, lo = _split3(x)
    return _dot(tri, hi) + _dot(tri, mid) + _dot(tri, lo)


def _s5_param_kernel(are_r, aim_r, ls_ref, btre, btim, ctre, ctim,
                     k_ref, wre_ref, wim_ref, zre_ref, zim_ref, mu_ref):
    dt = jnp.exp(ls_ref[0])
    lre, lim = are_r[0], aim_r[0]
    mag = jnp.exp(lre * dt)
    abr = mag * jnp.cos(lim * dt)
    abi = mag * jnp.sin(lim * dt)
    den = lre * lre + lim * lim
    nr = abr - 1.0
    zre = (nr * lre + abi * lim) / den
    zim = (abi * lre - nr * lim) / den

    def cpow(e):
        m = jnp.exp(lre * dt * e)
        a = lim * dt * e
        return m * jnp.cos(a), m * jnp.sin(a)

    def per_step(e):
        step = lax.broadcasted_iota(jnp.int32, (SSM_BLOCK, SSM_STATE), 0).astype(F32)
        pr, pi = cpow(e(step))
        rep = lambda t: jnp.broadcast_to(t[:, None, :], (SSM_BLOCK, SSM_GROUP, SSM_STATE)).reshape(
            BLOCK_LANES, SSM_STATE)
        return rep(pr), rep(pi)

    bre, bim = btre[0], btim[0]
    bbre = zre * bre - zim * bim
    bbim = zre * bim + zim * bre
    pwr, pwi = per_step(lambda s: (SSM_BLOCK - 1.0) - s)
    wre_ref[0] = (pwr * bbre - pwi * bbim).astype(BF16)
    wim_ref[0] = (pwr * bbim + pwi * bbre).astype(BF16)
    pbr, pbi = per_step(lambda s: -s)
    bxre = pbr * bbre - pbi * bbim
    bxim = pbr * bbim + pbi * bbre

    cre, cim = ctre[0], ctim[0]
    pcr, pci = per_step(lambda t: t)
    cxre = pcr * cre - pci * cim
    cxim = pcr * cim + pci * cre
    pzr, pzi = per_step(lambda t: t + 1.0)
    zre_ref[0] = (pzr * cre - pzi * cim).astype(BF16)
    zim_ref[0] = (-(pzr * cim + pzi * cre)).astype(BF16)

    kfull = _dot_nt_f32(bxre, cxre) - _dot_nt_f32(bxim, cxim)
    rs = lax.broadcasted_iota(jnp.int32, (BLOCK_LANES, BLOCK_LANES), 0) >> SSM_BLOCK_LOG2
    ct = lax.broadcasted_iota(jnp.int32, (BLOCK_LANES, BLOCK_LANES), 1) >> SSM_BLOCK_LOG2
    k_ref[0] = jnp.where(ct >= rs, kfull, 0.0).astype(BF16)

    mur, mui = cpow(float(SSM_BLOCK))
    mu_ref[0, 0:1, :] = mur
    mu_ref[0, 1:2, :] = mui


def _s5_params(a_re, a_im, log_step, b_re, b_im, c_re, c_im):
    g, p = a_re.shape
    bt = jnp.tile(jnp.swapaxes(b_re, 1, 2), (1, SSM_BLOCK, 1)), jnp.tile(jnp.swapaxes(b_im, 1, 2), (1, SSM_BLOCK, 1))
    ct = jnp.tile(c_re, (1, SSM_BLOCK, 1)), jnp.tile(c_im, (1, SSM_BLOCK, 1))
    row = pl.BlockSpec((1, 1, p), lambda i: (i, 0, 0))
    one = pl.BlockSpec((1, 1, 1), lambda i: (i, 0, 0))
    tall = pl.BlockSpec((1, BLOCK_LANES, p), lambda i: (i, 0, 0))
    return pl.pallas_call(
        _s5_param_kernel,
        grid=(g,),
        in_specs=[row, row, one, tall, tall, tall, tall],
        out_specs=[pl.BlockSpec((1, BLOCK_LANES, BLOCK_LANES), lambda i: (i, 0, 0)),
                   tall, tall, tall, tall, pl.BlockSpec((1, 2, p), lambda i: (i, 0, 0))],
        out_shape=[jax.ShapeDtypeStruct((g, BLOCK_LANES, BLOCK_LANES), BF16)]
        + [jax.ShapeDtypeStruct((g, BLOCK_LANES, p), BF16)] * 4
        + [jax.ShapeDtypeStruct((g, 2, p), F32)],
        compiler_params=_cparams(1),
        name="s5_params",
    )(a_re[:, None, :], a_im[:, None, :], log_step[:, None, None], bt[0], bt[1], ct[0], ct[1])


def _norm_kernel(x_ref, g_ref, u_ref):
    u_ref[...] = _rms(x_ref[...], g_ref[...]).astype(u_ref.dtype)


def _norm(x, g, tm):
    t = x.shape[0]
    tile = pl.BlockSpec((tm, D_MODEL), lambda i: (i, 0))
    return pl.pallas_call(
        _norm_kernel, grid=(t // tm,),
        in_specs=[tile, _full((1, D_MODEL))], out_specs=tile,
        out_shape=jax.ShapeDtypeStruct((t, D_MODEL), F32),
        compiler_params=_cparams(1), name="s5_prenorm",
    )(x, g[None, :])


def _s5_core_kernel(nb, n_seq, has_h0, *refs):
    if has_h0:
        u_ref, kk_ref, ww_ref, zz_ref, mu_ref, h0_ref, y_ref, f_ref, s_scr = refs
    else:
        u_ref, kk_ref, ww_ref, zz_ref, mu_ref, y_ref, f_ref, s_scr = refs
    rows = nb * n_seq
    half = COLUMN_GROUPS * SSM_STATE
    xp = [jnp.concatenate([u_ref[pl.ds(2 * p, rows, stride=SSM_BLOCK), :],
                           u_ref[pl.ds(2 * p + 1, rows, stride=SSM_BLOCK), :]], axis=1).astype(BF16)
          for p in range(STEP_PAIRS)]
    s = _dot(xp[0], ww_ref[0, 0])
    for p in range(1, STEP_PAIRS):
        s = s + _dot(xp[p], ww_ref[0, p])
    sre, sim = s[:, :half], s[:, half:]
    mu = mu_ref[0]
    mre, mim = mu[0:1, :], mu[1:2, :]
    blk = lax.broadcasted_iota(jnp.int32, (rows, half), 0) & (nb - 1)
    if has_h0:
        h0 = h0_ref[0]
        h0re, h0im = h0[:, :half], h0[:, half:]
        sre = sre + (mre * h0re - mim * h0im)
        sim = sim + (mre * h0im + mim * h0re)
    inner = min(nb, SUBLANES)
    pos = lax.broadcasted_iota(jnp.int32, (SUBLANES, half), 0) & (inner - 1)
    powers = []
    for _ in range(4):
        powers.append((mre, mim))
        mre, mim = mre * mre - mim * mim, 2.0 * mre * mim
    d = 1
    while d < inner:
        pr, pi = powers[d.bit_length() - 1]
        pr = jnp.tile(jnp.where(pos >= d, pr, 0.0), (rows // SUBLANES, 1))
        pi = jnp.tile(jnp.where(pos >= d, pi, 0.0), (rows // SUBLANES, 1))
        shr, shi = pltpu.roll(sre, d, 0), pltpu.roll(sim, d, 0)
        sre, sim = sre + (pr * shr - pi * shi), sim + (pr * shi + pi * shr)
        d *= 2
    if nb > SUBLANES:
        sub = lax.broadcasted_iota(jnp.int32, (SUBLANES, half), 0) + 1
        wre, wim = jnp.ones((SUBLANES, half), F32), jnp.zeros((SUBLANES, half), F32)
        for k, (pr, pi) in enumerate(powers):
            bit = ((sub >> k) & 1) == 1
            fr, fi = jnp.where(bit, pr, 1.0), jnp.where(bit, pi, 0.0)
            wre, wim = wre * fr - wim * fi, wre * fi + wim * fr
        groups = nb // SUBLANES
        out_re, out_im = [], []
        for j in range(rows // SUBLANES):
            lre, lim_ = sre[j * SUBLANES:(j + 1) * SUBLANES], sim[j * SUBLANES:(j + 1) * SUBLANES]
            if j % groups:
                lre, lim_ = lre + (wre * cre - wim * cim), lim_ + (wre * cim + wim * cre)
            cre = jnp.broadcast_to(lre[SUBLANES - 1:SUBLANES, :], (SUBLANES, half))
            cim = jnp.broadcast_to(lim_[SUBLANES - 1:SUBLANES, :], (SUBLANES, half))
            out_re.append(lre)
            out_im.append(lim_)
        sre, sim = jnp.concatenate(out_re, axis=0), jnp.concatenate(out_im, axis=0)
    n_chunks = half // LANES
    for i in range(n_chunks):
        s_scr[i] = sre[:, i * LANES:(i + 1) * LANES]
        s_scr[n_chunks + i] = sim[:, i * LANES:(i + 1) * LANES]
    f_ref[0, 0] = jnp.concatenate([s_scr[i, pl.ds(nb - 1, n_seq, stride=nb), :] for i in range(2 * n_chunks)],
                                  axis=1)
    first = blk == 0
    inre = jnp.where(first, 0.0, pltpu.roll(sre, 1, 0))
    inim = jnp.where(first, 0.0, pltpu.roll(sim, 1, 0))
    if has_h0:
        inre = inre + h0re
        inim = inim + h0im
    s_in = jnp.concatenate([inre, inim], axis=1).astype(BF16)
    for p in range(STEP_PAIRS):
        y = _dot_nt(s_in, zz_ref[0, p])
        for q in range(p + 1):
            y = y + _dot(xp[q], kk_ref[0, p - q])
        y_ref[pl.ds(2 * p, rows, stride=SSM_BLOCK), :] = y[:, :COLUMN_LANES]
        y_ref[pl.ds(2 * p + 1, rows, stride=SSM_BLOCK), :] = y[:, COLUMN_LANES:]


def _s5_block_diagonal(params):
    kmat, wre, wim, zre, zim, mu = params
    cg, nc = COLUMN_GROUPS, SSM_GROUPS // COLUMN_GROUPS
    rows = 2 * COLUMN_LANES
    grp = lambda n, width: (jnp.arange(n) // width) % cg

    def stack_rows(m, pairs_on_rows):
        minor = m.shape[-1]
        if pairs_on_rows:
            m6 = m.reshape(nc, cg, STEP_PAIRS, 2, SSM_GROUP, minor)
            return jnp.transpose(m6, (0, 2, 3, 1, 4, 5)).reshape(nc, STEP_PAIRS, rows, minor)
        m6 = m.reshape(nc, cg, 2, SSM_GROUP, STEP_PAIRS, minor // STEP_PAIRS)
        return jnp.transpose(m6, (0, 4, 2, 1, 3, 5)).reshape(nc, STEP_PAIRS, rows, minor // STEP_PAIRS)

    def spread(stacked, width):
        n_in = stacked.shape[-1]
        n_out = n_in * cg
        src = (jnp.arange(n_out) // (width * cg)) * width + jnp.arange(n_out) % width
        place = (jnp.arange(n_in)[:, None] == src[None, :]).astype(BF16)
        own = (grp(rows, SSM_GROUP)[:, None] == grp(n_out, width)[None, :]).astype(BF16)
        return jnp.einsum('cprk,kn->cprn', stacked, place, preferred_element_type=BF16) * own

    kk = spread(stack_rows(kmat[:, :2 * SSM_GROUP, :], False), SSM_GROUP)
    ww = spread(stack_rows(jnp.concatenate([wre, wim], axis=2), True), SSM_STATE)
    zz_t = spread(stack_rows(jnp.concatenate([zre, zim], axis=2), True), SSM_STATE)
    mu_c = jnp.transpose(mu.reshape(nc, cg, 2, SSM_STATE), (0, 2, 1, 3)).reshape(nc, 2, cg * SSM_STATE)
    return kk, ww, zz_t, mu_c


def _s5_core(u, bd, n_seq, seq_len, h0, fold_sequences):
    kk, ww, zz, mu_c = bd
    nc = kk.shape[0]
    nb = seq_len // SSM_BLOCK
    seqs = n_seq if fold_sequences else 1
    n_steps = n_seq // seqs
    rows_tok = seqs * seq_len
    rows = seqs * nb
    state = 2 * COLUMN_GROUPS * SSM_STATE
    col = pl.BlockSpec((rows_tok, COLUMN_LANES), lambda c, i: (i, c))
    per_c = lambda a: pl.BlockSpec((1,) + a.shape[1:], lambda c, i: (c,) + (0,) * (a.ndim - 1))
    in_specs = [col, per_c(kk), per_c(ww), per_c(zz), per_c(mu_c)]
    args = [u, kk, ww, zz, mu_c]
    if h0 is not None:
        in_specs.append(pl.BlockSpec((1, rows, state), lambda c, i: (c, i, 0)))
        args.append(h0)
    y, fin = pl.pallas_call(
        functools.partial(_s5_core_kernel, nb, seqs, h0 is not None),
        grid=(nc, n_steps),
        in_specs=in_specs,
        out_specs=[col, pl.BlockSpec((1, 1, seqs, state), lambda c, i: (c, i, 0, 0))],
        out_shape=[jax.ShapeDtypeStruct(u.shape, F32), jax.ShapeDtypeStruct((nc, n_steps, seqs, state), F32)],
        scratch_shapes=[pltpu.VMEM((state // LANES, rows, LANES), F32)],
        compiler_params=_cparams(2), name="s5_core",
    )(*args)
    return y, fin.reshape(nc, n_seq, state)


def _ffn_kernel(pre, final_norm, tiles_per_seq, seqs_per_tile, tm, *refs):
    refs = list(refs)
    x_ref = refs.pop(0)
    if pre == "glu":
        ys_ref, gmix_ref, dskip_ref, wglu_ref = refs[:4]
        refs = refs[4:]
    else:
        o_ref, wo_ref = refs[:2]
        refs = refs[2:]
    (hist_ref, gffn_ref, wup_ref, wgate_ref, convw_ref, convb_ref, wdown_ref) = refs[:7]
    refs = refs[7:]
    if final_norm:
        gfin_ref = refs.pop(0)
    out_ref, cstate_ref, carry = refs
    chained = seqs_per_tile == 0

    if chained:
        @pl.when(pl.program_id(0) % tiles_per_seq == 0)
        def _():
            carry[...] = hist_ref[0]

    x = x_ref[...]
    if pre == "glu":
        u = _rms(x, gmix_ref[...])
        g = _gelu(ys_ref[...] + dskip_ref[...] * u).astype(BF16)
        z = _dot(g, wglu_ref[...])
        x = x + z[:, :D_MODEL] * _sigmoid(z[:, D_MODEL:])
    else:
        x = x + _dot(o_ref[...], wo_ref[...])
    h = _rms(x, gffn_ref[...]).astype(BF16)

    seqs = max(1, seqs_per_tile)
    span = tm // seqs
    for lo, hi in zip(FFN_CHUNK_EDGES[:-1], FFN_CHUNK_EDGES[1:]):
        cs = slice(lo, hi)
        row = lax.broadcasted_iota(jnp.int32, (tm, hi - lo), 0)
        a = _dot(h, wup_ref[:, cs])
        a1, a2 = pltpu.roll(a, 1, 0), pltpu.roll(a, 2, 0)
        for s in range(seqs):
            prev = carry[:, cs] if chained else hist_ref[s, :, cs]
            a1 = jnp.where(row == s * span, prev[1:2, :], a1)
            a2 = jnp.where(row == s * span, prev[0:1, :], jnp.where(row == s * span + 1, prev[1:2, :], a2))
            cstate_ref[s, :, cs] = a[(s + 1) * span - (CONV_WIDTH - 1):(s + 1) * span, :]
        conv = a2 * convw_ref[0:1, cs] + a1 * convw_ref[1:2, cs] + a * convw_ref[2:3, cs] + convb_ref[:, cs]
        if chained:
            carry[:, cs] = a[tm - (CONV_WIDTH - 1):, :]
        hh = (_gelu(conv) * _dot(h, wgate_ref[:, cs])).astype(BF16)
        x = x + _dot(hh, wdown_ref[cs, :])
    if final_norm:
        x = _rms(x, gfin_ref[...])
    out_ref[...] = x


def _ffn(x, pre, pre_args, hist, g_ffn, w_up, w_gate, conv_w, conv_b, w_down, g_final, seq_len, tm):
    t = x.shape[0]
    n_seq = t // seq_len
    tiles_per_seq = max(1, seq_len // tm)
    seqs_per_tile = 0 if tm < seq_len else tm // seq_len
    tile = pl.BlockSpec((tm, D_MODEL), lambda i: (i, 0))
    vec = _full((1, D_MODEL))
    const = lambda a: pl.BlockSpec(a.shape, lambda i: (0,) * a.ndim, pipeline_mode=pl.Buffered(1))
    per_seq = pl.BlockSpec((max(1, seqs_per_tile), CONV_WIDTH - 1, D_FF), lambda i: (i // tiles_per_seq, 0, 0))
    args, in_specs = [x], [tile]
    if pre == "glu":
        ys, g_mix, d_skip, w_glu = pre_args
        args += [ys, g_mix[None, :], d_skip[None, :], w_glu]
        in_specs += [tile, vec, vec, const(w_glu)]
    else:
        o, w_o = pre_args
        args += [o, w_o]
        in_specs += [tile, const(w_o)]
    args += [hist, g_ffn[None, :], w_up, w_gate, conv_w, conv_b[None, :], w_down]
    in_specs += [per_seq, vec, const(w_up), const(w_gate), _full((CONV_WIDTH, D_FF)), _full((1, D_FF)),
                 const(w_down)]
    if g_final is not None:
        args.append(g_final[None, :])
        in_specs.append(vec)
    return pl.pallas_call(
        functools.partial(_ffn_kernel, pre, g_final is not None, tiles_per_seq, seqs_per_tile, tm),
        grid=(t // tm,),
        in_specs=in_specs,
        out_specs=[tile, per_seq],
        out_shape=[jax.ShapeDtypeStruct((t, D_MODEL), F32),
                   jax.ShapeDtypeStruct((n_seq, CONV_WIDTH - 1, D_FF), F32)],
        scratch_shapes=[pltpu.VMEM((CONV_WIDTH - 1, D_FF), F32)],
        compiler_params=_cparams(1), name="ffn_" + pre,
    )(*args)


def _c_pieces(c):
    hi, mid, lo = _split3(c * LOG2E)
    return (hi.astype(F32) + pltpu.roll(mid.astype(F32), FOX_HEADS, 1)
            + pltpu.roll(lo.astype(F32), 2 * FOX_HEADS, 1)).astype(BF16)


def _key_bias_lanes(c3):
    lane = lax.broadcasted_iota(jnp.int32, c3.shape, 1)
    ones = ((lane >= _CQ_LANE) & (lane < _CQ_LANE + 3)).astype(F32)
    return (ones - c3.astype(F32)).astype(BF16)


def _qkv_kernel(tiles_per_seq, tm, transposed_v, x_ref, g_ref, wq_ref, wk_ref, wv_ref, wf_ref, bf_ref,
                k_ref, v_ref, lf_ref, qb_ref, kb_ref, vb_ref, c3_ref, kc_ref, carry):
    @pl.when(pl.program_id(0) % tiles_per_seq == 0)
    def _():
        carry[...] = jnp.zeros_like(carry)

    u = _rms(x_ref[...], g_ref[...]).astype(BF16)
    lane = lax.broadcasted_iota(jnp.int32, (tm, LANES), 1)
    logf = jnp.where(lane < FOX_HEADS, _log_sigmoid(_dot(u, wf_ref[...]) + bf_ref[...]), 0.0)
    lf_ref[...] = logf[:, :FOX_HEADS]
    c = _tri_cumsum(logf, tm) + carry[0:1, :]
    carry[...] = jnp.broadcast_to(c[tm - 1:tm, :], carry.shape)
    c3 = _c_pieces(c)
    if transposed_v:
        c3_ref[0, 0] = jnp.transpose(c3.astype(F32))
    else:
        c3_ref[...] = c3
    kc_ref[...] = _key_bias_lanes(c3)
    v = _dot(u, wv_ref[...])
    v_ref[...] = v
    q = _dot(u, wq_ref[...]) * (FOX_HEAD_DIM ** -0.5 * LOG2E)
    if transposed_v:
        vb_ref[0, 0] = jnp.transpose(v).astype(BF16)
        qb_ref[0, 0] = jnp.transpose(q).astype(BF16)
    else:
        vb_ref[...] = v.astype(BF16)
        qb_ref[...] = q.astype(BF16)
    k = _dot(u, wk_ref[...])
    k_ref[...] = k
    kb_ref[...] = k.astype(BF16)


def _qkv(x, g, wq, wk, wv, wf, bf, seq_len, tm, transposed_v):
    t = x.shape[0]
    tiles_per_seq = seq_len // tm
    tile = pl.BlockSpec((tm, D_MODEL), lambda i: (i, 0))
    narrow = pl.BlockSpec((tm, LANES), lambda i: (i, 0))
    const = lambda a: pl.BlockSpec(a.shape, lambda i: (0,) * a.ndim, pipeline_mode=pl.Buffered(1))
    wide_bf = jax.ShapeDtypeStruct((t, D_MODEL), BF16)
    if transposed_v:
        per_tile = lambda i: (i // tiles_per_seq, i % tiles_per_seq, 0, 0)
        qv_spec = pl.BlockSpec((1, 1, D_MODEL, tm), per_tile)
        qv_shape = jax.ShapeDtypeStruct((t // seq_len, tiles_per_seq, D_MODEL, tm), BF16)
        c3_spec = pl.BlockSpec((1, 1, LANES, tm), per_tile)
        c3_shape = jax.ShapeDtypeStruct((t // seq_len, tiles_per_seq, LANES, tm), F32)
    else:
        qv_spec, qv_shape = tile, wide_bf
        c3_spec, c3_shape = narrow, jax.ShapeDtypeStruct((t, LANES), BF16)
    return pl.pallas_call(
        functools.partial(_qkv_kernel, tiles_per_seq, tm, transposed_v),
        grid=(t // tm,),
        in_specs=[tile, _full((1, D_MODEL)), const(wq), const(wk), const(wv), const(wf), _full((1, LANES))],
        out_specs=[tile, tile, pl.BlockSpec((tm, FOX_HEADS), lambda i: (i, 0)), qv_spec, tile, qv_spec,
                   c3_spec, narrow],
        out_shape=[jax.ShapeDtypeStruct((t, D_MODEL), F32), jax.ShapeDtypeStruct((t, D_MODEL), F32),
                   jax.ShapeDtypeStruct((t, FOX_HEADS), F32),
                   qv_shape, wide_bf, qv_shape, c3_shape, jax.ShapeDtypeStruct((t, LANES), BF16)],
        scratch_shapes=[pltpu.VMEM((8, LANES), F32)],
        compiler_params=_cparams(1), name="fox_qkv",
    )(x, g[None, :], wq, wk, wv, wf, bf)


def _query_ext(q, c3q, head, lo_half):
    rows = q.shape[0]
    lane = lax.broadcasted_iota(jnp.int32, (rows, LANES), 1)
    own = (lane < FOX_HEAD_DIM) if lo_half else (lane >= FOX_HEAD_DIM)
    qm = jnp.where(own, q, jnp.zeros_like(q))
    pick = lambda j: jnp.sum(jnp.where(lane == head + j * FOX_HEADS, c3q, 0.0), axis=1, keepdims=True)
    onehot = (lane == head) | (lane == head + FOX_HEADS) | (lane == head + 2 * FOX_HEADS)
    qc = jnp.where(lane == _CQ_LANE, pick(0),
                   jnp.where(lane == _CQ_LANE + 1, pick(1),
                             jnp.where(lane == _CQ_LANE + 2, pick(2), jnp.where(onehot, 1.0, 0.0))))
    return jnp.concatenate([qm, qc.astype(BF16)], axis=1)


def _sublane_allreduce(x, op):
    for sh in (4, 2, 1):
        x = op(x, pltpu.roll(x, sh, 0))
    return x


def _fold_rows(x, op):
    parts = [x[i * SUBLANES:(i + 1) * SUBLANES] for i in range(x.shape[0] // SUBLANES)]
    while len(parts) > 1:
        parts = [op(parts[i], parts[i + 1]) for i in range(0, len(parts), 2)]
    return parts[0]


KEY_CHUNK = 256
ATTN_PAIRS_PER_STEP = 4
ONES_ROWS = 16


def _attn_prompt_kernel(t, n_items, qi_tab, j_tab, q_ref, c3_ref, k_ref, kc_ref, vt_ref, o_ref,
                        qt, e_buf, cm_buf, acc):
    n_heads = acc.shape[0]
    first_head = pl.program_id(1) * n_heads
    pair_lanes = lambda a: slice((a // 2) * PAIR_LANES, (a // 2 + 1) * PAIR_LANES)
    big = -NEG_INF
    ck = KEY_CHUNK
    n_ck = t // ck
    for a in range(n_heads):
        acc[a] = jnp.zeros(acc.shape[1:], F32)
    ones_rows = jnp.ones((ONES_ROWS, ck), BF16)

    def load_q(qi):
        row = lax.broadcasted_iota(jnp.int32, (PAIR_LANES, t), 0)
        for a in range(n_heads):
            head = first_head + a
            own = (row < FOX_HEAD_DIM) if a % 2 == 0 else (row >= FOX_HEAD_DIM)
            q_pair = q_ref[0, qi, pair_lanes(a), :]
            piece = lambda j: c3_ref[0, qi, pl.ds(head + j * FOX_HEADS, 1), :]
            onehot = (row == head) | (row == head + FOX_HEADS) | (row == head + 2 * FOX_HEADS)
            bias = jnp.where(row == _CQ_LANE, piece(0),
                             jnp.where(row == _CQ_LANE + 1, piece(1),
                                       jnp.where(row == _CQ_LANE + 2, piece(2), jnp.where(onehot, 1.0, 0.0))))
            qt[a] = jnp.concatenate([jnp.where(own, q_pair, jnp.zeros_like(q_pair)), bias.astype(BF16)], axis=0)

    def stage_a(w, slot, diagonal):
        start = pl.multiple_of(j_tab[w] * t, t)
        width = 2 * LANES
        for a in range(n_heads):
            for i in range(n_ck):
                rows = pl.ds(start + i * ck, ck)
                kext = jnp.concatenate([k_ref[rows, pair_lanes(a)], kc_ref[rows, :]], axis=1)
                for q0 in range(0, t, width):
                    if diagonal and i * ck > q0 + width - 1:
                        e_buf[slot, a, i * ck:(i + 1) * ck, q0:q0 + width] = jnp.zeros((ck, width), BF16)
                        cm_buf[slot, a, i, :, q0:q0 + width] = jnp.full((SUBLANES, width), NEG_INF, F32)
                        continue
                    s = _dot(kext, qt[a, :, q0:q0 + width])
                    if diagonal and (i + 1) * ck - 1 > q0:
                        r = lax.broadcasted_iota(jnp.int32, (ck, width), 0) + i * ck
                        c = lax.broadcasted_iota(jnp.int32, (ck, width), 1) + q0
                        s = jnp.where(r <= c, s, NEG_INF)
                    cm = _sublane_allreduce(_fold_rows(s, jnp.maximum), jnp.maximum)
                    e_buf[slot, a, i * ck:(i + 1) * ck, q0:q0 + width] = jnp.exp2(
                        s - jnp.tile(cm, (ck // SUBLANES, 1))).astype(BF16)
                    cm_buf[slot, a, i, :, q0:q0 + width] = cm

    def stage_c(w, ms, slot):
        m_floor = jnp.where(j_tab[w] == 0, NEG_INF, big)
        new_ms = []
        for a in range(n_heads):
            m_old = jnp.minimum(ms[a], m_floor)
            cms = [cm_buf[slot, a, i] for i in range(n_ck)]
            m_new = functools.reduce(jnp.maximum, cms, m_old)
            n_rep = acc.shape[1] // SUBLANES
            total = acc[a] * jnp.tile(jnp.exp2(m_old - m_new), (n_rep, 1))
            for i in range(n_ck):
                vt = jnp.concatenate([vt_ref[0, j_tab[w], pair_lanes(a), i * ck:(i + 1) * ck], ones_rows], axis=0)
                pv = _dot(vt, e_buf[slot, a, i * ck:(i + 1) * ck, :])
                total = total + pv * jnp.tile(jnp.exp2(cms[i] - m_new), (n_rep, 1))
            acc[a] = total
            new_ms.append(m_new)
        return tuple(new_ms)

    def finalize(w):
        outs = []
        for a in range(n_heads):
            full = acc[a]
            denom = jnp.tile(full[PAIR_LANES:PAIR_LANES + SUBLANES], (PAIR_LANES // SUBLANES, 1))
            outs.append(full[:PAIR_LANES] / denom)
        row = lax.broadcasted_iota(jnp.int32, (PAIR_LANES, t), 0)
        rows = pl.ds(pl.multiple_of(qi_tab[w] * t, t), t)
        for a in range(0, n_heads, 2):
            o = jnp.transpose(jnp.where(row < FOX_HEAD_DIM, outs[a], outs[a + 1]))
            o_ref[rows, pair_lanes(a)] = o.astype(o_ref.dtype)

    load_q(0)
    stage_a(0, 0, True)

    def item(w, ms, slot):
        nxt = jnp.minimum(w + 1, n_items - 1)

        @pl.when(j_tab[nxt] == 0)
        def _():
            load_q(qi_tab[nxt])

        def stages(diagonal, ms):
            stage_a(nxt, 1 - slot, diagonal)
            ms = stage_c(w, ms, slot)
            return ms

        ms = lax.cond(j_tab[nxt] == qi_tab[nxt], functools.partial(stages, True),
                      functools.partial(stages, False), ms)

        @pl.when(j_tab[w] == qi_tab[w])
        def _():
            finalize(w)

        return ms

    def body(i, ms):
        return item(2 * i + 1, item(2 * i, ms, 0), 1)

    ms = lax.fori_loop(0, n_items // 2, body, tuple(jnp.full((SUBLANES, t), NEG_INF, F32) for _ in range(n_heads)))
    if n_items % 2:
        item(n_items - 1, ms, 0)


def _attn_prompt(qt, kb, kc, vt, c3t, n_seq, seq_len, t):
    total = kb.shape[0]
    nt = seq_len // t
    items = [(qi, j) for qi in range(nt) for j in range(qi + 1)]
    qi_tab = jnp.asarray([qi for qi, _ in items], jnp.int32)
    j_tab = jnp.asarray([j for _, j in items], jnp.int32)
    nh = 2 * ATTN_PAIRS_PER_STEP
    width = ATTN_PAIRS_PER_STEP * PAIR_LANES
    once = pl.Buffered(1)
    seq_in = pl.BlockSpec((seq_len, width), lambda b, p, *_: (b, p), pipeline_mode=once)
    seq_bias = pl.BlockSpec((seq_len, LANES), lambda b, p, *_: (b, 0), pipeline_mode=once)
    vt_in = pl.BlockSpec((1, nt, width, t), lambda b, p, *_: (b, 0, p, 0), pipeline_mode=once)
    c3t_in = pl.BlockSpec((1, nt, LANES, t), lambda b, p, *_: (b, 0, 0, 0), pipeline_mode=once)
    return pl.pallas_call(
        functools.partial(_attn_prompt_kernel, t, len(items)),
        grid_spec=pltpu.PrefetchScalarGridSpec(
            num_scalar_prefetch=2,
            grid=(n_seq, HEAD_PAIRS // ATTN_PAIRS_PER_STEP),
            in_specs=[vt_in, c3t_in, seq_in, seq_bias, vt_in],
            out_specs=pl.BlockSpec((seq_len, width), lambda b, p, *_: (b, p), pipeline_mode=once),
            scratch_shapes=[pltpu.VMEM((nh, 2 * LANES, t), BF16), pltpu.VMEM((2, nh, t, t), BF16),
                            pltpu.VMEM((2, nh, t // KEY_CHUNK, SUBLANES, t), F32),
                            pltpu.VMEM((nh, PAIR_LANES + ONES_ROWS, t), F32)]),
        out_shape=jax.ShapeDtypeStruct((total, D_MODEL), BF16),
        compiler_params=_cparams(2), name="fox_attn_prompt",
    )(qi_tab, j_tab, qt, c3t, kb, kc, vt)


def _attn_sample_kernel(n_cache_tiles, tk, t_new, q_ref, c3q_ref, kc_ref, vc_ref, lfc_ref,
                        kn_ref, vn_ref, lfn_ref, o_ref, qt, m_scr, acc_scr, carry, s_buf, p_buf, al_buf):
    j = pl.program_id(1)
    rows = 2 * t_new

    @pl.when(j == 0)
    def _():
        c3q = c3q_ref[...].astype(F32)
        for p in range(HEAD_PAIRS):
            q = q_ref[:, p * PAIR_LANES:(p + 1) * PAIR_LANES]
            qe = jnp.concatenate([_query_ext(q, c3q, 2 * p, True), _query_ext(q, c3q, 2 * p + 1, False)], axis=0)
            qt[p] = jnp.transpose(qe.astype(F32)).astype(BF16)
        m_scr[...] = jnp.full(m_scr.shape, NEG_INF, F32)
        acc_scr[...] = jnp.zeros(acc_scr.shape, F32)
        carry[...] = jnp.zeros_like(carry)

    def attend(k_pair, v_pair, n_keys, c3k, masked):
        kc = _key_bias_lanes(c3k)
        for p in range(HEAD_PAIRS):
            s = _dot(jnp.concatenate([k_pair(p), kc], axis=1), qt[p])
            if masked:
                r = lax.broadcasted_iota(jnp.int32, (n_keys, rows), 0)
                c = lax.broadcasted_iota(jnp.int32, (n_keys, rows), 1) & (t_new - 1)
                s = jnp.where(r <= c, s, NEG_INF)
            s_buf[p, :n_keys] = s
        for p in range(HEAD_PAIRS):
            s = s_buf[p, :n_keys]
            m_old = m_scr[p]
            m_new = jnp.maximum(m_old, _sublane_allreduce(_fold_rows(s, jnp.maximum), jnp.maximum))
            p_buf[p, :n_keys] = jnp.exp2(s - jnp.tile(m_new, (n_keys // SUBLANES, 1))).astype(BF16)
            alpha = jnp.exp2(m_old - m_new)
            al_buf[p] = jnp.transpose(jnp.tile(alpha, (rows // SUBLANES, 1)))
            m_scr[p] = m_new
        ones = jnp.ones((n_keys, LANES), BF16)
        for p in range(HEAD_PAIRS):
            pv = lax.dot_general(p_buf[p, :n_keys], jnp.concatenate([v_pair(p), ones], axis=1),
                                 (((0,), (0,)), ((), ())), preferred_element_type=F32)
            acc_scr[p] = acc_scr[p] * jnp.tile(al_buf[p], (1, 2)) + pv

    def cache_pair(ref):
        return lambda p: ref[0, :, p * PAIR_LANES:(p + 1) * PAIR_LANES].astype(BF16)

    def new_pair(ref):
        return lambda p: ref[:, p * PAIR_LANES:(p + 1) * PAIR_LANES]

    @pl.when(j < n_cache_tiles)
    def _():
        c = _tri_cumsum(lfc_ref[0], tk) + carry[0:1, :]
        carry[...] = jnp.broadcast_to(c[tk - 1:tk, :], carry.shape)
        attend(cache_pair(kc_ref), cache_pair(vc_ref), tk, _c_pieces(c), False)

    @pl.when(j == n_cache_tiles)
    def _():
        c = _tri_cumsum(lfn_ref[...], t_new) + carry[0:1, :]
        attend(new_pair(kn_ref), new_pair(vn_ref), t_new, _c_pieces(c), True)
        lane = lax.broadcasted_iota(jnp.int32, (t_new, LANES), 1)
        for p in range(HEAD_PAIRS):
            acc = acc_scr[p]
            oa = acc[:t_new, :LANES] / acc[:t_new, LANES:]
            ob = acc[t_new:, :LANES] / acc[t_new:, LANES:]
            o_ref[:, p * PAIR_LANES:(p + 1) * PAIR_LANES] = jnp.where(lane < FOX_HEAD_DIM, oa, ob).astype(o_ref.dtype)


def _attn_sample(qb, kb, vb, c3, logf, cache_k, cache_v, cache_logf, tk):
    n_seq, past, _ = cache_k.shape
    t_new = qb.shape[0] // n_seq
    nct = past // tk
    pad = lambda a: jnp.pad(a, ((0, 0),) * (a.ndim - 1) + ((0, LANES - FOX_HEADS),))
    new_tile = pl.BlockSpec((t_new, D_MODEL), lambda b, j: (b, 0))
    new_c3 = pl.BlockSpec((t_new, LANES), lambda b, j: (b, 0))
    cache_tile = pl.BlockSpec((1, tk, D_MODEL), lambda b, j: (b, jnp.minimum(j, nct - 1), 0))
    lf_tile = pl.BlockSpec((1, tk, LANES), lambda b, j: (b, jnp.minimum(j, nct - 1), 0))
    return pl.pallas_call(
        functools.partial(_attn_sample_kernel, nct, tk, t_new),
        grid=(n_seq, nct + 1),
        in_specs=[new_tile, new_c3, cache_tile, cache_tile, lf_tile, new_tile, new_tile, new_c3],
        out_specs=new_tile,
        out_shape=jax.ShapeDtypeStruct((n_seq * t_new, D_MODEL), BF16),
        scratch_shapes=[pltpu.VMEM((HEAD_PAIRS, 2 * LANES, 2 * t_new), BF16),
                        pltpu.VMEM((HEAD_PAIRS, SUBLANES, 2 * t_new), F32),
                        pltpu.VMEM((HEAD_PAIRS, 2 * t_new, 2 * LANES), F32),
                        pltpu.VMEM((8, LANES), F32),
                        pltpu.VMEM((HEAD_PAIRS, tk, 2 * t_new), F32),
                        pltpu.VMEM((HEAD_PAIRS, tk, 2 * t_new), BF16),
                        pltpu.VMEM((HEAD_PAIRS, 2 * t_new, 2 * t_new), F32)],
        compiler_params=_cparams(2), name="fox_attn_sample",
    )(qb, c3, cache_k, cache_v, pad(cache_logf), kb, vb, pad(logf))


def _trunk(x, n_seq, seq_len, ssm_h0, fox_cache, conv_hist, w, tm, tq):
    t = n_seq * seq_len
    nb = seq_len // SSM_BLOCK
    nc = SSM_GROUPS // COLUMN_GROUPS
    half = COLUMN_GROUPS * SSM_STATE
    x = x.reshape(t, D_MODEL)

    u0 = _norm(x, w["norm_mix"][0], tm)
    if ssm_h0 is None:
        h0 = None
    else:
        per_c = lambda h: jnp.swapaxes(h.reshape(n_seq, nc, half), 0, 1)
        h0c = jnp.concatenate([per_c(ssm_h0[0]), per_c(ssm_h0[1])], axis=2)
        h0 = jnp.zeros((nc, n_seq, nb, 2 * half), F32).at[:, :, 0, :].set(h0c).reshape(nc, n_seq * nb, 2 * half)
    ys, fin = _s5_core(u0, w["s5"], n_seq, seq_len, h0, fold_sequences=ssm_h0 is not None)
    per_g = lambda f: jnp.swapaxes(f.reshape(nc, n_seq, COLUMN_GROUPS, SSM_STATE), 0, 1).reshape(
        n_seq, SSM_GROUPS, SSM_STATE)
    fre, fim = per_g(fin[:, :, :half]), per_g(fin[:, :, half:])
    x, conv0 = _ffn(x, "glu", (ys, w["norm_mix"][0], w["ssm_d"], w["w_glu"]), conv_hist[0],
                    w["norm_ffn"][0], w["w_up"][0], w["w_gate"][0], w["conv_w"][0], w["conv_b"][0],
                    w["w_down"][0], None, seq_len, tm)

    k, v, logf, qb, kb, vb, c3, kc = _qkv(x, w["norm_mix"][1], w["wq"], w["wk"], w["wv"], w["wf"], w["bf"],
                                          seq_len, tq, transposed_v=fox_cache is None)
    if fox_cache is None:
        o = _attn_prompt(qb, kb, kc, vb, c3, n_seq, seq_len, tq)
    else:
        o = _attn_sample(qb, kb, vb, c3, logf, *fox_cache, tk=min(CACHE_TILE, fox_cache[0].shape[1]))
    y, conv1 = _ffn(x, "oproj", (o, w["w_o"]), conv_hist[1],
                    w["norm_ffn"][1], w["w_up"][1], w["w_gate"][1], w["conv_w"][1], w["conv_b"][1],
                    w["w_down"][1], w["norm_final"], seq_len, tm)

    shape4 = (1, n_seq, seq_len, FOX_HEADS, FOX_HEAD_DIM)
    return (y.reshape(n_seq, seq_len, D_MODEL),
            fre[None], fim[None],
            k.reshape(shape4), v.reshape(shape4), logf.reshape(1, n_seq, seq_len, FOX_HEADS),
            jnp.stack([conv0, conv1]))


def kernel(x_prompt, x_sample, state_ssm_re, state_ssm_im, cache_fox_k, cache_fox_v, cache_fox_logf, state_ffn_conv, norm_mix, norm_ffn, norm_final, ssm_a_re, ssm_a_im, ssm_log_step, ssm_b_re, ssm_b_im, ssm_c_re, ssm_c_im, ssm_d, ssm_w_glu, fox_w_qkvf, fox_b_f, fox_w_o, ffn_w_up, ffn_w_gate, ffn_conv_w, ffn_conv_b, ffn_w_down):
    n_p, l_p, _ = x_prompt.shape
    n_s, l_s, _ = x_sample.shape
    past = cache_fox_k.shape[2]
    wqkvf = fox_w_qkvf[0]
    w = {
        "norm_mix": norm_mix, "norm_ffn": norm_ffn, "norm_final": norm_final,
        "s5": _s5_block_diagonal(_s5_params(ssm_a_re[0], ssm_a_im[0], ssm_log_step[0], ssm_b_re[0],
                                            ssm_b_im[0], ssm_c_re[0], ssm_c_im[0])),
        "ssm_d": ssm_d[0], "w_glu": ssm_w_glu[0].astype(BF16),
        "wq": wqkvf[:, :D_MODEL].astype(BF16), "wk": wqkvf[:, D_MODEL:2 * D_MODEL].astype(BF16),
        "wv": wqkvf[:, 2 * D_MODEL:3 * D_MODEL].astype(BF16),
        "wf": jnp.pad(wqkvf[:, 3 * D_MODEL:], ((0, 0), (0, LANES - FOX_HEADS))).astype(BF16),
        "bf": jnp.pad(fox_b_f[0], (0, LANES - FOX_HEADS))[None, :],
        "w_o": fox_w_o[0].astype(BF16),
        "w_up": ffn_w_up.astype(BF16), "w_gate": ffn_w_gate.astype(BF16), "w_down": ffn_w_down.astype(BF16),
        "conv_w": ffn_conv_w, "conv_b": ffn_conv_b,
    }
    zero_hist = jnp.zeros((2, n_p, CONV_WIDTH - 1, D_FF), F32)
    outs_p = _trunk(x_prompt, n_p, l_p, None, None, zero_hist, w, tm=PROMPT_TILE, tq=PROMPT_TILE)
    cache = (cache_fox_k[0].reshape(n_s, past, D_MODEL), cache_fox_v[0].reshape(n_s, past, D_MODEL),
             cache_fox_logf[0])
    outs_s = _trunk(x_sample, n_s, l_s, (state_ssm_re[0], state_ssm_im[0]), cache, state_ffn_conv, w,
                    tm=SAMPLE_SEQS_PER_TILE * l_s, tq=l_s)
    return (outs_p[0], outs_s[0]) + outs_p[1:] + outs_s[1:]
```

```python
import functools
import math

import jax
import jax.numpy as jnp
from jax import lax
from jax.experimental import pallas as pl
from jax.experimental.pallas import tpu as pltpu

D_MODEL = 1024
SSM_GROUP = 16
SSM_GROUPS = D_MODEL // SSM_GROUP
SSM_STATE = 64
SSM_BLOCK = 16
SSM_BLOCK_LOG2 = 4
BLOCK_LANES = SSM_BLOCK * SSM_GROUP
COLUMN_LANES = 128
COLUMN_GROUPS = COLUMN_LANES // SSM_GROUP
STEP_PAIRS = SSM_BLOCK // 2
FOX_HEADS = 16
FOX_HEAD_DIM = 64
HEAD_PAIRS = FOX_HEADS // 2
PAIR_LANES = 2 * FOX_HEAD_DIM
D_FF = 2816
CONV_WIDTH = 3
PROMPT_TILE = 512
SAMPLE_SEQS_PER_TILE = 4
CACHE_TILE = 512
FFN_CHUNK_EDGES = (0, 1536, D_FF)
NORM_EPS = 1e-6
NEG_INF = -1e30
LOG2E = 1.4426950408889634
LANES = 128
SUBLANES = 8
_CQ_LANE = 3 * FOX_HEADS
VMEM_LIMIT = 56 * 1024 * 1024

BF16 = jnp.bfloat16
F32 = jnp.float32


def _cparams(n_axes, vmem=VMEM_LIMIT):
    return pltpu.CompilerParams(dimension_semantics=("arbitrary",) * n_axes, vmem_limit_bytes=vmem)


def _full(shape):
    nd = len(shape)
    return pl.BlockSpec(shape, lambda *_: (0,) * nd)


def _rms(x, g):
    ms = jnp.mean(x * x, axis=-1, keepdims=True)
    return x * lax.rsqrt(ms + NORM_EPS) * g


def _gelu(x):
    c = math.sqrt(2.0 / math.pi)
    return 0.5 * x * (1.0 + jnp.tanh(c * (x + 0.044715 * (x * x * x))))


def _sigmoid(x):
    return 1.0 / (1.0 + jnp.exp(-x))


def _log_sigmoid(x):
    return jnp.minimum(x, 0.0) - jnp.log1p(jnp.exp(-jnp.abs(x)))


def _split3(x):
    hi = x.astype(BF16)
    r1 = x - hi.astype(F32)
    mid = r1.astype(BF16)
    lo = (r1 - mid.astype(F32)).astype(BF16)
    return hi, mid, lo


def _dot(a, b):
    return jnp.dot(a, b, preferred_element_type=F32)


def _dot_nt(a, b):
    return lax.dot_general(a, b, (((1,), (1,)), ((), ())), preferred_element_type=F32)


def _dot_nt_f32(a, b):
    a0, a1, a2 = _split3(a)
    b0, b1, b2 = _split3(b)
    return (_dot_nt(a0, b0) + (_dot_nt(a0, b1) + _dot_nt(a1, b0))
            + (_dot_nt(a0, b2) + _dot_nt(a1, b1) + _dot_nt(a2, b0)))


def _tri_cumsum(x, n):
    r = lax.broadcasted_iota(jnp.int32, (n, n), 0)
    c = lax.broadcasted_iota(jnp.int32, (n, n), 1)
    tri = jnp.where(c <= r, 1.0, 0.0).astype(BF16)
    w = x.shape[1]
    r = _dot(tri, jnp.concatenate(_split3(x), axis=1))
    return r[:, :w] + r[:, w:2 * w] + r[:, 2 * w:]


def _s5_param_kernel(are_r, aim_r, ls_ref, btre, btim, ctre, ctim,
                     k_ref, wre_ref, wim_ref, zre_ref, zim_ref, mu_ref):
    dt = jnp.exp(ls_ref[0])
    lre, lim = are_r[0], aim_r[0]
    mag = jnp.exp(lre * dt)
    abr = mag * jnp.cos(lim * dt)
    abi = mag * jnp.sin(lim * dt)
    den = lre * lre + lim * lim
    nr = abr - 1.0
    zre = (nr * lre + abi * lim) / den
    zim = (abi * lre - nr * lim) / den

    def cpow(e):
        m = jnp.exp(lre * dt * e)
        a = lim * dt * e
        return m * jnp.cos(a), m * jnp.sin(a)

    def per_step(e):
        step = lax.broadcasted_iota(jnp.int32, (SSM_BLOCK, SSM_STATE), 0).astype(F32)
        pr, pi = cpow(e(step))
        rep = lambda t: jnp.broadcast_to(t[:, None, :], (SSM_BLOCK, SSM_GROUP, SSM_STATE)).reshape(
            BLOCK_LANES, SSM_STATE)
        return rep(pr), rep(pi)

    bre, bim = btre[0], btim[0]
    bbre = zre * bre - zim * bim
    bbim = zre * bim + zim * bre
    pwr, pwi = per_step(lambda s: (SSM_BLOCK - 1.0) - s)
    wre_ref[0] = (pwr * bbre - pwi * bbim).astype(BF16)
    wim_ref[0] = (pwr * bbim + pwi * bbre).astype(BF16)
    pbr, pbi = per_step(lambda s: -s)
    bxre = pbr * bbre - pbi * bbim
    bxim = pbr * bbim + pbi * bbre

    cre, cim = ctre[0], ctim[0]
    pcr, pci = per_step(lambda t: t)
    cxre = pcr * cre - pci * cim
    cxim = pcr * cim + pci * cre
    pzr, pzi = per_step(lambda t: t + 1.0)
    zre_ref[0] = (pzr * cre - pzi * cim).astype(BF16)
    zim_ref[0] = (-(pzr * cim + pzi * cre)).astype(BF16)

    kfull = _dot_nt_f32(bxre, cxre) - _dot_nt_f32(bxim, cxim)
    rs = lax.broadcasted_iota(jnp.int32, (BLOCK_LANES, BLOCK_LANES), 0) >> SSM_BLOCK_LOG2
    ct = lax.broadcasted_iota(jnp.int32, (BLOCK_LANES, BLOCK_LANES), 1) >> SSM_BLOCK_LOG2
    k_ref[0] = jnp.where(ct >= rs, kfull, 0.0).astype(BF16)

    mur, mui = cpow(float(SSM_BLOCK))
    mu_ref[0, 0:1, :] = mur
    mu_ref[0, 1:2, :] = mui


def _s5_params(a_re, a_im, log_step, b_re, b_im, c_re, c_im):
    g, p = a_re.shape
    bt = jnp.tile(jnp.swapaxes(b_re, 1, 2), (1, SSM_BLOCK, 1)), jnp.tile(jnp.swapaxes(b_im, 1, 2), (1, SSM_BLOCK, 1))
    ct = jnp.tile(c_re, (1, SSM_BLOCK, 1)), jnp.tile(c_im, (1, SSM_BLOCK, 1))
    row = pl.BlockSpec((1, 1, p), lambda i: (i, 0, 0))
    one = pl.BlockSpec((1, 1, 1), lambda i: (i, 0, 0))
    tall = pl.BlockSpec((1, BLOCK_LANES, p), lambda i: (i, 0, 0))
    return pl.pallas_call(
        _s5_param_kernel,
        grid=(g,),
        in_specs=[row, row, one, tall, tall, tall, tall],
        out_specs=[pl.BlockSpec((1, BLOCK_LANES, BLOCK_LANES), lambda i: (i, 0, 0)),
                   tall, tall, tall, tall, pl.BlockSpec((1, 2, p), lambda i: (i, 0, 0))],
        out_shape=[jax.ShapeDtypeStruct((g, BLOCK_LANES, BLOCK_LANES), BF16)]
        + [jax.ShapeDtypeStruct((g, BLOCK_LANES, p), BF16)] * 4
        + [jax.ShapeDtypeStruct((g, 2, p), F32)],
        compiler_params=_cparams(1),
        name="s5_params",
    )(a_re[:, None, :], a_im[:, None, :], log_step[:, None, None], bt[0], bt[1], ct[0], ct[1])


def _norm_kernel(x_ref, g_ref, u_ref):
    u_ref[...] = _rms(x_ref[...], g_ref[...]).astype(u_ref.dtype)


def _norm(x, g, tm):
    t = x.shape[0]
    tile = pl.BlockSpec((tm, D_MODEL), lambda i: (i, 0))
    return pl.pallas_call(
        _norm_kernel, grid=(t // tm,),
        in_specs=[tile, _full((1, D_MODEL))], out_specs=tile,
        out_shape=jax.ShapeDtypeStruct((t, D_MODEL), F32),
        compiler_params=_cparams(1), name="s5_prenorm",
    )(x, g[None, :])


def _s5_core_kernel(nb, n_seq, has_h0, *refs):
    if has_h0:
        u_ref, kk_ref, ww_ref, zz_ref, mu_ref, h0_ref, y_ref, f_ref, s_scr = refs
    else:
        u_ref, kk_ref, ww_ref, zz_ref, mu_ref, y_ref, f_ref, s_scr = refs
    rows = nb * n_seq
    half = COLUMN_GROUPS * SSM_STATE
    xp = [jnp.concatenate([u_ref[pl.ds(2 * p, rows, stride=SSM_BLOCK), :],
                           u_ref[pl.ds(2 * p + 1, rows, stride=SSM_BLOCK), :]], axis=1).astype(BF16)
          for p in range(STEP_PAIRS)]
    s = _dot(xp[0], ww_ref[0, 0])
    for p in range(1, STEP_PAIRS):
        s = s + _dot(xp[p], ww_ref[0, p])
    sre, sim = s[:, :half], s[:, half:]
    mu = mu_ref[0]
    mre, mim = mu[0:1, :], mu[1:2, :]
    blk = lax.broadcasted_iota(jnp.int32, (rows, half), 0) & (nb - 1)
    if has_h0:
        h0 = h0_ref[0]
        h0re, h0im = h0[:, :half], h0[:, half:]
        sre = sre + (mre * h0re - mim * h0im)
        sim = sim + (mre * h0im + mim * h0re)
    inner = min(nb, SUBLANES)
    pos = lax.broadcasted_iota(jnp.int32, (SUBLANES, half), 0) & (inner - 1)
    powers = []
    for _ in range(4):
        powers.append((mre, mim))
        mre, mim = mre * mre - mim * mim, 2.0 * mre * mim
    d = 1
    while d < inner:
        pr, pi = powers[d.bit_length() - 1]
        pr = jnp.tile(jnp.where(pos >= d, pr, 0.0), (rows // SUBLANES, 1))
        pi = jnp.tile(jnp.where(pos >= d, pi, 0.0), (rows // SUBLANES, 1))
        shr, shi = pltpu.roll(sre, d, 0), pltpu.roll(sim, d, 0)
        sre, sim = sre + (pr * shr - pi * shi), sim + (pr * shi + pi * shr)
        d *= 2
    if nb > SUBLANES:
        sub = lax.broadcasted_iota(jnp.int32, (SUBLANES, half), 0) + 1
        wre, wim = jnp.ones((SUBLANES, half), F32), jnp.zeros((SUBLANES, half), F32)
        for k, (pr, pi) in enumerate(powers):
            bit = ((sub >> k) & 1) == 1
            fr, fi = jnp.where(bit, pr, 1.0), jnp.where(bit, pi, 0.0)
            wre, wim = wre * fr - wim * fi, wre * fi + wim * fr
        groups = nb // SUBLANES
        out_re, out_im = [], []
        for j in range(rows // SUBLANES):
            lre, lim_ = sre[j * SUBLANES:(j + 1) * SUBLANES], sim[j * SUBLANES:(j + 1) * SUBLANES]
            if j % groups:
                lre, lim_ = lre + (wre * cre - wim * cim), lim_ + (wre * cim + wim * cre)
            cre = jnp.broadcast_to(lre[SUBLANES - 1:SUBLANES, :], (SUBLANES, half))
            cim = jnp.broadcast_to(lim_[SUBLANES - 1:SUBLANES, :], (SUBLANES, half))
            out_re.append(lre)
            out_im.append(lim_)
        sre, sim = jnp.concatenate(out_re, axis=0), jnp.concatenate(out_im, axis=0)
    n_chunks = half // LANES
    for i in range(n_chunks):
        s_scr[i] = sre[:, i * LANES:(i + 1) * LANES]
        s_scr[n_chunks + i] = sim[:, i * LANES:(i + 1) * LANES]
    f_ref[0, 0] = jnp.concatenate([s_scr[i, pl.ds(nb - 1, n_seq, stride=nb), :] for i in range(2 * n_chunks)],
                                  axis=1)
    first = blk == 0
    inre = jnp.where(first, 0.0, pltpu.roll(sre, 1, 0))
    inim = jnp.where(first, 0.0, pltpu.roll(sim, 1, 0))
    if has_h0:
        inre = inre + h0re
        inim = inim + h0im
    s_in = jnp.concatenate([inre, inim], axis=1).astype(BF16)
    for p in range(STEP_PAIRS):
        y = _dot_nt(s_in, zz_ref[0, p])
        for q in range(p + 1):
            y = y + _dot(xp[q], kk_ref[0, p - q])
        y_ref[pl.ds(2 * p, rows, stride=SSM_BLOCK), :] = y[:, :COLUMN_LANES]
        y_ref[pl.ds(2 * p + 1, rows, stride=SSM_BLOCK), :] = y[:, COLUMN_LANES:]


def _s5_block_diagonal(params):
    kmat, wre, wim, zre, zim, mu = params
    cg, nc = COLUMN_GROUPS, SSM_GROUPS // COLUMN_GROUPS
    rows = 2 * COLUMN_LANES
    grp = lambda n, width: (jnp.arange(n) // width) % cg

    def stack_rows(m, pairs_on_rows):
        minor = m.shape[-1]
        if pairs_on_rows:
            m6 = m.reshape(nc, cg, STEP_PAIRS, 2, SSM_GROUP, minor)
            return jnp.transpose(m6, (0, 2, 3, 1, 4, 5)).reshape(nc, STEP_PAIRS, rows, minor)
        m6 = m.reshape(nc, cg, 2, SSM_GROUP, STEP_PAIRS, minor // STEP_PAIRS)
        return jnp.transpose(m6, (0, 4, 2, 1, 3, 5)).reshape(nc, STEP_PAIRS, rows, minor // STEP_PAIRS)

    def spread(stacked, width):
        n_in = stacked.shape[-1]
        n_out = n_in * cg
        src = (jnp.arange(n_out) // (width * cg)) * width + jnp.arange(n_out) % width
        place = (jnp.arange(n_in)[:, None] == src[None, :]).astype(BF16)
        own = (grp(rows, SSM_GROUP)[:, None] == grp(n_out, width)[None, :]).astype(BF16)
        return jnp.einsum('cprk,kn->cprn', stacked, place, preferred_element_type=BF16) * own

    kk = spread(stack_rows(kmat[:, :2 * SSM_GROUP, :], False), SSM_GROUP)
    ww = spread(stack_rows(jnp.concatenate([wre, wim], axis=2), True), SSM_STATE)
    zz_t = spread(stack_rows(jnp.concatenate([zre, zim], axis=2), True), SSM_STATE)
    mu_c = jnp.transpose(mu.reshape(nc, cg, 2, SSM_STATE), (0, 2, 1, 3)).reshape(nc, 2, cg * SSM_STATE)
    return kk, ww, zz_t, mu_c


def _s5_core(u, bd, n_seq, seq_len, h0, fold_sequences):
    kk, ww, zz, mu_c = bd
    nc = kk.shape[0]
    nb = seq_len // SSM_BLOCK
    seqs = n_seq if fold_sequences else 1
    n_steps = n_seq // seqs
    rows_tok = seqs * seq_len
    rows = seqs * nb
    state = 2 * COLUMN_GROUPS * SSM_STATE
    col = pl.BlockSpec((rows_tok, COLUMN_LANES), lambda c, i: (i, c))
    per_c = lambda a: pl.BlockSpec((1,) + a.shape[1:], lambda c, i: (c,) + (0,) * (a.ndim - 1))
    in_specs = [col, per_c(kk), per_c(ww), per_c(zz), per_c(mu_c)]
    args = [u, kk, ww, zz, mu_c]
    if h0 is not None:
        in_specs.append(pl.BlockSpec((1, rows, state), lambda c, i: (c, i, 0)))
        args.append(h0)
    y, fin = pl.pallas_call(
        functools.partial(_s5_core_kernel, nb, seqs, h0 is not None),
        grid=(nc, n_steps),
        in_specs=in_specs,
        out_specs=[col, pl.BlockSpec((1, 1, seqs, state), lambda c, i: (c, i, 0, 0))],
        out_shape=[jax.ShapeDtypeStruct(u.shape, F32), jax.ShapeDtypeStruct((nc, n_steps, seqs, state), F32)],
        scratch_shapes=[pltpu.VMEM((state // LANES, rows, LANES), F32)],
        compiler_params=_cparams(2), name="s5_core",
    )(*args)
    return y, fin.reshape(nc, n_seq, state)


def _ffn_kernel(pre, final_norm, tiles_per_seq, seqs_per_tile, tm, *refs):
    refs = list(refs)
    x_ref = refs.pop(0)
    if pre == "glu":
        ys_ref, gmix_ref, dskip_ref, wglu_ref = refs[:4]
        refs = refs[4:]
    else:
        o_ref, wo_ref = refs[:2]
        refs = refs[2:]
    (hist_ref, gffn_ref, wup_ref, wgate_ref, convw_ref, convb_ref, wdown_ref) = refs[:7]
    refs = refs[7:]
    if final_norm:
        gfin_ref = refs.pop(0)
    out_ref, cstate_ref, carry = refs
    chained = seqs_per_tile == 0

    if chained:
        @pl.when(pl.program_id(0) % tiles_per_seq == 0)
        def _():
            carry[...] = hist_ref[0]

    x = x_ref[...]
    if pre == "glu":
        u = _rms(x, gmix_ref[...])
        g = _gelu(ys_ref[...] + dskip_ref[...] * u).astype(BF16)
        z = _dot(g, wglu_ref[...])
        x = x + z[:, :D_MODEL] * _sigmoid(z[:, D_MODEL:])
    else:
        x = x + _dot(o_ref[...], wo_ref[...])
    h = _rms(x, gffn_ref[...]).astype(BF16)

    seqs = max(1, seqs_per_tile)
    span = tm // seqs
    for lo, hi in zip(FFN_CHUNK_EDGES[:-1], FFN_CHUNK_EDGES[1:]):
        cs = slice(lo, hi)
        row = lax.broadcasted_iota(jnp.int32, (tm, hi - lo), 0)
        a = _dot(h, wup_ref[:, cs])
        a1, a2 = pltpu.roll(a, 1, 0), pltpu.roll(a, 2, 0)
        for s in range(seqs):
            prev = carry[:, cs] if chained else hist_ref[s, :, cs]
            a1 = jnp.where(row == s * span, prev[1:2, :], a1)
            a2 = jnp.where(row == s * span, prev[0:1, :], jnp.where(row == s * span + 1, prev[1:2, :], a2))
            cstate_ref[s, :, cs] = a[(s + 1) * span - (CONV_WIDTH - 1):(s + 1) * span, :]
        conv = a2 * convw_ref[0:1, cs] + a1 * convw_ref[1:2, cs] + a * convw_ref[2:3, cs] + convb_ref[:, cs]
        if chained:
            carry[:, cs] = a[tm - (CONV_WIDTH - 1):, :]
        hh = (_gelu(conv) * _dot(h, wgate_ref[:, cs])).astype(BF16)
        x = x + _dot(hh, wdown_ref[cs, :])
    if final_norm:
        x = _rms(x, gfin_ref[...])
    out_ref[...] = x


def _ffn(x, pre, pre_args, hist, g_ffn, w_up, w_gate, conv_w, conv_b, w_down, g_final, seq_len, tm):
    t = x.shape[0]
    n_seq = t // seq_len
    tiles_per_seq = max(1, seq_len // tm)
    seqs_per_tile = 0 if tm < seq_len else tm // seq_len
    tile = pl.BlockSpec((tm, D_MODEL), lambda i: (i, 0))
    vec = _full((1, D_MODEL))
    const = lambda a: pl.BlockSpec(a.shape, lambda i: (0,) * a.ndim, pipeline_mode=pl.Buffered(1))
    per_seq = pl.BlockSpec((max(1, seqs_per_tile), CONV_WIDTH - 1, D_FF), lambda i: (i // tiles_per_seq, 0, 0))
    args, in_specs = [x], [tile]
    if pre == "glu":
        ys, g_mix, d_skip, w_glu = pre_args
        args += [ys, g_mix[None, :], d_skip[None, :], w_glu]
        in_specs += [tile, vec, vec, const(w_glu)]
    else:
        o, w_o = pre_args
        args += [o, w_o]
        in_specs += [tile, const(w_o)]
    args += [hist, g_ffn[None, :], w_up, w_gate, conv_w, conv_b[None, :], w_down]
    in_specs += [per_seq, vec, const(w_up), const(w_gate), _full((CONV_WIDTH, D_FF)), _full((1, D_FF)),
                 const(w_down)]
    if g_final is not None:
        args.append(g_final[None, :])
        in_specs.append(vec)
    return pl.pallas_call(
        functools.partial(_ffn_kernel, pre, g_final is not None, tiles_per_seq, seqs_per_tile, tm),
        grid=(t // tm,),
        in_specs=in_specs,
        out_specs=[tile, per_seq],
        out_shape=[jax.ShapeDtypeStruct((t, D_MODEL), F32),
                   jax.ShapeDtypeStruct((n_seq, CONV_WIDTH - 1, D_FF), F32)],
        scratch_shapes=[pltpu.VMEM((CONV_WIDTH - 1, D_FF), F32)],
        compiler_params=_cparams(1), name="ffn_" + pre,
    )(*args)


def _c_pieces(c):
    hi, mid, lo = _split3(c * LOG2E)
    return (hi.astype(F32) + pltpu.roll(mid.astype(F32), FOX_HEADS, 1)
            + pltpu.roll(lo.astype(F32), 2 * FOX_HEADS, 1)).astype(BF16)


def _key_bias_lanes(c3):
    lane = lax.broadcasted_iota(jnp.int32, c3.shape, 1)
    ones = ((lane >= _CQ_LANE) & (lane < _CQ_LANE + 3)).astype(F32)
    return (ones - c3.astype(F32)).astype(BF16)


def _qkv_kernel(tiles_per_seq, tm, transposed_v, x_ref, g_ref, wq_ref, wk_ref, wv_ref, wf_ref, bf_ref,
                k_ref, v_ref, lf_ref, qb_ref, kb_ref, vb_ref, c3_ref, kc_ref, carry):
    @pl.when(pl.program_id(0) % tiles_per_seq == 0)
    def _():
        carry[...] = jnp.zeros_like(carry)

    u = _rms(x_ref[...], g_ref[...]).astype(BF16)
    lane = lax.broadcasted_iota(jnp.int32, (tm, LANES), 1)
    logf = jnp.where(lane < FOX_HEADS, _log_sigmoid(_dot(u, wf_ref[...]) + bf_ref[...]), 0.0)
    lf_ref[...] = logf[:, :FOX_HEADS]
    c = _tri_cumsum(logf, tm) + carry[0:1, :]
    carry[...] = jnp.broadcast_to(c[tm - 1:tm, :], carry.shape)
    c3 = _c_pieces(c)
    if transposed_v:
        c3_ref[0, 0] = jnp.transpose(c3.astype(F32))
    else:
        c3_ref[...] = c3
    kc_ref[...] = _key_bias_lanes(c3)
    v = _dot(u, wv_ref[...])
    v_ref[...] = v
    q = _dot(u, wq_ref[...]) * (FOX_HEAD_DIM ** -0.5 * LOG2E)
    if transposed_v:
        vb_ref[0, 0] = jnp.transpose(v).astype(BF16)
        qb_ref[0, 0] = jnp.transpose(q).astype(BF16)
    else:
        vb_ref[...] = v.astype(BF16)
        qb_ref[...] = q.astype(BF16)
    k = _dot(u, wk_ref[...])
    k_ref[...] = k
    kb_ref[...] = k.astype(BF16)


def _qkv(x, g, wq, wk, wv, wf, bf, seq_len, tm, transposed_v):
    t = x.shape[0]
    tiles_per_seq = seq_len // tm
    tile = pl.BlockSpec((tm, D_MODEL), lambda i: (i, 0))
    narrow = pl.BlockSpec((tm, LANES), lambda i: (i, 0))
    const = lambda a: pl.BlockSpec(a.shape, lambda i: (0,) * a.ndim, pipeline_mode=pl.Buffered(1))
    wide_bf = jax.ShapeDtypeStruct((t, D_MODEL), BF16)
    if transposed_v:
        per_tile = lambda i: (i // tiles_per_seq, i % tiles_per_seq, 0, 0)
        qv_spec = pl.BlockSpec((1, 1, D_MODEL, tm), per_tile)
        qv_shape = jax.ShapeDtypeStruct((t // seq_len, tiles_per_seq, D_MODEL, tm), BF16)
        c3_spec = pl.BlockSpec((1, 1, LANES, tm), per_tile)
        c3_shape = jax.ShapeDtypeStruct((t // seq_len, tiles_per_seq, LANES, tm), F32)
    else:
        qv_spec, qv_shape = tile, wide_bf
        c3_spec, c3_shape = narrow, jax.ShapeDtypeStruct((t, LANES), BF16)
    return pl.pallas_call(
        functools.partial(_qkv_kernel, tiles_per_seq, tm, transposed_v),
        grid=(t // tm,),
        in_specs=[tile, _full((1, D_MODEL)), const(wq), const(wk), const(wv), const(wf), _full((1, LANES))],
        out_specs=[tile, tile, pl.BlockSpec((tm, FOX_HEADS), lambda i: (i, 0)), qv_spec, tile, qv_spec,
                   c3_spec, narrow],
        out_shape=[jax.ShapeDtypeStruct((t, D_MODEL), F32), jax.ShapeDtypeStruct((t, D_MODEL), F32),
                   jax.ShapeDtypeStruct((t, FOX_HEADS), F32),
                   qv_shape, wide_bf, qv_shape, c3_shape, jax.ShapeDtypeStruct((t, LANES), BF16)],
        scratch_shapes=[pltpu.VMEM((8, LANES), F32)],
        compiler_params=_cparams(1), name="fox_qkv",
    )(x, g[None, :], wq, wk, wv, wf, bf)


def _query_ext(q, c3q, head, lo_half):
    rows = q.shape[0]
    lane = lax.broadcasted_iota(jnp.int32, (rows, LANES), 1)
    own = (lane < FOX_HEAD_DIM) if lo_half else (lane >= FOX_HEAD_DIM)
    qm = jnp.where(own, q, jnp.zeros_like(q))
    pick = lambda j: jnp.sum(jnp.where(lane == head + j * FOX_HEADS, c3q, 0.0), axis=1, keepdims=True)
    onehot = (lane == head) | (lane == head + FOX_HEADS) | (lane == head + 2 * FOX_HEADS)
    qc = jnp.where(lane == _CQ_LANE, pick(0),
                   jnp.where(lane == _CQ_LANE + 1, pick(1),
                             jnp.where(lane == _CQ_LANE + 2, pick(2), jnp.where(onehot, 1.0, 0.0))))
    return jnp.concatenate([qm, qc.astype(BF16)], axis=1)


def _sublane_allreduce(x, op):
    for sh in (4, 2, 1):
        x = op(x, pltpu.roll(x, sh, 0))
    return x


def _fold_rows(x, op):
    parts = [x[i * SUBLANES:(i + 1) * SUBLANES] for i in range(x.shape[0] // SUBLANES)]
    while len(parts) > 1:
        parts = [op(parts[i], parts[i + 1]) for i in range(0, len(parts), 2)]
    return parts[0]


KEY_CHUNK = 256
ATTN_PAIRS_PER_STEP = 4
ONES_ROWS = 16


def _attn_prompt_kernel(t, n_items, qi_tab, j_tab, q_ref, c3_ref, k_ref, kc_ref, vt_ref, o_ref,
                        qt, e_buf, cm_buf, acc):
    n_heads = acc.shape[0]
    first_head = pl.program_id(1) * n_heads
    pair_lanes = lambda a: slice((a // 2) * PAIR_LANES, (a // 2 + 1) * PAIR_LANES)
    big = -NEG_INF
    ck = KEY_CHUNK
    n_ck = t // ck
    for a in range(n_heads):
        acc[a] = jnp.zeros(acc.shape[1:], F32)
    ones_rows = jnp.ones((ONES_ROWS, ck), BF16)

    def load_q(qi):
        row = lax.broadcasted_iota(jnp.int32, (PAIR_LANES, t), 0)
        for a in range(n_heads):
            head = first_head + a
            own = (row < FOX_HEAD_DIM) if a % 2 == 0 else (row >= FOX_HEAD_DIM)
            q_pair = q_ref[0, qi, pair_lanes(a), :]
            piece = lambda j: c3_ref[0, qi, pl.ds(head + j * FOX_HEADS, 1), :]
            onehot = (row == head) | (row == head + FOX_HEADS) | (row == head + 2 * FOX_HEADS)
            bias = jnp.where(row == _CQ_LANE, piece(0),
                             jnp.where(row == _CQ_LANE + 1, piece(1),
                                       jnp.where(row == _CQ_LANE + 2, piece(2), jnp.where(onehot, 1.0, 0.0))))
            qt[a] = jnp.concatenate([jnp.where(own, q_pair, jnp.zeros_like(q_pair)), bias.astype(BF16)], axis=0)

    def stage_a(w, slot, diagonal):
        start = pl.multiple_of(j_tab[w] * t, t)
        width = 2 * LANES
        for a in range(n_heads):
            for i in range(n_ck):
                rows = pl.ds(start + i * ck, ck)
                kext = jnp.concatenate([k_ref[rows, pair_lanes(a)], kc_ref[rows, :]], axis=1)
                for q0 in range(0, t, width):
                    if diagonal and i * ck > q0 + width - 1:
                        e_buf[slot, a, i * ck:(i + 1) * ck, q0:q0 + width] = jnp.zeros((ck, width), BF16)
                        cm_buf[slot, a, i, :, q0:q0 + width] = jnp.full((SUBLANES, width), NEG_INF, F32)
                        continue
                    s = _dot(kext, qt[a, :, q0:q0 + width])
                    if diagonal and (i + 1) * ck - 1 > q0:
                        r = lax.broadcasted_iota(jnp.int32, (ck, width), 0) + i * ck
                        c = lax.broadcasted_iota(jnp.int32, (ck, width), 1) + q0
                        s = jnp.where(r <= c, s, NEG_INF)
                    cm = _sublane_allreduce(_fold_rows(s, jnp.maximum), jnp.maximum)
                    e_buf[slot, a, i * ck:(i + 1) * ck, q0:q0 + width] = jnp.exp2(
                        s - jnp.tile(cm, (ck // SUBLANES, 1))).astype(BF16)
                    cm_buf[slot, a, i, :, q0:q0 + width] = cm

    def stage_c(w, ms, slot):
        m_floor = jnp.where(j_tab[w] == 0, NEG_INF, big)
        new_ms = []
        for a in range(n_heads):
            m_old = jnp.minimum(ms[a], m_floor)
            cms = [cm_buf[slot, a, i] for i in range(n_ck)]
            m_new = functools.reduce(jnp.maximum, cms, m_old)
            n_rep = acc.shape[1] // SUBLANES
            total = acc[a] * jnp.tile(jnp.exp2(m_old - m_new), (n_rep, 1))
            for i in range(n_ck):
                vt = jnp.concatenate([vt_ref[0, j_tab[w], pair_lanes(a), i * ck:(i + 1) * ck], ones_rows], axis=0)
                pv = _dot(vt, e_buf[slot, a, i * ck:(i + 1) * ck, :])
                total = total + pv * jnp.tile(jnp.exp2(cms[i] - m_new), (n_rep, 1))
            acc[a] = total
            new_ms.append(m_new)
        return tuple(new_ms)

    def finalize(w):
        outs = []
        for a in range(n_heads):
            full = acc[a]
            denom = jnp.tile(full[PAIR_LANES:PAIR_LANES + SUBLANES], (PAIR_LANES // SUBLANES, 1))
            outs.append(full[:PAIR_LANES] / denom)
        row = lax.broadcasted_iota(jnp.int32, (PAIR_LANES, t), 0)
        rows = pl.ds(pl.multiple_of(qi_tab[w] * t, t), t)
        for a in range(0, n_heads, 2):
            o = jnp.transpose(jnp.where(row < FOX_HEAD_DIM, outs[a], outs[a + 1]))
            o_ref[rows, pair_lanes(a)] = o.astype(o_ref.dtype)

    load_q(0)
    stage_a(0, 0, True)

    def item(w, ms, slot):
        nxt = jnp.minimum(w + 1, n_items - 1)

        @pl.when(j_tab[nxt] == 0)
        def _():
            load_q(qi_tab[nxt])

        def stages(diagonal, ms):
            stage_a(nxt, 1 - slot, diagonal)
            ms = stage_c(w, ms, slot)
            return ms

        ms = lax.cond(j_tab[nxt] == qi_tab[nxt], functools.partial(stages, True),
                      functools.partial(stages, False), ms)

        @pl.when(j_tab[w] == qi_tab[w])
        def _():
            finalize(w)

        return ms

    def body(i, ms):
        return item(2 * i + 1, item(2 * i, ms, 0), 1)

    ms = lax.fori_loop(0, n_items // 2, body, tuple(jnp.full((SUBLANES, t), NEG_INF, F32) for _ in range(n_heads)))
    if n_items % 2:
        item(n_items - 1, ms, 0)


def _attn_prompt(qt, kb, kc, vt, c3t, n_seq, seq_len, t):
    total = kb.shape[0]
    nt = seq_len // t
    items = [(qi, j) for qi in range(nt) for j in range(qi + 1)]
    qi_tab = jnp.asarray([qi for qi, _ in items], jnp.int32)
    j_tab = jnp.asarray([j for _, j in items], jnp.int32)
    nh = 2 * ATTN_PAIRS_PER_STEP
    width = ATTN_PAIRS_PER_STEP * PAIR_LANES
    once = pl.Buffered(1)
    seq_in = pl.BlockSpec((seq_len, width), lambda b, p, *_: (b, p), pipeline_mode=once)
    seq_bias = pl.BlockSpec((seq_len, LANES), lambda b, p, *_: (b, 0), pipeline_mode=once)
    vt_in = pl.BlockSpec((1, nt, width, t), lambda b, p, *_: (b, 0, p, 0), pipeline_mode=once)
    c3t_in = pl.BlockSpec((1, nt, LANES, t), lambda b, p, *_: (b, 0, 0, 0), pipeline_mode=once)
    return pl.pallas_call(
        functools.partial(_attn_prompt_kernel, t, len(items)),
        grid_spec=pltpu.PrefetchScalarGridSpec(
            num_scalar_prefetch=2,
            grid=(n_seq, HEAD_PAIRS // ATTN_PAIRS_PER_STEP),
            in_specs=[vt_in, c3t_in, seq_in, seq_bias, vt_in],
            out_specs=pl.BlockSpec((seq_len, width), lambda b, p, *_: (b, p), pipeline_mode=once),
            scratch_shapes=[pltpu.VMEM((nh, 2 * LANES, t), BF16), pltpu.VMEM((2, nh, t, t), BF16),
                            pltpu.VMEM((2, nh, t // KEY_CHUNK, SUBLANES, t), F32),
                            pltpu.VMEM((nh, PAIR_LANES + ONES_ROWS, t), F32)]),
        out_shape=jax.ShapeDtypeStruct((total, D_MODEL), BF16),
        compiler_params=_cparams(2), name="fox_attn_prompt",
    )(qi_tab, j_tab, qt, c3t, kb, kc, vt)


def _attn_sample_kernel(n_cache_tiles, tk, t_new, q_ref, c3q_ref, kc_ref, vc_ref, lfc_ref,
                        kn_ref, vn_ref, lfn_ref, o_ref, qt, m_scr, acc_scr, carry, s_buf, p_buf, al_buf):
    j = pl.program_id(1)
    rows = 2 * t_new

    @pl.when(j == 0)
    def _():
        c3q = c3q_ref[...].astype(F32)
        for p in range(HEAD_PAIRS):
            q = q_ref[:, p * PAIR_LANES:(p + 1) * PAIR_LANES]
            qe = jnp.concatenate([_query_ext(q, c3q, 2 * p, True), _query_ext(q, c3q, 2 * p + 1, False)], axis=0)
            qt[p] = jnp.transpose(qe.astype(F32)).astype(BF16)
        m_scr[...] = jnp.full(m_scr.shape, NEG_INF, F32)
        acc_scr[...] = jnp.zeros(acc_scr.shape, F32)
        carry[...] = jnp.zeros_like(carry)

    def attend(k_pair, v_pair, n_keys, c3k, masked):
        kc = _key_bias_lanes(c3k)
        for p in range(HEAD_PAIRS):
            s = _dot(jnp.concatenate([k_pair(p), kc], axis=1), qt[p])
            if masked:
                r = lax.broadcasted_iota(jnp.int32, (n_keys, rows), 0)
                c = lax.broadcasted_iota(jnp.int32, (n_keys, rows), 1) & (t_new - 1)
                s = jnp.where(r <= c, s, NEG_INF)
            s_buf[p, :n_keys] = s
        for p in range(HEAD_PAIRS):
            s = s_buf[p, :n_keys]
            m_old = m_scr[p]
            m_new = jnp.maximum(m_old, _sublane_allreduce(_fold_rows(s, jnp.maximum), jnp.maximum))
            p_buf[p, :n_keys] = jnp.exp2(s - jnp.tile(m_new, (n_keys // SUBLANES, 1))).astype(BF16)
            alpha = jnp.exp2(m_old - m_new)
            al_buf[p] = jnp.transpose(jnp.tile(alpha, (rows // SUBLANES, 1)))
            m_scr[p] = m_new
        ones = jnp.ones((n_keys, LANES), BF16)
        for p in range(HEAD_PAIRS):
            pv = lax.dot_general(p_buf[p, :n_keys], jnp.concatenate([v_pair(p), ones], axis=1),
                                 (((0,), (0,)), ((), ())), preferred_element_type=F32)
            acc_scr[p] = acc_scr[p] * jnp.tile(al_buf[p], (1, 2)) + pv

    def cache_pair(ref):
        return lambda p: ref[0, :, p * PAIR_LANES:(p + 1) * PAIR_LANES].astype(BF16)

    def new_pair(ref):
        return lambda p: ref[:, p * PAIR_LANES:(p + 1) * PAIR_LANES]

    @pl.when(j < n_cache_tiles)
    def _():
        c = _tri_cumsum(lfc_ref[0], tk) + carry[0:1, :]
        carry[...] = jnp.broadcast_to(c[tk - 1:tk, :], carry.shape)
        attend(cache_pair(kc_ref), cache_pair(vc_ref), tk, _c_pieces(c), False)

    @pl.when(j == n_cache_tiles)
    def _():
        c = _tri_cumsum(lfn_ref[...], t_new) + carry[0:1, :]
        attend(new_pair(kn_ref), new_pair(vn_ref), t_new, _c_pieces(c), True)
        lane = lax.broadcasted_iota(jnp.int32, (t_new, LANES), 1)
        for p in range(HEAD_PAIRS):
            acc = acc_scr[p]
            oa = acc[:t_new, :LANES] / acc[:t_new, LANES:]
            ob = acc[t_new:, :LANES] / acc[t_new:, LANES:]
            o_ref[:, p * PAIR_LANES:(p + 1) * PAIR_LANES] = jnp.where(lane < FOX_HEAD_DIM, oa, ob).astype(o_ref.dtype)


def _attn_sample(qb, kb, vb, c3, logf, cache_k, cache_v, cache_logf, tk):
    n_seq, past, _ = cache_k.shape
    t_new = qb.shape[0] // n_seq
    nct = past // tk
    pad = lambda a: jnp.pad(a, ((0, 0),) * (a.ndim - 1) + ((0, LANES - FOX_HEADS),))
    new_tile = pl.BlockSpec((t_new, D_MODEL), lambda b, j: (b, 0))
    new_c3 = pl.BlockSpec((t_new, LANES), lambda b, j: (b, 0))
    cache_tile = pl.BlockSpec((1, tk, D_MODEL), lambda b, j: (b, jnp.minimum(j, nct - 1), 0))
    lf_tile = pl.BlockSpec((1, tk, LANES), lambda b, j: (b, jnp.minimum(j, nct - 1), 0))
    return pl.pallas_call(
        functools.partial(_attn_sample_kernel, nct, tk, t_new),
        grid=(n_seq, nct + 1),
        in_specs=[new_tile, new_c3, cache_tile, cache_tile, lf_tile, new_tile, new_tile, new_c3],
        out_specs=new_tile,
        out_shape=jax.ShapeDtypeStruct((n_seq * t_new, D_MODEL), BF16),
        scratch_shapes=[pltpu.VMEM((HEAD_PAIRS, 2 * LANES, 2 * t_new), BF16),
                        pltpu.VMEM((HEAD_PAIRS, SUBLANES, 2 * t_new), F32),
                        pltpu.VMEM((HEAD_PAIRS, 2 * t_new, 2 * LANES), F32),
                        pltpu.VMEM((8, LANES), F32),
                        pltpu.VMEM((HEAD_PAIRS, tk, 2 * t_new), F32),
                        pltpu.VMEM((HEAD_PAIRS, tk, 2 * t_new), BF16),
                        pltpu.VMEM((HEAD_PAIRS, 2 * t_new, 2 * t_new), F32)],
        compiler_params=_cparams(2), name="fox_attn_sample",
    )(qb, c3, cache_k, cache_v, pad(cache_logf), kb, vb, pad(logf))


def _trunk(x, n_seq, seq_len, ssm_h0, fox_cache, conv_hist, w, tm, tq):
    t = n_seq * seq_len
    nb = seq_len // SSM_BLOCK
    nc = SSM_GROUPS // COLUMN_GROUPS
    half = COLUMN_GROUPS * SSM_STATE
    x = x.reshape(t, D_MODEL)

    u0 = _norm(x, w["norm_mix"][0], tm)
    if ssm_h0 is None:
        h0 = None
    else:
        per_c = lambda h: jnp.swapaxes(h.reshape(n_seq, nc, half), 0, 1)
        h0c = jnp.concatenate([per_c(ssm_h0[0]), per_c(ssm_h0[1])], axis=2)
        h0 = jnp.zeros((nc, n_seq, nb, 2 * half), F32).at[:, :, 0, :].set(h0c).reshape(nc, n_seq * nb, 2 * half)
    ys, fin = _s5_core(u0, w["s5"], n_seq, seq_len, h0, fold_sequences=ssm_h0 is not None)
    per_g = lambda f: jnp.swapaxes(f.reshape(nc, n_seq, COLUMN_GROUPS, SSM_STATE), 0, 1).reshape(
        n_seq, SSM_GROUPS, SSM_STATE)
    fre, fim = per_g(fin[:, :, :half]), per_g(fin[:, :, half:])
    x, conv0 = _ffn(x, "glu", (ys, w["norm_mix"][0], w["ssm_d"], w["w_glu"]), conv_hist[0],
                    w["norm_ffn"][0], w["w_up"][0], w["w_gate"][0], w["conv_w"][0], w["conv_b"][0],
                    w["w_down"][0], None, seq_len, tm)

    k, v, logf, qb, kb, vb, c3, kc = _qkv(x, w["norm_mix"][1], w["wq"], w["wk"], w["wv"], w["wf"], w["bf"],
                                          seq_len, tq, transposed_v=fox_cache is None)
    if fox_cache is None:
        o = _attn_prompt(qb, kb, kc, vb, c3, n_seq, seq_len, tq)
    else:
        o = _attn_sample(qb, kb, vb, c3, logf, *fox_cache, tk=min(CACHE_TILE, fox_cache[0].shape[1]))
    y, conv1 = _ffn(x, "oproj", (o, w["w_o"]), conv_hist[1],
                    w["norm_ffn"][1], w["w_up"][1], w["w_gate"][1], w["conv_w"][1], w["conv_b"][1],
                    w["w_down"][1], w["norm_final"], seq_len, tm)

    shape4 = (1, n_seq, seq_len, FOX_HEADS, FOX_HEAD_DIM)
    return (y.reshape(n_seq, seq_len, D_MODEL),
            fre[None], fim[None],
            k.reshape(shape4), v.reshape(shape4), logf.reshape(1, n_seq, seq_len, FOX_HEADS),
            jnp.stack([conv0, conv1]))


def kernel(x_prompt, x_sample, state_ssm_re, state_ssm_im, cache_fox_k, cache_fox_v, cache_fox_logf, state_ffn_conv, norm_mix, norm_ffn, norm_final, ssm_a_re, ssm_a_im, ssm_log_step, ssm_b_re, ssm_b_im, ssm_c_re, ssm_c_im, ssm_d, ssm_w_glu, fox_w_qkvf, fox_b_f, fox_w_o, ffn_w_up, ffn_w_gate, ffn_conv_w, ffn_conv_b, ffn_w_down):
    n_p, l_p, _ = x_prompt.shape
    n_s, l_s, _ = x_sample.shape
    past = cache_fox_k.shape[2]
    wqkvf = fox_w_qkvf[0]
    w = {
        "norm_mix": norm_mix, "norm_ffn": norm_ffn, "norm_final": norm_final,
        "s5": _s5_block_diagonal(_s5_params(ssm_a_re[0], ssm_a_im[0], ssm_log_step[0], ssm_b_re[0],
                                            ssm_b_im[0], ssm_c_re[0], ssm_c_im[0])),
        "ssm_d": ssm_d[0], "w_glu": ssm_w_glu[0].astype(BF16),
        "wq": wqkvf[:, :D_MODEL].astype(BF16), "wk": wqkvf[:, D_MODEL:2 * D_MODEL].astype(BF16),
        "wv": wqkvf[:, 2 * D_MODEL:3 * D_MODEL].astype(BF16),
        "wf": jnp.pad(wqkvf[:, 3 * D_MODEL:], ((0, 0), (0, LANES - FOX_HEADS))).astype(BF16),
        "bf": jnp.pad(fox_b_f[0], (0, LANES - FOX_HEADS))[None, :],
        "w_o": fox_w_o[0].astype(BF16),
        "w_up": ffn_w_up.astype(BF16), "w_gate": ffn_w_gate.astype(BF16), "w_down": ffn_w_down.astype(BF16),
        "conv_w": ffn_conv_w, "conv_b": ffn_conv_b,
    }
    zero_hist = jnp.zeros((2, n_p, CONV_WIDTH - 1, D_FF), F32)
    outs_p = _trunk(x_prompt, n_p, l_p, None, None, zero_hist, w, tm=PROMPT_TILE, tq=PROMPT_TILE)
    cache = (cache_fox_k[0].reshape(n_s, past, D_MODEL), cache_fox_v[0].reshape(n_s, past, D_MODEL),
             cache_fox_logf[0])
    outs_s = _trunk(x_sample, n_s, l_s, (state_ssm_re[0], state_ssm_im[0]), cache, state_ffn_conv, w,
                    tm=SAMPLE_SEQS_PER_TILE * l_s, tq=l_s)
    return (outs_p[0], outs_s[0]) + outs_p[1:] + outs_s[1:]
```
